```python
import jax
import jax.numpy as jnp
from jax import lax
import numpy as np

D_MODEL = 1024
BATCH = 8
SEQ = 4096
DEPTH = 1

HEAD_DIM = 64
ATTN_WIDTH = D_MODEL // 2
N_ATTN_HEADS = ATTN_WIDTH // HEAD_DIM
N_KV_HEADS = 2
KV_WIDTH = N_KV_HEADS * HEAD_DIM
WINDOW = 128
BLOCK = 128
ROPE_THETA = 500000.0
ROTARY_DIM = HEAD_DIM // 4
POOL_WINDOWS = (2, 4, 8, 16)
N_POOL_GROUPS = len(POOL_WINDOWS)
POOL_WIDTH = D_MODEL // 2
POOL_GROUP_WIDTH = POOL_WIDTH // N_POOL_GROUPS
MIX_WIDTH = ATTN_WIDTH + POOL_WIDTH
IN_WIDTH = ATTN_WIDTH + 2 * KV_WIDTH + POOL_WIDTH
D_FF = 2816
EPS = 1e-6

kernel_name = 'hybrid_window_gqa_multiscale_pool_macaron'


def rms_norm(x, g):
    xf = x.astype(jnp.float32)
    y = xf * lax.rsqrt(jnp.mean(xf * xf, axis=-1, keepdims=True) + EPS)
    return (y * g.astype(jnp.float32)).astype(x.dtype)


def swiglu(x, w_gate, w_up, w_down):
    return (jax.nn.silu(x @ w_gate) * (x @ w_up)) @ w_down


def rope_tables(positions):
    inv_freq = ROPE_THETA ** (-jnp.arange(0, ROTARY_DIM, 2, dtype=jnp.float32) / ROTARY_DIM)
    ang = positions.astype(jnp.float32)[:, None] * inv_freq[None, :]
    emb = jnp.concatenate([ang, ang], axis=-1)
    return jnp.cos(emb)[None, :, None, :], jnp.sin(emb)[None, :, None, :]


def apply_partial_rope(t, cos, sin):
    tf = t.astype(jnp.float32)
    rot, rest = tf[..., :ROTARY_DIM], tf[..., ROTARY_DIM:]
    half = ROTARY_DIM // 2
    rot_half = jnp.concatenate([-rot[..., half:], rot[..., :half]], axis=-1)
    rot = rot * cos + rot_half * sin
    return jnp.concatenate([rot, rest], axis=-1).astype(t.dtype)


def window_attention(q, k, v, sink):
    B, S, H, D = q.shape
    nb = S // BLOCK
    G = H // N_KV_HEADS
    qb = q.reshape(B, nb, BLOCK, N_KV_HEADS, G, D)

    def bands(t):
        tp = jnp.pad(t, ((0, 0), (BLOCK, BLOCK), (0, 0), (0, 0)))
        tb = tp.reshape(B, nb + 2, BLOCK, N_KV_HEADS, D)
        return jnp.concatenate([tb[:, :-2], tb[:, 1:-1], tb[:, 2:]], axis=2)

    kb, vb = bands(k), bands(v)
    scores = jnp.einsum('bnqkgd,bnskd->bnkgqs', qb, kb,
                        preferred_element_type=jnp.float32) * (D ** -0.5)
    qpos = jnp.arange(nb)[:, None] * BLOCK + jnp.arange(BLOCK)[None, :]
    kpos = jnp.arange(nb)[:, None] * BLOCK + jnp.arange(3 * BLOCK)[None, :] - BLOCK
    dist = qpos[:, :, None] - kpos[:, None, :]
    valid = (jnp.abs(dist) <= WINDOW) & (kpos[:, None, :] >= 0) & (kpos[:, None, :] < S)
    scores = jnp.where(valid[None, :, None, None], scores, -1e30)
    sink_l = sink.astype(jnp.float32).reshape(N_KV_HEADS, G)[None, None, :, :, None, None]
    m = jnp.maximum(jnp.max(scores, axis=-1, keepdims=True), sink_l)
    p = jnp.exp(scores - m)
    denom = jnp.sum(p, axis=-1, keepdims=True) + jnp.exp(sink_l - m)
    p = (p / denom).astype(v.dtype)
    out = jnp.einsum('bnkgqs,bnskd->bnqkgd', p, vb)
    return out.reshape(B, S, H * D)


def multiscale_pool(u, w_pool, pool_scale):
    B, S, _ = u.shape
    uf = u.astype(jnp.float32)
    csum = jnp.concatenate([jnp.zeros((B, 1, POOL_WIDTH), jnp.float32),
                            jnp.cumsum(uf, axis=1)], axis=1)
    t = jnp.arange(S)
    means = []
    for gi, w in enumerate(POOL_WINDOWS):
        cg = csum[..., gi * POOL_GROUP_WIDTH:(gi + 1) * POOL_GROUP_WIDTH]
        half = w // 2

        def win_mean(lo, hi):
            a = jnp.clip(lo, 0, S)
            b = jnp.clip(hi + 1, 0, S)
            s = jnp.take(cg, b, axis=1) - jnp.take(cg, a, axis=1)
            return s / (b - a).astype(jnp.float32)[None, :, None]

        means.append(0.5 * (win_mean(t - half, t + half - 1) + win_mean(t - half + 1, t + half)))
    mean = jnp.concatenate(means, axis=-1)
    d = (mean - uf).reshape(B, S, N_POOL_GROUPS, POOL_GROUP_WIDTH)
    y = jnp.einsum('bsgc,gcd->bsgd', d, w_pool.astype(jnp.float32)).reshape(B, S, POOL_WIDTH)
    return (y * pool_scale.astype(jnp.float32)).astype(u.dtype)


def setup_inputs(seed: int = 0) -> dict:
    key = jax.random.key(seed)
    ks = jax.random.split(key, 18)
    f32 = jnp.float32
    L = DEPTH

    def nrm(k, shape, scale):
        return jax.random.normal(k, shape, f32) * scale

    def gain(k, shape, s=0.05):
        return 1.0 + s * jax.random.normal(k, shape, f32)

    return {
        'x': nrm(ks[0], (BATCH, SEQ, D_MODEL), 1.0),
        'ffn1_norm': gain(ks[1], (L, D_MODEL)),
        'ffn1_w_gate': nrm(ks[2], (L, D_MODEL, D_FF), D_MODEL ** -0.5),
        'ffn1_w_up': nrm(ks[3], (L, D_MODEL, D_FF), D_MODEL ** -0.5),
        'ffn1_w_down': nrm(ks[4], (L, D_FF, D_MODEL), D_FF ** -0.5),
        'mix_norm': gain(ks[5], (L, D_MODEL)),
        'w_in': nrm(ks[6], (L, D_MODEL, IN_WIDTH), D_MODEL ** -0.5),
        'sink_logits': nrm(ks[7], (L, N_ATTN_HEADS), 0.5),
        'pool_w': nrm(ks[8], (L, N_POOL_GROUPS, POOL_GROUP_WIDTH, POOL_GROUP_WIDTH), POOL_GROUP_WIDTH ** -0.5),
        'pool_scale': gain(ks[9], (L, POOL_WIDTH), 0.1),
        'w_out': nrm(ks[10], (L, MIX_WIDTH, D_MODEL), MIX_WIDTH ** -0.5),
        'ffn2_norm': gain(ks[11], (L, D_MODEL)),
        'ffn2_w_gate': nrm(ks[12], (L, D_MODEL, D_FF), D_MODEL ** -0.5),
        'ffn2_w_up': nrm(ks[13], (L, D_MODEL, D_FF), D_MODEL ** -0.5),
        'ffn2_w_down': nrm(ks[14], (L, D_FF, D_MODEL), D_FF ** -0.5),
        'final_norm': gain(ks[15], (D_MODEL,)),
    }


def reference(x, ffn1_norm, ffn1_w_gate, ffn1_w_up, ffn1_w_down, mix_norm, w_in,
              sink_logits, pool_w, pool_scale, w_out, ffn2_norm, ffn2_w_gate,
              ffn2_w_up, ffn2_w_down, final_norm):
    B, S, _ = x.shape
    positions = jnp.arange(S, dtype=jnp.int32)
    cos, sin = rope_tables(positions)
    h = x
    for l in range(DEPTH):
        h = h + 0.5 * swiglu(rms_norm(h, ffn1_norm[l]), ffn1_w_gate[l], ffn1_w_up[l], ffn1_w_down[l])
        u = rms_norm(h, mix_norm[l]) @ w_in[l]
        q, k, v, pc = jnp.split(u, [ATTN_WIDTH, ATTN_WIDTH + KV_WIDTH, ATTN_WIDTH + 2 * KV_WIDTH], axis=-1)
        q = apply_partial_rope(q.reshape(B, S, N_ATTN_HEADS, HEAD_DIM), cos, sin)
        k = apply_partial_rope(k.reshape(B, S, N_KV_HEADS, HEAD_DIM), cos, sin)
        v = v.reshape(B, S, N_KV_HEADS, HEAD_DIM)
        a = window_attention(q, k, v, sink_logits[l])
        p = multiscale_pool(pc, pool_w[l], pool_scale[l])
        h = h + jnp.concatenate([a, p], axis=-1) @ w_out[l]
        h = h + 0.5 * swiglu(rms_norm(h, ffn2_norm[l]), ffn2_w_gate[l], ffn2_w_up[l], ffn2_w_down[l])
    return rms_norm(h, final_norm)
```

```python
import functools

import numpy as np
import jax
import jax.numpy as jnp
from jax import lax
from jax.experimental import pallas as pl
from jax.experimental.pallas import tpu as pltpu

D_MODEL = 1024
HEAD_DIM = 64
N_HEADS = 8
N_KV = 2
Q_WIDTH = N_HEADS * HEAD_DIM
KV_WIDTH = N_KV * HEAD_DIM
WINDOW = 128
QBLK = 128
KEYS = 3 * QBLK
ROPE_THETA = 500000.0
ROTARY_DIM = HEAD_DIM // 4
POOL_WINDOWS = (2, 4, 8, 16)
POOL_HALO = 8
POOL_WIDTH = 512
GROUP_W = 128
D_FF = 2816
EPS = 1e-6
LANES = 128
NEG = -1e30

TM = 512
VMEM_LIMIT = 58 * 1024 * 1024

F32 = jnp.float32
BF16 = jnp.bfloat16


def _rms(x, g):
    ms = jnp.mean(x * x, axis=-1, keepdims=True)
    return x * lax.rsqrt(ms + EPS) * g


def _swiglu(xn, wg_ref, wu_ref, wd_ref):
    gate = jnp.dot(xn, wg_ref[...], preferred_element_type=F32)
    up = jnp.dot(xn, wu_ref[...], preferred_element_type=F32)
    act = (gate * jax.nn.sigmoid(gate) * up).astype(BF16)
    return jnp.dot(act, wd_ref[...], preferred_element_type=F32)


def _rope(t, c, sa, sb):
    t_plus = pltpu.roll(t, LANES - ROTARY_DIM // 2, 1)
    t_minus = pltpu.roll(t, ROTARY_DIM // 2, 1)
    return t * c + t_plus * sa + t_minus * sb


def _ffn_in_kernel(x_ref, g1_ref, wg_ref, wu_ref, wd_ref, gm_ref, win_ref,
                   c_ref, sa_ref, sb_ref,
                   h_ref, q_ref, kt_ref, v_ref, pc_ref):
    x = x_ref[0]
    xn = _rms(x, g1_ref[...]).astype(BF16)
    h = x + 0.5 * _swiglu(xn, wg_ref, wu_ref, wd_ref)
    h_ref[0] = h
    hn = _rms(h, gm_ref[...]).astype(BF16)
    u = jnp.dot(hn, win_ref[...], preferred_element_type=F32)
    c, sa, sb = c_ref[...], sa_ref[...], sb_ref[...]
    scale = HEAD_DIM ** -0.5
    for t in range(Q_WIDTH // LANES):
        qt = _rope(u[:, t * LANES:(t + 1) * LANES], c, sa, sb)
        q_ref[0, :, t * LANES:(t + 1) * LANES] = (qt * scale).astype(BF16)
    k = _rope(u[:, Q_WIDTH:Q_WIDTH + KV_WIDTH], c, sa, sb)
    kt_ref[0] = k.T.astype(BF16)
    v_ref[0] = u[:, Q_WIDTH + KV_WIDTH:Q_WIDTH + 2 * KV_WIDTH].astype(BF16)
    pc_ref[0] = u[:, Q_WIDTH + 2 * KV_WIDTH:]


def _mix_ffn_kernel(sink_ref, h_ref, q_ref, ktp_ref, ktc_ref, ktn_ref,
                    vp_ref, vc_ref, vn_ref, pcp_ref, pcc_ref, pcn_ref,
                    bias_ref, poolw_ref, pscale_ref, wout_ref,
                    g2_ref, wg_ref, wu_ref, wd_ref, gf_ref,
                    o_ref,
                    kt_buf, vab_buf, pc_buf, d_buf, mix_buf, *, tm, seq):
    i = pl.program_id(1)
    n_tiles = seq // tm
    nq = tm // QBLK
    nb = seq // QBLK

    kt_buf[:, 0:QBLK] = ktp_ref[0]
    kt_buf[:, QBLK:QBLK + tm] = ktc_ref[0]
    kt_buf[:, QBLK + tm:] = ktn_ref[0]

    for lo, ref, n in ((0, vp_ref, QBLK), (QBLK, vc_ref, tm), (QBLK + tm, vn_ref, QBLK)):
        v = ref[0].astype(F32)
        vr = pltpu.roll(v, HEAD_DIM, 1)
        low = lax.broadcasted_iota(jnp.int32, v.shape, 1) < HEAD_DIM
        zero = jnp.zeros_like(v)
        vab_buf[0, 0, lo:lo + n] = jnp.where(low, v, zero).astype(BF16)
        vab_buf[0, 1, lo:lo + n] = jnp.where(low, zero, vr).astype(BF16)
        vab_buf[1, 0, lo:lo + n] = jnp.where(low, vr, zero).astype(BF16)
        vab_buf[1, 1, lo:lo + n] = jnp.where(low, zero, v).astype(BF16)

    low_out = lax.broadcasted_iota(jnp.int32, (QBLK, LANES), 1) < HEAD_DIM
    for j in range(nq):
        blk = i * nq + j
        variant = jnp.where(blk == 0, 0, jnp.where(blk == nb - 1, 2, 1))
        bias = bias_ref[variant]
        rows = slice(j * QBLK, (j + 1) * QBLK)
        keys = slice(j * QBLK, j * QBLK + KEYS)
        for c in range(N_KV):
            lhs = jnp.concatenate(
                [q_ref[0, rows, (2 * c) * LANES:(2 * c + 1) * LANES],
                 q_ref[0, rows, (2 * c + 1) * LANES:(2 * c + 2) * LANES]], axis=0)
            kc = kt_buf[c * HEAD_DIM:(c + 1) * HEAD_DIM, keys]
            z = jnp.zeros_like(kc)
            rhs = jnp.concatenate(
                [jnp.concatenate([kc, z], axis=1), jnp.concatenate([z, kc], axis=1)], axis=0)
            s = jnp.dot(lhs, rhs, preferred_element_type=F32) + bias
            p_rows, inv = [], []
            for r in range(2):
                p_cols = []
                for par in range(2):
                    sq = s[r * QBLK:(r + 1) * QBLK, par * KEYS:(par + 1) * KEYS]
                    sink = sink_ref[4 * c + 2 * r + par]
                    m = jnp.maximum(jnp.max(sq, axis=-1, keepdims=True), sink)
                    p = jnp.exp(sq - m)
                    den = jnp.sum(p, axis=-1, keepdims=True) + jnp.exp(sink - m)
                    p_cols.append(p.astype(BF16))
                    inv.append(1.0 / den)
                p_rows.append(jnp.concatenate(p_cols, axis=1))
            pmat = jnp.concatenate(p_rows, axis=0)
            vrhs = jnp.concatenate([vab_buf[c, 0, keys], vab_buf[c, 1, keys]], axis=0)
            o = jnp.dot(pmat, vrhs, preferred_element_type=F32)
            for r in range(2):
                sc = jnp.where(low_out, inv[2 * r], inv[2 * r + 1])
                t = 2 * c + r
                mix_buf[rows, t * LANES:(t + 1) * LANES] = (
                    o[r * QBLK:(r + 1) * QBLK] * sc).astype(BF16)

    pc_buf[0:POOL_HALO] = jnp.where(i > 0, pcp_ref[0], 0.0)
    pc_buf[POOL_HALO:POOL_HALO + tm] = pcc_ref[0]
    pc_buf[POOL_HALO + tm:] = jnp.where(i < n_tiles - 1, pcn_ref[0], 0.0)

    for g, w in enumerate(POOL_WINDOWS):
        half = w // 2
        lanes = slice(g * GROUP_W, (g + 1) * GROUP_W)
        ends = (pc_buf[POOL_HALO - half:POOL_HALO - half + tm, lanes]
                + pc_buf[POOL_HALO + half:POOL_HALO + half + tm, lanes])
        inner = pc_buf[POOL_HALO:POOL_HALO + tm, lanes]
        center = inner
        for k in range(1, half):
            inner = inner + (pc_buf[POOL_HALO - k:POOL_HALO - k + tm, lanes]
                             + pc_buf[POOL_HALO + k:POOL_HALO + k + tm, lanes])
        d_buf[:, lanes] = (ends + 2.0 * inner) * (0.5 / w) - center

    def _edge_fix(row0):
        tpos = i * tm + row0 + lax.broadcasted_iota(jnp.int32, (POOL_HALO, GROUP_W), 0)
        base = POOL_HALO + row0
        for g, w in enumerate(POOL_WINDOWS):
            half = w // 2
            lanes = slice(g * GROUP_W, (g + 1) * GROUP_W)

            def wsum(lo, hi):
                acc = pc_buf[base + lo:base + lo + POOL_HALO, lanes]
                for k in range(lo + 1, hi + 1):
                    acc = acc + pc_buf[base + k:base + k + POOL_HALO, lanes]
                return acc

            def count(lo, hi):
                a = jnp.clip(tpos + lo, 0, seq)
                b = jnp.clip(tpos + hi + 1, 0, seq)
                return (b - a).astype(F32)

            mean = 0.5 * (wsum(-half, half - 1) / count(-half, half - 1)
                          + wsum(-half + 1, half) / count(-half + 1, half))
            d_buf[row0:row0 + POOL_HALO, lanes] = mean - pc_buf[base:base + POOL_HALO, lanes]

    @pl.when(i == 0)
    def _():
        _edge_fix(0)

    @pl.when(i == n_tiles - 1)
    def _():
        _edge_fix(tm - POOL_HALO)

    for g in range(len(POOL_WINDOWS)):
        lanes = slice(g * GROUP_W, (g + 1) * GROUP_W)
        y = jnp.dot(d_buf[:, lanes].astype(BF16), poolw_ref[g], preferred_element_type=F32)
        mix_buf[:, Q_WIDTH + g * GROUP_W:Q_WIDTH + (g + 1) * GROUP_W] = (
            y * pscale_ref[:, lanes]).astype(BF16)

    h = h_ref[0] + jnp.dot(mix_buf[...], wout_ref[...], preferred_element_type=F32)
    hn = _rms(h, g2_ref[...]).astype(BF16)
    h = h + 0.5 * _swiglu(hn, wg_ref, wu_ref, wd_ref)
    o_ref[0] = _rms(h, gf_ref[...])


def _rope_tables(seq):
    pos = jnp.arange(seq, dtype=jnp.int32)
    inv_freq = ROPE_THETA ** (-jnp.arange(0, ROTARY_DIM, 2, dtype=F32) / ROTARY_DIM)
    ang = pos.astype(F32)[:, None] * inv_freq[None, :]
    cos, sin = jnp.cos(ang), jnp.sin(ang)
    half = ROTARY_DIM // 2
    rest = HEAD_DIM - ROTARY_DIM
    c = jnp.concatenate([cos, cos, jnp.ones((seq, rest), F32)], axis=1)
    sa = jnp.concatenate([-sin, jnp.zeros((seq, HEAD_DIM - half), F32)], axis=1)
    sb = jnp.concatenate([jnp.zeros((seq, half), F32), sin, jnp.zeros((seq, rest), F32)], axis=1)
    rep = LANES // HEAD_DIM
    return jnp.tile(c, (1, rep)), jnp.tile(sa, (1, rep)), jnp.tile(sb, (1, rep))


def _band_bias(seq):
    r = np.arange(QBLK)[:, None]
    s = np.arange(KEYS)[None, :]
    band = (s - r >= 0) & (s - r <= 2 * WINDOW)
    first = band & (s >= QBLK)
    last = band & (s < 2 * QBLK)
    out = np.stack([np.tile(np.where(m, 0.0, NEG), (2, 2)) for m in (first, band, last)])
    return jnp.asarray(out, dtype=F32)


def _const_spec(shape):
    nd = len(shape)
    return pl.BlockSpec(shape, lambda b, i: (0,) * nd, pipeline_mode=pl.Buffered(1))


def kernel(x, ffn1_norm, ffn1_w_gate, ffn1_w_up, ffn1_w_down, mix_norm, w_in, sink_logits,
           pool_w, pool_scale, w_out, ffn2_norm, ffn2_w_gate, ffn2_w_up, ffn2_w_down, final_norm):
    B, S, D = x.shape
    assert D == D_MODEL and S % TM == 0 and TM % QBLK == 0 and ffn1_norm.shape[0] == 1
    tm = TM
    grid = (B, S // tm)
    in_width = w_in.shape[-1]
    params = pltpu.CompilerParams(dimension_semantics=("arbitrary", "arbitrary"),
                                  vmem_limit_bytes=VMEM_LIMIT)

    c_tab, sa_tab, sb_tab = _rope_tables(S)
    row = lambda g: g.reshape(1, -1).astype(F32)
    tile_spec = lambda width: pl.BlockSpec((1, tm, width), lambda b, i: (b, i, 0))
    tab_spec = pl.BlockSpec((tm, LANES), lambda b, i: (i, 0))

    h1, q, kt, v, pc = pl.pallas_call(
        _ffn_in_kernel,
        grid=grid,
        in_specs=[
            tile_spec(D),
            _const_spec((1, D)),
            _const_spec((D, D_FF)), _const_spec((D, D_FF)), _const_spec((D_FF, D)),
            _const_spec((1, D)),
            _const_spec((D, in_width)),
            tab_spec, tab_spec, tab_spec,
        ],
        out_specs=[
            tile_spec(D),
            tile_spec(Q_WIDTH),
            pl.BlockSpec((1, KV_WIDTH, tm), lambda b, i: (b, 0, i)),
            tile_spec(KV_WIDTH),
            tile_spec(POOL_WIDTH),
        ],
        out_shape=[
            jax.ShapeDtypeStruct((B, S, D), F32),
            jax.ShapeDtypeStruct((B, S, Q_WIDTH), BF16),
            jax.ShapeDtypeStruct((B, KV_WIDTH, S), BF16),
            jax.ShapeDtypeStruct((B, S, KV_WIDTH), BF16),
            jax.ShapeDtypeStruct((B, S, POOL_WIDTH), F32),
        ],
        compiler_params=params,
        name="ffn1_inproj",
    )(x, row(ffn1_norm[0]), ffn1_w_gate[0].astype(BF16), ffn1_w_up[0].astype(BF16),
      ffn1_w_down[0].astype(BF16), row(mix_norm[0]), w_in[0].astype(BF16),
      c_tab, sa_tab, sb_tab)

    qb = tm // QBLK
    pb = tm // POOL_HALO
    n_qb = S // QBLK
    n_pb = S // POOL_HALO
    prev_q = lambda i: jnp.maximum(i * qb - 1, 0)
    next_q = lambda i: jnp.minimum((i + 1) * qb, n_qb - 1)
    prev_p = lambda i: jnp.maximum(i * pb - 1, 0)
    next_p = lambda i: jnp.minimum((i + 1) * pb, n_pb - 1)

    out = pl.pallas_call(
        functools.partial(_mix_ffn_kernel, tm=tm, seq=S),
        grid=grid,
        in_specs=[
            pl.BlockSpec(memory_space=pltpu.SMEM),
            tile_spec(D),
            tile_spec(Q_WIDTH),
            pl.BlockSpec((1, KV_WIDTH, QBLK), lambda b, i: (b, 0, prev_q(i))),
            pl.BlockSpec((1, KV_WIDTH, tm), lambda b, i: (b, 0, i)),
            pl.BlockSpec((1, KV_WIDTH, QBLK), lambda b, i: (b, 0, next_q(i))),
            pl.BlockSpec((1, QBLK, KV_WIDTH), lambda b, i: (b, prev_q(i), 0)),
            tile_spec(KV_WIDTH),
            pl.BlockSpec((1, QBLK, KV_WIDTH), lambda b, i: (b, next_q(i), 0)),
            pl.BlockSpec((1, POOL_HALO, POOL_WIDTH), lambda b, i: (b, prev_p(i), 0)),
            tile_spec(POOL_WIDTH),
            pl.BlockSpec((1, POOL_HALO, POOL_WIDTH), lambda b, i: (b, next_p(i), 0)),
            _const_spec((3, 2 * QBLK, 2 * KEYS)),
            _const_spec((len(POOL_WINDOWS), GROUP_W, GROUP_W)),
            _const_spec((1, POOL_WIDTH)),
            _const_spec((D, D)),
            _const_spec((1, D)),
            _const_spec((D, D_FF)), _const_spec((D, D_FF)), _const_spec((D_FF, D)),
            _const_spec((1, D)),
        ],
        out_specs=tile_spec(D),
        out_shape=jax.ShapeDtypeStruct((B, S, D), x.dtype),
        scratch_shapes=[
            pltpu.VMEM((KV_WIDTH, tm + 2 * QBLK), BF16),
            pltpu.VMEM((N_KV, 2, tm + 2 * QBLK, LANES), BF16),
            pltpu.VMEM((tm + 2 * POOL_HALO, POOL_WIDTH), F32),
            pltpu.VMEM((tm, POOL_WIDTH), F32),
            pltpu.VMEM((tm, D), BF16),
        ],
        compiler_params=params,
        name="mix_ffn2",
    )(sink_logits[0].astype(F32), h1, q, kt, kt, kt, v, v, v, pc, pc, pc,
      _band_bias(S), pool_w[0].astype(BF16), row(pool_scale[0]), w_out[0].astype(BF16),
      row(ffn2_norm[0]), ffn2_w_gate[0].astype(BF16), ffn2_w_up[0].astype(BF16),
      ffn2_w_down[0].astype(BF16), row(final_norm))
    return out
```

```python
import functools

import numpy as np
import jax
import jax.numpy as jnp
from jax import lax
from jax.experimental import pallas as pl
from jax.experimental.pallas import tpu as pltpu

D_MODEL = 1024
HEAD_DIM = 64
N_HEADS = 8
N_KV = 2
Q_WIDTH = N_HEADS * HEAD_DIM
KV_WIDTH = N_KV * HEAD_DIM
WINDOW = 128
QBLK = 128
KEYS = 3 * QBLK
ROPE_THETA = 500000.0
ROTARY_DIM = HEAD_DIM // 4
POOL_WINDOWS = (2, 4, 8, 16)
POOL_HALO = 8
POOL_WIDTH = 512
GROUP_W = 128
D_FF = 2816
EPS = 1e-6
LANES = 128
NEG = -1e30
LOG2E = 1.4426950408889634

TM = 512
VMEM_LIMIT = 58 * 1024 * 1024

F32 = jnp.float32
BF16 = jnp.bfloat16


def _rms(x, g):
    ms = jnp.mean(x * x, axis=-1, keepdims=True)
    return x * lax.rsqrt(ms + EPS) * g


def _swiglu(xn, wg_ref, wu_ref, wd_ref):
    gate = jnp.dot(xn, wg_ref[...], preferred_element_type=F32)
    up = jnp.dot(xn, wu_ref[...], preferred_element_type=F32)
    act = (gate * jax.nn.sigmoid(gate) * up).astype(BF16)
    return jnp.dot(act, wd_ref[...], preferred_element_type=F32)


def _rope(t, c, sa, sb):
    t_plus = pltpu.roll(t, LANES - ROTARY_DIM // 2, 1)
    t_minus = pltpu.roll(t, ROTARY_DIM // 2, 1)
    return t * c + t_plus * sa + t_minus * sb


def _ffn_in_kernel(x_ref, g1_ref, wg_ref, wu_ref, wd_ref, gm_ref, win_ref,
                   c_ref, sa_ref, sb_ref,
                   h_ref, q_ref, kt_ref, v_ref, pc_ref):
    x = x_ref[0]
    xn = _rms(x, g1_ref[...]).astype(BF16)
    h = x + 0.5 * _swiglu(xn, wg_ref, wu_ref, wd_ref)
    h_ref[0] = h
    hn = _rms(h, gm_ref[...]).astype(BF16)
    u = jnp.dot(hn, win_ref[...], preferred_element_type=F32)
    c, sa, sb = c_ref[...], sa_ref[...], sb_ref[...]
    scale = HEAD_DIM ** -0.5 * LOG2E
    for t in range(Q_WIDTH // LANES):
        qt = _rope(u[:, t * LANES:(t + 1) * LANES], c, sa, sb)
        q_ref[0, :, t * LANES:(t + 1) * LANES] = (qt * scale).astype(BF16)
    k = _rope(u[:, Q_WIDTH:Q_WIDTH + KV_WIDTH], c, sa, sb)
    kt_ref[0] = k.T.astype(BF16)
    v_ref[0] = u[:, Q_WIDTH + KV_WIDTH:Q_WIDTH + 2 * KV_WIDTH].astype(BF16)
    pc_ref[0] = u[:, Q_WIDTH + 2 * KV_WIDTH:]


def _mix_ffn_kernel(sink_ref, h_ref, q_ref, ktp_ref, ktc_ref, ktn_ref,
                    vp_ref, vc_ref, vn_ref, pcp_ref, pcc_ref, pcn_ref,
                    bias_ref, eye_ref, poolw_ref, pscale_ref, wout_ref,
                    g2_ref, wg_ref, wu_ref, wd_ref, gf_ref,
                    o_ref,
                    kt_buf, vab_buf, pc_buf, d_buf, mix_buf, *, tm, seq):
    i = pl.program_id(1)
    n_tiles = seq // tm
    nq = tm // QBLK
    nb = seq // QBLK

    kt_buf[:, 0:QBLK] = ktp_ref[0]
    kt_buf[:, QBLK:QBLK + tm] = ktc_ref[0]
    kt_buf[:, QBLK + tm:] = ktn_ref[0]

    for lo, ref, n in ((0, vp_ref, QBLK), (QBLK, vc_ref, tm), (QBLK + tm, vn_ref, QBLK)):
        v = ref[0].astype(F32)
        vr = pltpu.roll(v, HEAD_DIM, 1)
        low = lax.broadcasted_iota(jnp.int32, v.shape, 1) < HEAD_DIM
        zero = jnp.zeros_like(v)
        ones_low = jnp.where(low, 1.0, 0.0).astype(BF16)
        ones_high = jnp.where(low, 0.0, 1.0).astype(BF16)
        vab_buf[0, 0, lo:lo + n, 0:LANES] = jnp.where(low, v, zero).astype(BF16)
        vab_buf[0, 1, lo:lo + n, 0:LANES] = jnp.where(low, zero, vr).astype(BF16)
        vab_buf[1, 0, lo:lo + n, 0:LANES] = jnp.where(low, vr, zero).astype(BF16)
        vab_buf[1, 1, lo:lo + n, 0:LANES] = jnp.where(low, zero, v).astype(BF16)
        for c in range(N_KV):
            vab_buf[c, 0, lo:lo + n, LANES:2 * LANES] = ones_low
            vab_buf[c, 1, lo:lo + n, LANES:2 * LANES] = ones_high

    low_out = lax.broadcasted_iota(jnp.int32, (QBLK, LANES), 1) < HEAD_DIM
    eye = eye_ref[...]
    for j in range(nq):
        blk = i * nq + j
        variant = jnp.where(blk == 0, 0, jnp.where(blk == nb - 1, 2, 1))
        bias_t = bias_ref[variant]
        rows = slice(j * QBLK, (j + 1) * QBLK)
        keys = slice(j * QBLK, j * QBLK + KEYS)
        for c in range(N_KV):
            lhs = jnp.concatenate(
                [jnp.concatenate([q_ref[0, rows, (2 * c) * LANES:(2 * c + 1) * LANES], eye], axis=1),
                 jnp.concatenate([q_ref[0, rows, (2 * c + 1) * LANES:(2 * c + 2) * LANES], eye],
                                 axis=1)], axis=0)
            kc = kt_buf[c * HEAD_DIM:(c + 1) * HEAD_DIM, keys]
            z = jnp.zeros_like(kc)
            rhs = jnp.concatenate(
                [jnp.concatenate([kc, z], axis=1), jnp.concatenate([z, kc], axis=1), bias_t],
                axis=0)
            s = jnp.dot(lhs, rhs, preferred_element_type=F32)
            p_rows, sink_terms = [], []
            for r in range(2):
                p_cols = []
                for par in range(2):
                    sq = s[r * QBLK:(r + 1) * QBLK, par * KEYS:(par + 1) * KEYS]
                    sink = sink_ref[4 * c + 2 * r + par] * LOG2E
                    m = jnp.maximum(jnp.max(sq, axis=-1, keepdims=True), sink)
                    p_cols.append(jnp.exp2(sq - m).astype(BF16))
                    sink_terms.append(jnp.exp2(sink - m))
                p_rows.append(jnp.concatenate(p_cols, axis=1))
            pmat = jnp.concatenate(p_rows, axis=0)
            vrhs = jnp.concatenate([vab_buf[c, 0, keys], vab_buf[c, 1, keys]], axis=0)
            o = jnp.dot(pmat, vrhs, preferred_element_type=F32)
            for r in range(2):
                orow = o[r * QBLK:(r + 1) * QBLK]
                den = orow[:, LANES:] + jnp.where(low_out, sink_terms[2 * r], sink_terms[2 * r + 1])
                t = 2 * c + r
                mix_buf[rows, t * LANES:(t + 1) * LANES] = (orow[:, :LANES] / den).astype(BF16)

    pc_buf[0:POOL_HALO] = jnp.where(i > 0, pcp_ref[0], 0.0)
    pc_buf[POOL_HALO:POOL_HALO + tm] = pcc_ref[0]
    pc_buf[POOL_HALO + tm:] = jnp.where(i < n_tiles - 1, pcn_ref[0], 0.0)

    for g, w in enumerate(POOL_WINDOWS):
        half = w // 2
        lanes = slice(g * GROUP_W, (g + 1) * GROUP_W)
        ends = (pc_buf[POOL_HALO - half:POOL_HALO - half + tm, lanes]
                + pc_buf[POOL_HALO + half:POOL_HALO + half + tm, lanes])
        inner = pc_buf[POOL_HALO:POOL_HALO + tm, lanes]
        center = inner
        for k in range(1, half):
            inner = inner + (pc_buf[POOL_HALO - k:POOL_HALO - k + tm, lanes]
                             + pc_buf[POOL_HALO + k:POOL_HALO + k + tm, lanes])
        d_buf[:, lanes] = (ends + 2.0 * inner) * (0.5 / w) - center

    for row0, clipped in ((0, i == 0), (tm - POOL_HALO, i == n_tiles - 1)):
        tpos = i * tm + row0 + lax.broadcasted_iota(jnp.int32, (POOL_HALO, GROUP_W), 0)
        base = POOL_HALO + row0
        for g, w in enumerate(POOL_WINDOWS):
            half = w // 2
            lanes = slice(g * GROUP_W, (g + 1) * GROUP_W)

            def wsum(lo, hi):
                acc = pc_buf[base + lo:base + lo + POOL_HALO, lanes]
                for k in range(lo + 1, hi + 1):
                    acc = acc + pc_buf[base + k:base + k + POOL_HALO, lanes]
                return acc

            def count(lo, hi):
                a = jnp.clip(tpos + lo, 0, seq)
                b = jnp.clip(tpos + hi + 1, 0, seq)
                return (b - a).astype(F32)

            mean = 0.5 * (wsum(-half, half - 1) / count(-half, half - 1)
                          + wsum(-half + 1, half) / count(-half + 1, half))
            fixed = mean - pc_buf[base:base + POOL_HALO, lanes]
            d_buf[row0:row0 + POOL_HALO, lanes] = jnp.where(
                clipped, fixed, d_buf[row0:row0 + POOL_HALO, lanes])

    for g in range(len(POOL_WINDOWS)):
        lanes = slice(g * GROUP_W, (g + 1) * GROUP_W)
        y = jnp.dot(d_buf[:, lanes].astype(BF16), poolw_ref[g], preferred_element_type=F32)
        mix_buf[:, Q_WIDTH + g * GROUP_W:Q_WIDTH + (g + 1) * GROUP_W] = (
            y * pscale_ref[:, lanes]).astype(BF16)

    h = h_ref[0] + jnp.dot(mix_buf[...], wout_ref[...], preferred_element_type=F32)
    hn = _rms(h, g2_ref[...]).astype(BF16)
    h = h + 0.5 * _swiglu(hn, wg_ref, wu_ref, wd_ref)
    o_ref[0] = _rms(h, gf_ref[...])


def _rope_tables(seq):
    pos = jnp.arange(seq, dtype=jnp.int32)
    inv_freq = ROPE_THETA ** (-jnp.arange(0, ROTARY_DIM, 2, dtype=F32) / ROTARY_DIM)
    ang = pos.astype(F32)[:, None] * inv_freq[None, :]
    cos, sin = jnp.cos(ang), jnp.sin(ang)
    half = ROTARY_DIM // 2
    rest = HEAD_DIM - ROTARY_DIM
    c = jnp.concatenate([cos, cos, jnp.ones((seq, rest), F32)], axis=1)
    sa = jnp.concatenate([-sin, jnp.zeros((seq, HEAD_DIM - half), F32)], axis=1)
    sb = jnp.concatenate([jnp.zeros((seq, half), F32), sin, jnp.zeros((seq, rest), F32)], axis=1)
    rep = LANES // HEAD_DIM
    return jnp.tile(c, (1, rep)), jnp.tile(sa, (1, rep)), jnp.tile(sb, (1, rep))


def _band_bias(seq):
    r = np.arange(QBLK)[:, None]
    s = np.arange(KEYS)[None, :]
    band = (s - r >= 0) & (s - r <= 2 * WINDOW)
    first = band & (s >= QBLK)
    last = band & (s < 2 * QBLK)
    out = np.stack([np.tile(np.where(m, 0.0, NEG), (1, 2)) for m in (first, band, last)])
    return jnp.asarray(out, dtype=F32).astype(BF16)


def _const_spec(shape):
    nd = len(shape)
    return pl.BlockSpec(shape, lambda b, i: (0,) * nd, pipeline_mode=pl.Buffered(1))


def kernel(x, ffn1_norm, ffn1_w_gate, ffn1_w_up, ffn1_w_down, mix_norm, w_in, sink_logits,
           pool_w, pool_scale, w_out, ffn2_norm, ffn2_w_gate, ffn2_w_up, ffn2_w_down, final_norm):
    B, S, D = x.shape
    assert D == D_MODEL and S % TM == 0 and TM % QBLK == 0 and ffn1_norm.shape[0] == 1
    tm = TM
    grid = (B, S // tm)
    in_width = w_in.shape[-1]
    params = pltpu.CompilerParams(dimension_semantics=("arbitrary", "arbitrary"),
                                  vmem_limit_bytes=VMEM_LIMIT)

    c_tab, sa_tab, sb_tab = _rope_tables(S)
    row = lambda g: g.reshape(1, -1).astype(F32)
    tile_spec = lambda width: pl.BlockSpec((1, tm, width), lambda b, i: (b, i, 0))
    tab_spec = pl.BlockSpec((tm, LANES), lambda b, i: (i, 0))

    h1, q, kt, v, pc = pl.pallas_call(
        _ffn_in_kernel,
        grid=grid,
        in_specs=[
            tile_spec(D),
            _const_spec((1, D)),
            _const_spec((D, D_FF)), _const_spec((D, D_FF)), _const_spec((D_FF, D)),
            _const_spec((1, D)),
            _const_spec((D, in_width)),
            tab_spec, tab_spec, tab_spec,
        ],
        out_specs=[
            tile_spec(D),
            tile_spec(Q_WIDTH),
            pl.BlockSpec((1, KV_WIDTH, tm), lambda b, i: (b, 0, i)),
            tile_spec(KV_WIDTH),
            tile_spec(POOL_WIDTH),
        ],
        out_shape=[
            jax.ShapeDtypeStruct((B, S, D), F32),
            jax.ShapeDtypeStruct((B, S, Q_WIDTH), BF16),
            jax.ShapeDtypeStruct((B, KV_WIDTH, S), BF16),
            jax.ShapeDtypeStruct((B, S, KV_WIDTH), BF16),
            jax.ShapeDtypeStruct((B, S, POOL_WIDTH), F32),
        ],
        compiler_params=params,
        name="ffn1_inproj",
    )(x, row(ffn1_norm[0]), ffn1_w_gate[0].astype(BF16), ffn1_w_up[0].astype(BF16),
      ffn1_w_down[0].astype(BF16), row(mix_norm[0]), w_in[0].astype(BF16),
      c_tab, sa_tab, sb_tab)

    qb = tm // QBLK
    pb = tm // POOL_HALO
    n_qb = S // QBLK
    n_pb = S // POOL_HALO
    prev_q = lambda i: jnp.maximum(i * qb - 1, 0)
    next_q = lambda i: jnp.minimum((i + 1) * qb, n_qb - 1)
    prev_p = lambda i: jnp.maximum(i * pb - 1, 0)
    next_p = lambda i: jnp.minimum((i + 1) * pb, n_pb - 1)

    out = pl.pallas_call(
        functools.partial(_mix_ffn_kernel, tm=tm, seq=S),
        grid=grid,
        in_specs=[
            pl.BlockSpec(memory_space=pltpu.SMEM),
            tile_spec(D),
            tile_spec(Q_WIDTH),
            pl.BlockSpec((1, KV_WIDTH, QBLK), lambda b, i: (b, 0, prev_q(i))),
            pl.BlockSpec((1, KV_WIDTH, tm), lambda b, i: (b, 0, i)),
            pl.BlockSpec((1, KV_WIDTH, QBLK), lambda b, i: (b, 0, next_q(i))),
            pl.BlockSpec((1, QBLK, KV_WIDTH), lambda b, i: (b, prev_q(i), 0)),
            tile_spec(KV_WIDTH),
            pl.BlockSpec((1, QBLK, KV_WIDTH), lambda b, i: (b, next_q(i), 0)),
            pl.BlockSpec((1, POOL_HALO, POOL_WIDTH), lambda b, i: (b, prev_p(i), 0)),
            tile_spec(POOL_WIDTH),
            pl.BlockSpec((1, POOL_HALO, POOL_WIDTH), lambda b, i: (b, next_p(i), 0)),
            _const_spec((3, QBLK, 2 * KEYS)),
            _const_spec((QBLK, QBLK)),
            _const_spec((len(POOL_WINDOWS), GROUP_W, GROUP_W)),
            _const_spec((1, POOL_WIDTH)),
            _const_spec((D, D)),
            _const_spec((1, D)),
            _const_spec((D, D_FF)), _const_spec((D, D_FF)), _const_spec((D_FF, D)),
            _const_spec((1, D)),
        ],
        out_specs=tile_spec(D),
        out_shape=jax.ShapeDtypeStruct((B, S, D), x.dtype),
        scratch_shapes=[
            pltpu.VMEM((KV_WIDTH, tm + 2 * QBLK), BF16),
            pltpu.VMEM((N_KV, 2, tm + 2 * QBLK, 2 * LANES), BF16),
            pltpu.VMEM((tm + 2 * POOL_HALO, POOL_WIDTH), F32),
            pltpu.VMEM((tm, POOL_WIDTH), F32),
            pltpu.VMEM((tm, D), BF16),
        ],
        compiler_params=params,
        name="mix_ffn2",
    )(sink_logits[0].astype(F32), h1, q, kt, kt, kt, v, v, v, pc, pc, pc,
      _band_bias(S), jnp.eye(QBLK, dtype=BF16), pool_w[0].astype(BF16), row(pool_scale[0]), w_out[0].astype(BF16),
      row(ffn2_norm[0]), ffn2_w_gate[0].astype(BF16), ffn2_w_up[0].astype(BF16),
      ffn2_w_down[0].astype(BF16), row(final_norm))
    return out
```

```python
import functools

import numpy as np
import jax
import jax.numpy as jnp
from jax import lax
from jax.experimental import pallas as pl
from jax.experimental.pallas import tpu as pltpu

D_MODEL = 1024
HEAD_DIM = 64
N_HEADS = 8
N_KV = 2
Q_WIDTH = N_HEADS * HEAD_DIM
KV_WIDTH = N_KV * HEAD_DIM
WINDOW = 128
QBLK = 128
KEYS = 3 * QBLK
ROPE_THETA = 500000.0
ROTARY_DIM = HEAD_DIM // 4
POOL_WINDOWS = (2, 4, 8, 16)
POOL_HALO = 8
POOL_MARGIN = 64
POOL_WIDTH = 512
GROUP_W = 128
D_FF = 2816
EPS = 1e-6
LANES = 128
NEG = -1e30
LOG2E = 1.4426950408889634

TM = 512
N_SUB = 2
VMEM_LIMIT = 58 * 1024 * 1024

F32 = jnp.float32
BF16 = jnp.bfloat16


def _rms(x, g):
    ms = jnp.mean(x * x, axis=-1, keepdims=True)
    return x * lax.rsqrt(ms + EPS) * g


def _swiglu(xn, wg_ref, wu_ref, wd_ref):
    gate = jnp.dot(xn, wg_ref[...], preferred_element_type=F32)
    up = jnp.dot(xn, wu_ref[...], preferred_element_type=F32)
    act = (gate * jax.nn.sigmoid(gate) * up).astype(BF16)
    return jnp.dot(act, wd_ref[...], preferred_element_type=F32)


def _rope(t, c, sa, sb):
    t_plus = pltpu.roll(t, LANES - ROTARY_DIM // 2, 1)
    t_minus = pltpu.roll(t, ROTARY_DIM // 2, 1)
    return t * c + t_plus * sa + t_minus * sb


def _ffn_in_kernel(x_ref, g1_ref, wg_ref, wu_ref, wd_ref, gm_ref, win_ref,
                   c_ref, sa_ref, sb_ref,
                   h_ref, q_ref, kt_ref, v_ref, pc_ref):
    tm = x_ref.shape[1]
    sub = tm // N_SUB
    scale = HEAD_DIM ** -0.5 * LOG2E
    for n in range(N_SUB):
        rows = slice(n * sub, (n + 1) * sub)
        x = x_ref[0, rows]
        xn = _rms(x, g1_ref[...]).astype(BF16)
        h = x + 0.5 * _swiglu(xn, wg_ref, wu_ref, wd_ref)
        h_ref[0, rows] = h
        hn = _rms(h, gm_ref[...]).astype(BF16)
        u = jnp.dot(hn, win_ref[...], preferred_element_type=F32)
        c, sa, sb = c_ref[rows], sa_ref[rows], sb_ref[rows]
        for t in range(Q_WIDTH // LANES):
            qt = _rope(u[:, t * LANES:(t + 1) * LANES], c, sa, sb)
            q_ref[0, rows, t * LANES:(t + 1) * LANES] = (qt * scale).astype(BF16)
        k = _rope(u[:, Q_WIDTH:Q_WIDTH + KV_WIDTH], c, sa, sb)
        kt_ref[0, :, rows] = k.T.astype(BF16)
        v_ref[0, rows] = u[:, Q_WIDTH + KV_WIDTH:Q_WIDTH + 2 * KV_WIDTH].astype(BF16)
        pc_ref[0, rows] = u[:, Q_WIDTH + 2 * KV_WIDTH:]


def _mix_ffn_kernel(sink_ref, h_ref, q_ref, ktp_ref, ktc_ref, ktn_ref,
                    vp_ref, vc_ref, vn_ref, pcp_ref, pcc_ref, pcn_ref,
                    bias_ref, eye_ref, band_ref, poolw_ref, pscale_ref, wout_ref,
                    g2_ref, wg_ref, wu_ref, wd_ref, gf_ref,
                    o_ref,
                    kt_buf, vab_buf, hl_buf, edge_buf, d_buf, mix_buf, *, tm, seq):
    i = pl.program_id(1)
    n_tiles = seq // tm
    nq = tm // QBLK
    nb = seq // QBLK

    kt_buf[:, 0:QBLK] = ktp_ref[0]
    kt_buf[:, QBLK:QBLK + tm] = ktc_ref[0]
    kt_buf[:, QBLK + tm:] = ktn_ref[0]

    for lo, ref, n in ((0, vp_ref, QBLK), (QBLK, vc_ref, tm), (QBLK + tm, vn_ref, QBLK)):
        v = ref[0].astype(F32)
        vr = pltpu.roll(v, HEAD_DIM, 1)
        low = lax.broadcasted_iota(jnp.int32, v.shape, 1) < HEAD_DIM
        zero = jnp.zeros_like(v)
        ones_low = jnp.where(low, 1.0, 0.0).astype(BF16)
        ones_high = jnp.where(low, 0.0, 1.0).astype(BF16)
        vab_buf[0, 0, lo:lo + n, 0:LANES] = jnp.where(low, v, zero).astype(BF16)
        vab_buf[0, 1, lo:lo + n, 0:LANES] = jnp.where(low, zero, vr).astype(BF16)
        vab_buf[1, 0, lo:lo + n, 0:LANES] = jnp.where(low, vr, zero).astype(BF16)
        vab_buf[1, 1, lo:lo + n, 0:LANES] = jnp.where(low, zero, v).astype(BF16)
        for c in range(N_KV):
            vab_buf[c, 0, lo:lo + n, LANES:2 * LANES] = ones_low
            vab_buf[c, 1, lo:lo + n, LANES:2 * LANES] = ones_high

    prev_halo = jnp.where(i > 0, pcp_ref[0], 0.0)
    next_halo = jnp.where(i < n_tiles - 1, pcn_ref[0], 0.0)
    zpad = jnp.zeros((POOL_MARGIN - POOL_HALO, POOL_WIDTH), F32)
    for r0, nr, u in ((0, POOL_MARGIN, jnp.concatenate([zpad, prev_halo], axis=0)),
                      (POOL_MARGIN, tm, pcc_ref[0]),
                      (POOL_MARGIN + tm, POOL_MARGIN, jnp.concatenate([next_halo, zpad], axis=0))):
        hi = u.astype(BF16)
        lo = (u - hi.astype(F32)).astype(BF16)
        for g in range(len(POOL_WINDOWS)):
            lanes = slice(g * GROUP_W, (g + 1) * GROUP_W)
            hl_buf[r0:r0 + nr, 2 * g * GROUP_W:(2 * g + 1) * GROUP_W] = hi[:, lanes]
            hl_buf[r0:r0 + nr, (2 * g + 1) * GROUP_W:(2 * g + 2) * GROUP_W] = lo[:, lanes]
    for n in range(tm // QBLK):
        for g in range(len(POOL_WINDOWS)):
            win = hl_buf[n * QBLK:n * QBLK + 2 * QBLK, 2 * g * GROUP_W:(2 * g + 2) * GROUP_W]
            dd = jnp.dot(band_ref[g], win, preferred_element_type=F32)
            d_buf[n * QBLK:(n + 1) * QBLK, g * GROUP_W:(g + 1) * GROUP_W] = (
                dd[:, :GROUP_W] + dd[:, GROUP_W:])

    edge_buf[0, 0:POOL_HALO] = prev_halo
    edge_buf[0, POOL_HALO:] = pcc_ref[0, 0:2 * POOL_HALO]
    edge_buf[1, 0:2 * POOL_HALO] = pcc_ref[0, tm - 2 * POOL_HALO:tm]
    edge_buf[1, 2 * POOL_HALO:] = next_halo

    low_out = lax.broadcasted_iota(jnp.int32, (QBLK, LANES), 1) < HEAD_DIM
    eye = eye_ref[...]
    for j in range(nq):
        blk = i * nq + j
        variant = jnp.where(blk == 0, 0, jnp.where(blk == nb - 1, 2, 1))
        bias_t = bias_ref[variant]
        rows = slice(j * QBLK, (j + 1) * QBLK)
        keys = slice(j * QBLK, j * QBLK + KEYS)
        for c in range(N_KV):
            lhs = jnp.concatenate(
                [jnp.concatenate([q_ref[0, rows, (2 * c) * LANES:(2 * c + 1) * LANES], eye], axis=1),
                 jnp.concatenate([q_ref[0, rows, (2 * c + 1) * LANES:(2 * c + 2) * LANES], eye],
                                 axis=1)], axis=0)
            kc = kt_buf[c * HEAD_DIM:(c + 1) * HEAD_DIM, keys]
            z = jnp.zeros_like(kc)
            rhs = jnp.concatenate(
                [jnp.concatenate([kc, z], axis=1), jnp.concatenate([z, kc], axis=1), bias_t],
                axis=0)
            s = jnp.dot(lhs, rhs, preferred_element_type=F32)
            p_rows, sink_terms = [], []
            for r in range(2):
                p_cols = []
                for par in range(2):
                    sq = s[r * QBLK:(r + 1) * QBLK, par * KEYS:(par + 1) * KEYS]
                    sink = sink_ref[4 * c + 2 * r + par] * LOG2E
                    m = jnp.maximum(jnp.max(sq, axis=-1, keepdims=True), sink)
                    p_cols.append(jnp.exp2(sq - m).astype(BF16))
                    sink_terms.append(jnp.exp2(sink - m))
                p_rows.append(jnp.concatenate(p_cols, axis=1))
            pmat = jnp.concatenate(p_rows, axis=0)
            vrhs = jnp.concatenate([vab_buf[c, 0, keys], vab_buf[c, 1, keys]], axis=0)
            o = jnp.dot(pmat, vrhs, preferred_element_type=F32)
            for r in range(2):
                orow = o[r * QBLK:(r + 1) * QBLK]
                den = orow[:, LANES:] + jnp.where(low_out, sink_terms[2 * r], sink_terms[2 * r + 1])
                t = 2 * c + r
                mix_buf[rows, t * LANES:(t + 1) * LANES] = (orow[:, :LANES] / den).astype(BF16)

    for e, (row0, clipped) in enumerate(((0, i == 0), (tm - POOL_HALO, i == n_tiles - 1))):
        tpos = i * tm + row0 + lax.broadcasted_iota(jnp.int32, (POOL_HALO, GROUP_W), 0)
        base = POOL_HALO
        for g, w in enumerate(POOL_WINDOWS):
            half = w // 2
            lanes = slice(g * GROUP_W, (g + 1) * GROUP_W)

            def wsum(lo, hi):
                acc = edge_buf[e, base + lo:base + lo + POOL_HALO, lanes]
                for k in range(lo + 1, hi + 1):
                    acc = acc + edge_buf[e, base + k:base + k + POOL_HALO, lanes]
                return acc

            def count(lo, hi):
                a = jnp.clip(tpos + lo, 0, seq)
                b = jnp.clip(tpos + hi + 1, 0, seq)
                return (b - a).astype(F32)

            mean = 0.5 * (wsum(-half, half - 1) / count(-half, half - 1)
                          + wsum(-half + 1, half) / count(-half + 1, half))
            fixed = mean - edge_buf[e, base:base + POOL_HALO, lanes]
            d_buf[row0:row0 + POOL_HALO, lanes] = jnp.where(
                clipped, fixed, d_buf[row0:row0 + POOL_HALO, lanes])

    for g in range(len(POOL_WINDOWS)):
        lanes = slice(g * GROUP_W, (g + 1) * GROUP_W)
        y = jnp.dot(d_buf[:, lanes].astype(BF16), poolw_ref[g], preferred_element_type=F32)
        mix_buf[:, Q_WIDTH + g * GROUP_W:Q_WIDTH + (g + 1) * GROUP_W] = (
            y * pscale_ref[:, lanes]).astype(BF16)

    h = h_ref[0] + jnp.dot(mix_buf[...], wout_ref[...], preferred_element_type=F32)
    hn = _rms(h, g2_ref[...]).astype(BF16)
    h = h + 0.5 * _swiglu(hn, wg_ref, wu_ref, wd_ref)
    o_ref[0] = _rms(h, gf_ref[...])


def _rope_tables(seq):
    pos = jnp.arange(seq, dtype=jnp.int32)
    inv_freq = ROPE_THETA ** (-jnp.arange(0, ROTARY_DIM, 2, dtype=F32) / ROTARY_DIM)
    ang = pos.astype(F32)[:, None] * inv_freq[None, :]
    cos, sin = jnp.cos(ang), jnp.sin(ang)
    half = ROTARY_DIM // 2
    rest = HEAD_DIM - ROTARY_DIM
    c = jnp.concatenate([cos, cos, jnp.ones((seq, rest), F32)], axis=1)
    sa = jnp.concatenate([-sin, jnp.zeros((seq, HEAD_DIM - half), F32)], axis=1)
    sb = jnp.concatenate([jnp.zeros((seq, half), F32), sin, jnp.zeros((seq, rest), F32)], axis=1)
    rep = LANES // HEAD_DIM
    return jnp.tile(c, (1, rep)), jnp.tile(sa, (1, rep)), jnp.tile(sb, (1, rep))


def _band_bias(seq):
    r = np.arange(QBLK)[:, None]
    s = np.arange(KEYS)[None, :]
    band = (s - r >= 0) & (s - r <= 2 * WINDOW)
    first = band & (s >= QBLK)
    last = band & (s < 2 * QBLK)
    out = np.stack([np.tile(np.where(m, 0.0, NEG), (1, 2)) for m in (first, band, last)])
    return jnp.asarray(out, dtype=F32).astype(BF16)


def _pool_band():
    out = np.zeros((len(POOL_WINDOWS), QBLK, 2 * QBLK), np.float32)
    r = np.arange(QBLK)
    for g, w in enumerate(POOL_WINDOWS):
        half = w // 2
        for k in range(-half, half + 1):
            out[g, r, r + POOL_MARGIN + k] = (0.5 if abs(k) == half else 1.0) / w
        out[g, r, r + POOL_MARGIN] -= 1.0
    return jnp.asarray(out).astype(BF16)


def _const_spec(shape):
    nd = len(shape)
    return pl.BlockSpec(shape, lambda b, i: (0,) * nd, pipeline_mode=pl.Buffered(1))


def kernel(x, ffn1_norm, ffn1_w_gate, ffn1_w_up, ffn1_w_down, mix_norm, w_in, sink_logits,
           pool_w, pool_scale, w_out, ffn2_norm, ffn2_w_gate, ffn2_w_up, ffn2_w_down, final_norm):
    B, S, D = x.shape
    assert D == D_MODEL and S % TM == 0 and TM % QBLK == 0 and ffn1_norm.shape[0] == 1
    tm = TM
    grid = (B, S // tm)
    in_width = w_in.shape[-1]
    params = pltpu.CompilerParams(dimension_semantics=("arbitrary", "arbitrary"),
                                  vmem_limit_bytes=VMEM_LIMIT)

    c_tab, sa_tab, sb_tab = _rope_tables(S)
    row = lambda g: g.reshape(1, -1).astype(F32)
    tile_spec = lambda width: pl.BlockSpec((1, tm, width), lambda b, i: (b, i, 0))
    tab_spec = pl.BlockSpec((tm, LANES), lambda b, i: (i, 0))

    h1, q, kt, v, pc = pl.pallas_call(
        _ffn_in_kernel,
        grid=grid,
        in_specs=[
            tile_spec(D),
            _const_spec((1, D)),
            _const_spec((D, D_FF)), _const_spec((D, D_FF)), _const_spec((D_FF, D)),
            _const_spec((1, D)),
            _const_spec((D, in_width)),
            tab_spec, tab_spec, tab_spec,
        ],
        out_specs=[
            tile_spec(D),
            tile_spec(Q_WIDTH),
            pl.BlockSpec((1, KV_WIDTH, tm), lambda b, i: (b, 0, i)),
            tile_spec(KV_WIDTH),
            tile_spec(POOL_WIDTH),
        ],
        out_shape=[
            jax.ShapeDtypeStruct((B, S, D), F32),
            jax.ShapeDtypeStruct((B, S, Q_WIDTH), BF16),
            jax.ShapeDtypeStruct((B, KV_WIDTH, S), BF16),
            jax.ShapeDtypeStruct((B, S, KV_WIDTH), BF16),
            jax.ShapeDtypeStruct((B, S, POOL_WIDTH), F32),
        ],
        compiler_params=params,
        name="ffn1_inproj",
    )(x, row(ffn1_norm[0]), ffn1_w_gate[0].astype(BF16), ffn1_w_up[0].astype(BF16),
      ffn1_w_down[0].astype(BF16), row(mix_norm[0]), w_in[0].astype(BF16),
      c_tab, sa_tab, sb_tab)

    qb = tm // QBLK
    pb = tm // POOL_HALO
    n_qb = S // QBLK
    n_pb = S // POOL_HALO
    prev_q = lambda i: jnp.maximum(i * qb - 1, 0)
    next_q = lambda i: jnp.minimum((i + 1) * qb, n_qb - 1)
    prev_p = lambda i: jnp.maximum(i * pb - 1, 0)
    next_p = lambda i: jnp.minimum((i + 1) * pb, n_pb - 1)

    out = pl.pallas_call(
        functools.partial(_mix_ffn_kernel, tm=tm, seq=S),
        grid=grid,
        in_specs=[
            pl.BlockSpec(memory_space=pltpu.SMEM),
            tile_spec(D),
            tile_spec(Q_WIDTH),
            pl.BlockSpec((1, KV_WIDTH, QBLK), lambda b, i: (b, 0, prev_q(i))),
            pl.BlockSpec((1, KV_WIDTH, tm), lambda b, i: (b, 0, i)),
            pl.BlockSpec((1, KV_WIDTH, QBLK), lambda b, i: (b, 0, next_q(i))),
            pl.BlockSpec((1, QBLK, KV_WIDTH), lambda b, i: (b, prev_q(i), 0)),
            tile_spec(KV_WIDTH),
            pl.BlockSpec((1, QBLK, KV_WIDTH), lambda b, i: (b, next_q(i), 0)),
            pl.BlockSpec((1, POOL_HALO, POOL_WIDTH), lambda b, i: (b, prev_p(i), 0)),
            tile_spec(POOL_WIDTH),
            pl.BlockSpec((1, POOL_HALO, POOL_WIDTH), lambda b, i: (b, next_p(i), 0)),
            _const_spec((3, QBLK, 2 * KEYS)),
            _const_spec((QBLK, QBLK)),
            _const_spec((len(POOL_WINDOWS), QBLK, 2 * QBLK)),
            _const_spec((len(POOL_WINDOWS), GROUP_W, GROUP_W)),
            _const_spec((1, POOL_WIDTH)),
            _const_spec((D, D)),
            _const_spec((1, D)),
            _const_spec((D, D_FF)), _const_spec((D, D_FF)), _const_spec((D_FF, D)),
            _const_spec((1, D)),
        ],
        out_specs=tile_spec(D),
        out_shape=jax.ShapeDtypeStruct((B, S, D), x.dtype),
        scratch_shapes=[
            pltpu.VMEM((KV_WIDTH, tm + 2 * QBLK), BF16),
            pltpu.VMEM((N_KV, 2, tm + 2 * QBLK, 2 * LANES), BF16),
            pltpu.VMEM((tm + 2 * POOL_MARGIN, 2 * POOL_WIDTH), BF16),
            pltpu.VMEM((2, 3 * POOL_HALO, POOL_WIDTH), F32),
            pltpu.VMEM((tm, POOL_WIDTH), F32),
            pltpu.VMEM((tm, D), BF16),
        ],
        compiler_params=params,
        name="mix_ffn2",
    )(sink_logits[0].astype(F32), h1, q, kt, kt, kt, v, v, v, pc, pc, pc,
      _band_bias(S), jnp.eye(QBLK, dtype=BF16), _pool_band(), pool_w[0].astype(BF16), row(pool_scale[0]), w_out[0].astype(BF16),
      row(ffn2_norm[0]), ffn2_w_gate[0].astype(BF16), ffn2_w_up[0].astype(BF16),
      ffn2_w_down[0].astype(BF16), row(final_norm))
    return out
```

```python
import functools

import numpy as np
import jax
import jax.numpy as jnp
from jax import lax
from jax.experimental import pallas as pl
from jax.experimental.pallas import tpu as pltpu

D_MODEL = 1024
HEAD_DIM = 64
N_HEADS = 8
N_KV = 2
Q_WIDTH = N_HEADS * HEAD_DIM
KV_WIDTH = N_KV * HEAD_DIM
WINDOW = 128
QBLK = 128
KEYS = 3 * QBLK
ROPE_THETA = 500000.0
ROTARY_DIM = HEAD_DIM // 4
POOL_WINDOWS = (2, 4, 8, 16)
POOL_HALO = 8
POOL_MARGIN = 64
POOL_WIDTH = 512
GROUP_W = 128
D_FF = 2816
EPS = 1e-6
LANES = 128
NEG = -1e30
LOG2E = 1.4426950408889634

TM = 512
TM_IN = 1024
SUB_ROWS = 256
VMEM_LIMIT = 58 * 1024 * 1024

F32 = jnp.float32
BF16 = jnp.bfloat16


def _rms(x, g):
    ms = jnp.mean(x * x, axis=-1, keepdims=True)
    return x * lax.rsqrt(ms + EPS) * g


def _swiglu(xn, wg_ref, wu_ref, wd_ref):
    gate = jnp.dot(xn, wg_ref[...], preferred_element_type=F32)
    up = jnp.dot(xn, wu_ref[...], preferred_element_type=F32)
    act = (gate * jax.nn.sigmoid(gate) * up).astype(BF16)
    return jnp.dot(act, wd_ref[...], preferred_element_type=F32)


def _rope(t, c, sa, sb):
    t_plus = pltpu.roll(t, LANES - ROTARY_DIM // 2, 1)
    t_minus = pltpu.roll(t, ROTARY_DIM // 2, 1)
    return t * c + t_plus * sa + t_minus * sb


def _ffn_in_kernel(x_ref, g1_ref, wg_ref, wu_ref, wd_ref, gm_ref, win_ref,
                   c_ref, sa_ref, sb_ref,
                   h_ref, q_ref, kt_ref, v_ref, pc_ref):
    tm = x_ref.shape[1]
    sub = SUB_ROWS
    scale = HEAD_DIM ** -0.5 * LOG2E
    for n in range(tm // sub):
        rows = slice(n * sub, (n + 1) * sub)
        x = x_ref[0, rows]
        xn = _rms(x, g1_ref[...]).astype(BF16)
        h = x + 0.5 * _swiglu(xn, wg_ref, wu_ref, wd_ref)
        h_ref[0, rows] = h
        hn = _rms(h, gm_ref[...]).astype(BF16)
        u = jnp.dot(hn, win_ref[...], preferred_element_type=F32)
        c, sa, sb = c_ref[rows], sa_ref[rows], sb_ref[rows]
        for t in range(Q_WIDTH // LANES):
            qt = _rope(u[:, t * LANES:(t + 1) * LANES], c, sa, sb)
            q_ref[0, rows, t * LANES:(t + 1) * LANES] = (qt * scale).astype(BF16)
        k = _rope(u[:, Q_WIDTH:Q_WIDTH + KV_WIDTH], c, sa, sb)
        kt_ref[0, :, rows] = k.T.astype(BF16)
        v_ref[0, rows] = u[:, Q_WIDTH + KV_WIDTH:Q_WIDTH + 2 * KV_WIDTH].astype(BF16)
        pc_ref[0, rows] = u[:, Q_WIDTH + 2 * KV_WIDTH:]


def _mix_ffn_kernel(sink_ref, h_ref, q_ref, ktp_ref, ktc_ref, ktn_ref,
                    vp_ref, vc_ref, vn_ref, pcp_ref, pcc_ref, pcn_ref,
                    bias_ref, eye_ref, band_ref, poolw_ref, pscale_ref, wout_ref,
                    g2_ref, wg_ref, wu_ref, wd_ref, gf_ref,
                    o_ref,
                    kt_buf, vab_buf, hl_buf, edge_buf, d_buf, mix_buf, *, tm, seq):
    _mix_stage(sink_ref, q_ref, ktp_ref, ktc_ref, ktn_ref, vp_ref, vc_ref, vn_ref,
               pcp_ref, pcc_ref, pcn_ref, bias_ref, eye_ref, band_ref, poolw_ref, pscale_ref,
               kt_buf, vab_buf, hl_buf, edge_buf, d_buf, mix_buf, pl.program_id(1),
               tm=tm, seq=seq)
    h = h_ref[0] + jnp.dot(mix_buf[...], wout_ref[...], preferred_element_type=F32)
    hn = _rms(h, g2_ref[...]).astype(BF16)
    h = h + 0.5 * _swiglu(hn, wg_ref, wu_ref, wd_ref)
    o_ref[0] = _rms(h, gf_ref[...])


def _mix_stage(sink_ref, q_ref, ktp_ref, ktc_ref, ktn_ref, vp_ref, vc_ref, vn_ref,
               pcp_ref, pcc_ref, pcn_ref, bias_ref, eye_ref, band_ref, poolw_ref, pscale_ref,
               kt_buf, vab_buf, hl_buf, edge_buf, d_buf, mix_buf, i, *, tm, seq):
    n_tiles = seq // tm
    nq = tm // QBLK
    nb = seq // QBLK

    kt_buf[:, 0:QBLK] = ktp_ref[0]
    kt_buf[:, QBLK:QBLK + tm] = ktc_ref[0]
    kt_buf[:, QBLK + tm:] = ktn_ref[0]

    for lo, ref, n in ((0, vp_ref, QBLK), (QBLK, vc_ref, tm), (QBLK + tm, vn_ref, QBLK)):
        v = ref[0].astype(F32)
        vr = pltpu.roll(v, HEAD_DIM, 1)
        low = lax.broadcasted_iota(jnp.int32, v.shape, 1) < HEAD_DIM
        zero = jnp.zeros_like(v)
        ones_low = jnp.where(low, 1.0, 0.0).astype(BF16)
        ones_high = jnp.where(low, 0.0, 1.0).astype(BF16)
        vab_buf[0, 0, lo:lo + n, 0:LANES] = jnp.where(low, v, zero).astype(BF16)
        vab_buf[0, 1, lo:lo + n, 0:LANES] = jnp.where(low, zero, vr).astype(BF16)
        vab_buf[1, 0, lo:lo + n, 0:LANES] = jnp.where(low, vr, zero).astype(BF16)
        vab_buf[1, 1, lo:lo + n, 0:LANES] = jnp.where(low, zero, v).astype(BF16)
        for c in range(N_KV):
            vab_buf[c, 0, lo:lo + n, LANES:2 * LANES] = ones_low
            vab_buf[c, 1, lo:lo + n, LANES:2 * LANES] = ones_high

    prev_halo = jnp.where(i > 0, pcp_ref[0], 0.0)
    next_halo = jnp.where(i < n_tiles - 1, pcn_ref[0], 0.0)
    zpad = jnp.zeros((POOL_MARGIN - POOL_HALO, POOL_WIDTH), F32)
    for r0, nr, u in ((0, POOL_MARGIN, jnp.concatenate([zpad, prev_halo], axis=0)),
                      (POOL_MARGIN, tm, pcc_ref[0]),
                      (POOL_MARGIN + tm, POOL_MARGIN, jnp.concatenate([next_halo, zpad], axis=0))):
        hi = u.astype(BF16)
        lo = (u - hi.astype(F32)).astype(BF16)
        for g in range(len(POOL_WINDOWS)):
            lanes = slice(g * GROUP_W, (g + 1) * GROUP_W)
            hl_buf[r0:r0 + nr, 2 * g * GROUP_W:(2 * g + 1) * GROUP_W] = hi[:, lanes]
            hl_buf[r0:r0 + nr, (2 * g + 1) * GROUP_W:(2 * g + 2) * GROUP_W] = lo[:, lanes]
    for n in range(tm // QBLK):
        for g in range(len(POOL_WINDOWS)):
            win = hl_buf[n * QBLK:n * QBLK + 2 * QBLK, 2 * g * GROUP_W:(2 * g + 2) * GROUP_W]
            dd = jnp.dot(band_ref[g], win, preferred_element_type=F32)
            d_buf[n * QBLK:(n + 1) * QBLK, g * GROUP_W:(g + 1) * GROUP_W] = (
                dd[:, :GROUP_W] + dd[:, GROUP_W:])

    edge_buf[0, 0:POOL_HALO] = prev_halo
    edge_buf[0, POOL_HALO:] = pcc_ref[0, 0:2 * POOL_HALO]
    edge_buf[1, 0:2 * POOL_HALO] = pcc_ref[0, tm - 2 * POOL_HALO:tm]
    edge_buf[1, 2 * POOL_HALO:] = next_halo

    low_out = lax.broadcasted_iota(jnp.int32, (QBLK, LANES), 1) < HEAD_DIM
    eye = eye_ref[...]
    for j in range(nq):
        blk = i * nq + j
        variant = jnp.where(blk == 0, 0, jnp.where(blk == nb - 1, 2, 1))
        bias_t = bias_ref[variant]
        rows = slice(j * QBLK, (j + 1) * QBLK)
        keys = slice(j * QBLK, j * QBLK + KEYS)
        for c in range(N_KV):
            lhs = jnp.concatenate(
                [jnp.concatenate([q_ref[0, rows, (2 * c) * LANES:(2 * c + 1) * LANES], eye], axis=1),
                 jnp.concatenate([q_ref[0, rows, (2 * c + 1) * LANES:(2 * c + 2) * LANES], eye],
                                 axis=1)], axis=0)
            kc = kt_buf[c * HEAD_DIM:(c + 1) * HEAD_DIM, keys]
            z = jnp.zeros_like(kc)
            rhs = jnp.concatenate(
                [jnp.concatenate([kc, z], axis=1), jnp.concatenate([z, kc], axis=1), bias_t],
                axis=0)
            s = jnp.dot(lhs, rhs, preferred_element_type=F32)
            p_rows, sink_terms = [], []
            for r in range(2):
                p_cols = []
                for par in range(2):
                    sq = s[r * QBLK:(r + 1) * QBLK, par * KEYS:(par + 1) * KEYS]
                    sink = sink_ref[4 * c + 2 * r + par] * LOG2E
                    m = jnp.maximum(jnp.max(sq, axis=-1, keepdims=True), sink)
                    p_cols.append(jnp.exp2(sq - m).astype(BF16))
                    sink_terms.append(jnp.exp2(sink - m))
                p_rows.append(jnp.concatenate(p_cols, axis=1))
            pmat = jnp.concatenate(p_rows, axis=0)
            vrhs = jnp.concatenate([vab_buf[c, 0, keys], vab_buf[c, 1, keys]], axis=0)
            o = jnp.dot(pmat, vrhs, preferred_element_type=F32)
            for r in range(2):
                orow = o[r * QBLK:(r + 1) * QBLK]
                den = orow[:, LANES:] + jnp.where(low_out, sink_terms[2 * r], sink_terms[2 * r + 1])
                t = 2 * c + r
                mix_buf[rows, t * LANES:(t + 1) * LANES] = (orow[:, :LANES] / den).astype(BF16)

    for e, (row0, clipped) in enumerate(((0, i == 0), (tm - POOL_HALO, i == n_tiles - 1))):
        tpos = i * tm + row0 + lax.broadcasted_iota(jnp.int32, (POOL_HALO, GROUP_W), 0)
        base = POOL_HALO
        for g, w in enumerate(POOL_WINDOWS):
            half = w // 2
            lanes = slice(g * GROUP_W, (g + 1) * GROUP_W)

            def wsum(lo, hi):
                acc = edge_buf[e, base + lo:base + lo + POOL_HALO, lanes]
                for k in range(lo + 1, hi + 1):
                    acc = acc + edge_buf[e, base + k:base + k + POOL_HALO, lanes]
                return acc

            def count(lo, hi):
                a = jnp.clip(tpos + lo, 0, seq)
                b = jnp.clip(tpos + hi + 1, 0, seq)
                return (b - a).astype(F32)

            mean = 0.5 * (wsum(-half, half - 1) / count(-half, half - 1)
                          + wsum(-half + 1, half) / count(-half + 1, half))
            fixed = mean - edge_buf[e, base:base + POOL_HALO, lanes]
            d_buf[row0:row0 + POOL_HALO, lanes] = jnp.where(
                clipped, fixed, d_buf[row0:row0 + POOL_HALO, lanes])

    for g in range(len(POOL_WINDOWS)):
        lanes = slice(g * GROUP_W, (g + 1) * GROUP_W)
        y = jnp.dot(d_buf[:, lanes].astype(BF16), poolw_ref[g], preferred_element_type=F32)
        mix_buf[:, Q_WIDTH + g * GROUP_W:Q_WIDTH + (g + 1) * GROUP_W] = (
            y * pscale_ref[:, lanes]).astype(BF16)


def _rope_tables(seq):
    pos = jnp.arange(seq, dtype=jnp.int32)
    inv_freq = ROPE_THETA ** (-jnp.arange(0, ROTARY_DIM, 2, dtype=F32) / ROTARY_DIM)
    ang = pos.astype(F32)[:, None] * inv_freq[None, :]
    cos, sin = jnp.cos(ang), jnp.sin(ang)
    half = ROTARY_DIM // 2
    rest = HEAD_DIM - ROTARY_DIM
    c = jnp.concatenate([cos, cos, jnp.ones((seq, rest), F32)], axis=1)
    sa = jnp.concatenate([-sin, jnp.zeros((seq, HEAD_DIM - half), F32)], axis=1)
    sb = jnp.concatenate([jnp.zeros((seq, half), F32), sin, jnp.zeros((seq, rest), F32)], axis=1)
    rep = LANES // HEAD_DIM
    return jnp.tile(c, (1, rep)), jnp.tile(sa, (1, rep)), jnp.tile(sb, (1, rep))


def _band_bias(seq):
    r = np.arange(QBLK)[:, None]
    s = np.arange(KEYS)[None, :]
    band = (s - r >= 0) & (s - r <= 2 * WINDOW)
    first = band & (s >= QBLK)
    last = band & (s < 2 * QBLK)
    out = np.stack([np.tile(np.where(m, 0.0, NEG), (1, 2)) for m in (first, band, last)])
    return jnp.asarray(out, dtype=F32).astype(BF16)


def _pool_band():
    out = np.zeros((len(POOL_WINDOWS), QBLK, 2 * QBLK), np.float32)
    r = np.arange(QBLK)
    for g, w in enumerate(POOL_WINDOWS):
        half = w // 2
        for k in range(-half, half + 1):
            out[g, r, r + POOL_MARGIN + k] = (0.5 if abs(k) == half else 1.0) / w
        out[g, r, r + POOL_MARGIN] -= 1.0
    return jnp.asarray(out).astype(BF16)


def _const_spec(shape):
    nd = len(shape)
    return pl.BlockSpec(shape, lambda *_: (0,) * nd, pipeline_mode=pl.Buffered(1))


def kernel(x, ffn1_norm, ffn1_w_gate, ffn1_w_up, ffn1_w_down, mix_norm, w_in, sink_logits,
           pool_w, pool_scale, w_out, ffn2_norm, ffn2_w_gate, ffn2_w_up, ffn2_w_down, final_norm):
    B, S, D = x.shape
    assert D == D_MODEL and S % TM == 0 and TM % QBLK == 0 and ffn1_norm.shape[0] == 1
    assert S % TM_IN == 0 and TM_IN % SUB_ROWS == 0
    tm = TM
    grid = (B, S // tm)
    in_width = w_in.shape[-1]
    params = pltpu.CompilerParams(dimension_semantics=("arbitrary", "arbitrary"),
                                  vmem_limit_bytes=VMEM_LIMIT)

    c_tab, sa_tab, sb_tab = _rope_tables(S)
    row = lambda g: g.reshape(1, -1).astype(F32)
    tile_spec = lambda width, rows=tm: pl.BlockSpec((1, rows, width), lambda b, i: (b, i, 0))
    tm_in = TM_IN
    tab_spec = pl.BlockSpec((tm_in, LANES), lambda b, i: (i, 0))

    h1, q, kt, v, pc = pl.pallas_call(
        _ffn_in_kernel,
        grid=(B, S // tm_in),
        in_specs=[
            tile_spec(D, tm_in),
            _const_spec((1, D)),
            _const_spec((D, D_FF)), _const_spec((D, D_FF)), _const_spec((D_FF, D)),
            _const_spec((1, D)),
            _const_spec((D, in_width)),
            tab_spec, tab_spec, tab_spec,
        ],
        out_specs=[
            tile_spec(D, tm_in),
            tile_spec(Q_WIDTH, tm_in),
            pl.BlockSpec((1, KV_WIDTH, tm_in), lambda b, i: (b, 0, i)),
            tile_spec(KV_WIDTH, tm_in),
            tile_spec(POOL_WIDTH, tm_in),
        ],
        out_shape=[
            jax.ShapeDtypeStruct((B, S, D), F32),
            jax.ShapeDtypeStruct((B, S, Q_WIDTH), BF16),
            jax.ShapeDtypeStruct((B, KV_WIDTH, S), BF16),
            jax.ShapeDtypeStruct((B, S, KV_WIDTH), BF16),
            jax.ShapeDtypeStruct((B, S, POOL_WIDTH), F32),
        ],
        compiler_params=params,
        name="ffn1_inproj",
    )(x, row(ffn1_norm[0]), ffn1_w_gate[0].astype(BF16), ffn1_w_up[0].astype(BF16),
      ffn1_w_down[0].astype(BF16), row(mix_norm[0]), w_in[0].astype(BF16),
      c_tab, sa_tab, sb_tab)

    qb = tm // QBLK
    pb = tm // POOL_HALO
    n_qb = S // QBLK
    n_pb = S // POOL_HALO
    mix_spec = pl.BlockSpec
    prev_q = lambda i: jnp.maximum(i * qb - 1, 0)
    next_q = lambda i: jnp.minimum((i + 1) * qb, n_qb - 1)
    prev_p = lambda i: jnp.maximum(i * pb - 1, 0)
    next_p = lambda i: jnp.minimum((i + 1) * pb, n_pb - 1)

    out = pl.pallas_call(
        functools.partial(_mix_ffn_kernel, tm=tm, seq=S),
        grid=grid,
        in_specs=[
            pl.BlockSpec(memory_space=pltpu.SMEM),
            tile_spec(D),
            mix_spec((1, tm, Q_WIDTH), lambda b, i: (b, i, 0)),
            mix_spec((1, KV_WIDTH, QBLK), lambda b, i: (b, 0, prev_q(i))),
            mix_spec((1, KV_WIDTH, tm), lambda b, i: (b, 0, i)),
            mix_spec((1, KV_WIDTH, QBLK), lambda b, i: (b, 0, next_q(i))),
            mix_spec((1, QBLK, KV_WIDTH), lambda b, i: (b, prev_q(i), 0)),
            mix_spec((1, tm, KV_WIDTH), lambda b, i: (b, i, 0)),
            mix_spec((1, QBLK, KV_WIDTH), lambda b, i: (b, next_q(i), 0)),
            mix_spec((1, POOL_HALO, POOL_WIDTH), lambda b, i: (b, prev_p(i), 0)),
            mix_spec((1, tm, POOL_WIDTH), lambda b, i: (b, i, 0)),
            mix_spec((1, POOL_HALO, POOL_WIDTH), lambda b, i: (b, next_p(i), 0)),
            _const_spec((3, QBLK, 2 * KEYS)),
            _const_spec((QBLK, QBLK)),
            _const_spec((len(POOL_WINDOWS), QBLK, 2 * QBLK)),
            _const_spec((len(POOL_WINDOWS), GROUP_W, GROUP_W)),
            _const_spec((1, POOL_WIDTH)),
            _const_spec((D, D)),
            _const_spec((1, D)),
            _const_spec((D, D_FF)), _const_spec((D, D_FF)), _const_spec((D_FF, D)),
            _const_spec((1, D)),
        ],
        out_specs=tile_spec(D),
        out_shape=jax.ShapeDtypeStruct((B, S, D), x.dtype),
        scratch_shapes=[
            pltpu.VMEM((KV_WIDTH, tm + 2 * QBLK), BF16),
            pltpu.VMEM((N_KV, 2, tm + 2 * QBLK, 2 * LANES), BF16),
            pltpu.VMEM((tm + 2 * POOL_MARGIN, 2 * POOL_WIDTH), BF16),
            pltpu.VMEM((2, 3 * POOL_HALO, POOL_WIDTH), F32),
            pltpu.VMEM((tm, POOL_WIDTH), F32),
            pltpu.VMEM((tm, D), BF16),
        ],
        compiler_params=params,
        name="mix_ffn2",
    )(sink_logits[0].astype(F32), h1, q, kt, kt, kt, v, v, v, pc, pc, pc,
      _band_bias(S), jnp.eye(QBLK, dtype=BF16), _pool_band(), pool_w[0].astype(BF16),
      row(pool_scale[0]), w_out[0].astype(BF16),
      row(ffn2_norm[0]), ffn2_w_gate[0].astype(BF16), ffn2_w_up[0].astype(BF16),
      ffn2_w_down[0].astype(BF16), row(final_norm))
    return out
```

```python
import functools

import numpy as np
import jax
import jax.numpy as jnp
from jax import lax
from jax.experimental import pallas as pl
from jax.experimental.pallas import tpu as pltpu

D_MODEL = 1024
HEAD_DIM = 64
N_HEADS = 8
N_KV = 2
Q_WIDTH = N_HEADS * HEAD_DIM
KV_WIDTH = N_KV * HEAD_DIM
WINDOW = 128
QBLK = 128
KEYS = 3 * QBLK
ROPE_THETA = 500000.0
ROTARY_DIM = HEAD_DIM // 4
POOL_WINDOWS = (2, 4, 8, 16)
POOL_HALO = 8
POOL_MARGIN = 64
POOL_WIDTH = 512
GROUP_W = 128
D_FF = 2816
EPS = 1e-6
LANES = 128
NEG = -1e30
LOG2E = 1.4426950408889634

TM = 512
TM_IN = 1024
SUB_ROWS = 256
VMEM_LIMIT = 58 * 1024 * 1024

F32 = jnp.float32
BF16 = jnp.bfloat16


def _rms(x, g):
    ms = jnp.mean(x * x, axis=-1, keepdims=True)
    return x * lax.rsqrt(ms + EPS) * g


def _swiglu(xn, wg_ref, wu_ref, wd_ref):
    gate = jnp.dot(xn, wg_ref[...], preferred_element_type=F32)
    up = jnp.dot(xn, wu_ref[...], preferred_element_type=F32)
    act = (gate * jax.nn.sigmoid(gate) * up).astype(BF16)
    return jnp.dot(act, wd_ref[...], preferred_element_type=F32)


def _rope(t, c, sa, sb):
    t_plus = pltpu.roll(t, LANES - ROTARY_DIM // 2, 1)
    t_minus = pltpu.roll(t, ROTARY_DIM // 2, 1)
    return t * c + t_plus * sa + t_minus * sb


def _ffn_in_kernel(x_ref, g1_ref, wg_ref, wu_ref, wd_ref, gm_ref, win_ref,
                   c_ref, sa_ref, sb_ref,
                   h_ref, q_ref, kt_ref, v_ref, pc_ref):
    tm = x_ref.shape[1]
    sub = SUB_ROWS
    scale = HEAD_DIM ** -0.5 * LOG2E
    def stage_norm(rows):
        return _rms(x_ref[0, rows], g1_ref[...]).astype(BF16)

    def stage_gate_up(xn):
        gate = jnp.dot(xn, wg_ref[...], preferred_element_type=F32)
        up = jnp.dot(xn, wu_ref[...], preferred_element_type=F32)
        return (gate * jax.nn.sigmoid(gate) * up).astype(BF16)

    def stage_down(rows, act):
        h = x_ref[0, rows] + 0.5 * jnp.dot(act, wd_ref[...], preferred_element_type=F32)
        h_ref[0, rows] = h
        return _rms(h, gm_ref[...]).astype(BF16)

    def stage_proj(hn):
        return jnp.dot(hn, win_ref[...], preferred_element_type=F32)

    def stage_out(rows, u):
        c, sa, sb = c_ref[rows], sa_ref[rows], sb_ref[rows]
        for t in range(Q_WIDTH // LANES):
            qt = _rope(u[:, t * LANES:(t + 1) * LANES], c, sa, sb)
            q_ref[0, rows, t * LANES:(t + 1) * LANES] = (qt * scale).astype(BF16)
        k = _rope(u[:, Q_WIDTH:Q_WIDTH + KV_WIDTH], c, sa, sb)
        kt_ref[0, :, rows] = k.T.astype(BF16)
        v_ref[0, rows] = u[:, Q_WIDTH + KV_WIDTH:Q_WIDTH + 2 * KV_WIDTH].astype(BF16)
        pc_ref[0, rows] = u[:, Q_WIDTH + 2 * KV_WIDTH:]

    for pair in range(tm // (2 * sub)):
        ra = slice((2 * pair) * sub, (2 * pair + 1) * sub)
        rb = slice((2 * pair + 1) * sub, (2 * pair + 2) * sub)
        xn_a, xn_b = stage_norm(ra), stage_norm(rb)
        act_a = stage_gate_up(xn_a)
        act_b = stage_gate_up(xn_b)
        hn_a = stage_down(ra, act_a)
        hn_b = stage_down(rb, act_b)
        u_a = stage_proj(hn_a)
        u_b = stage_proj(hn_b)
        stage_out(ra, u_a)
        stage_out(rb, u_b)


def _mix_ffn_kernel(sink_ref, h_ref, q_ref, ktp_ref, ktc_ref, ktn_ref,
                    vp_ref, vc_ref, vn_ref, pcp_ref, pcc_ref, pcn_ref,
                    bias_ref, eye_ref, band_ref, poolw_ref, pscale_ref, wout_ref,
                    g2_ref, wg_ref, wu_ref, wd_ref, gf_ref,
                    o_ref,
                    kt_buf, vab_buf, hl_buf, edge_buf, d_buf, mix_buf, *, tm, seq):
    _mix_stage(sink_ref, q_ref, ktp_ref, ktc_ref, ktn_ref, vp_ref, vc_ref, vn_ref,
               pcp_ref, pcc_ref, pcn_ref, bias_ref, eye_ref, band_ref, poolw_ref, pscale_ref,
               kt_buf, vab_buf, hl_buf, edge_buf, d_buf, mix_buf, pl.program_id(1),
               tm=tm, seq=seq)
    h = h_ref[0] + jnp.dot(mix_buf[...], wout_ref[...], preferred_element_type=F32)
    hn = _rms(h, g2_ref[...]).astype(BF16)
    h = h + 0.5 * _swiglu(hn, wg_ref, wu_ref, wd_ref)
    o_ref[0] = _rms(h, gf_ref[...])


def _mix_stage(sink_ref, q_ref, ktp_ref, ktc_ref, ktn_ref, vp_ref, vc_ref, vn_ref,
               pcp_ref, pcc_ref, pcn_ref, bias_ref, eye_ref, band_ref, poolw_ref, pscale_ref,
               kt_buf, vab_buf, hl_buf, edge_buf, d_buf, mix_buf, i, *, tm, seq):
    n_tiles = seq // tm
    nq = tm // QBLK
    nb = seq // QBLK

    kt_buf[:, 0:QBLK] = ktp_ref[0]
    kt_buf[:, QBLK:QBLK + tm] = ktc_ref[0]
    kt_buf[:, QBLK + tm:] = ktn_ref[0]

    for lo, ref, n in ((0, vp_ref, QBLK), (QBLK, vc_ref, tm), (QBLK + tm, vn_ref, QBLK)):
        v = ref[0].astype(F32)
        vr = pltpu.roll(v, HEAD_DIM, 1)
        low = lax.broadcasted_iota(jnp.int32, v.shape, 1) < HEAD_DIM
        zero = jnp.zeros_like(v)
        ones_low = jnp.where(low, 1.0, 0.0).astype(BF16)
        ones_high = jnp.where(low, 0.0, 1.0).astype(BF16)
        vab_buf[0, 0, lo:lo + n, 0:LANES] = jnp.where(low, v, zero).astype(BF16)
        vab_buf[0, 1, lo:lo + n, 0:LANES] = jnp.where(low, zero, vr).astype(BF16)
        vab_buf[1, 0, lo:lo + n, 0:LANES] = jnp.where(low, vr, zero).astype(BF16)
        vab_buf[1, 1, lo:lo + n, 0:LANES] = jnp.where(low, zero, v).astype(BF16)
        for c in range(N_KV):
            vab_buf[c, 0, lo:lo + n, LANES:2 * LANES] = ones_low
            vab_buf[c, 1, lo:lo + n, LANES:2 * LANES] = ones_high

    prev_halo = jnp.where(i > 0, pcp_ref[0], 0.0)
    next_halo = jnp.where(i < n_tiles - 1, pcn_ref[0], 0.0)
    zpad = jnp.zeros((POOL_MARGIN - POOL_HALO, POOL_WIDTH), F32)
    for r0, nr, u in ((0, POOL_MARGIN, jnp.concatenate([zpad, prev_halo], axis=0)),
                      (POOL_MARGIN, tm, pcc_ref[0]),
                      (POOL_MARGIN + tm, POOL_MARGIN, jnp.concatenate([next_halo, zpad], axis=0))):
        hi = u.astype(BF16)
        lo = (u - hi.astype(F32)).astype(BF16)
        for g in range(len(POOL_WINDOWS)):
            lanes = slice(g * GROUP_W, (g + 1) * GROUP_W)
            hl_buf[r0:r0 + nr, 2 * g * GROUP_W:(2 * g + 1) * GROUP_W] = hi[:, lanes]
            hl_buf[r0:r0 + nr, (2 * g + 1) * GROUP_W:(2 * g + 2) * GROUP_W] = lo[:, lanes]
    for n in range(tm // QBLK):
        for g in range(len(POOL_WINDOWS)):
            win = hl_buf[n * QBLK:n * QBLK + 2 * QBLK, 2 * g * GROUP_W:(2 * g + 2) * GROUP_W]
            dd = jnp.dot(band_ref[g], win, preferred_element_type=F32)
            d_buf[n * QBLK:(n + 1) * QBLK, g * GROUP_W:(g + 1) * GROUP_W] = (
                dd[:, :GROUP_W] + dd[:, GROUP_W:])

    edge_buf[0, 0:POOL_HALO] = prev_halo
    edge_buf[0, POOL_HALO:] = pcc_ref[0, 0:2 * POOL_HALO]
    edge_buf[1, 0:2 * POOL_HALO] = pcc_ref[0, tm - 2 * POOL_HALO:tm]
    edge_buf[1, 2 * POOL_HALO:] = next_halo

    low_out = lax.broadcasted_iota(jnp.int32, (QBLK, LANES), 1) < HEAD_DIM
    eye = eye_ref[...]
    for j in range(nq):
        blk = i * nq + j
        variant = jnp.where(blk == 0, 0, jnp.where(blk == nb - 1, 2, 1))
        bias_t = bias_ref[variant]
        rows = slice(j * QBLK, (j + 1) * QBLK)
        keys = slice(j * QBLK, j * QBLK + KEYS)
        for c in range(N_KV):
            lhs = jnp.concatenate(
                [jnp.concatenate([q_ref[0, rows, (2 * c) * LANES:(2 * c + 1) * LANES], eye], axis=1),
                 jnp.concatenate([q_ref[0, rows, (2 * c + 1) * LANES:(2 * c + 2) * LANES], eye],
                                 axis=1)], axis=0)
            kc = kt_buf[c * HEAD_DIM:(c + 1) * HEAD_DIM, keys]
            z = jnp.zeros_like(kc)
            rhs = jnp.concatenate(
                [jnp.concatenate([kc, z], axis=1), jnp.concatenate([z, kc], axis=1), bias_t],
                axis=0)
            s = jnp.dot(lhs, rhs, preferred_element_type=F32)
            p_rows, sink_terms = [], []
            for r in range(2):
                p_cols = []
                for par in range(2):
                    sq = s[r * QBLK:(r + 1) * QBLK, par * KEYS:(par + 1) * KEYS]
                    sink = sink_ref[4 * c + 2 * r + par] * LOG2E
                    m = jnp.maximum(jnp.max(sq, axis=-1, keepdims=True), sink)
                    p_cols.append(jnp.exp2(sq - m).astype(BF16))
                    sink_terms.append(jnp.exp2(sink - m))
                p_rows.append(jnp.concatenate(p_cols, axis=1))
            pmat = jnp.concatenate(p_rows, axis=0)
            vrhs = jnp.concatenate([vab_buf[c, 0, keys], vab_buf[c, 1, keys]], axis=0)
            o = jnp.dot(pmat, vrhs, preferred_element_type=F32)
            for r in range(2):
                orow = o[r * QBLK:(r + 1) * QBLK]
                den = orow[:, LANES:] + jnp.where(low_out, sink_terms[2 * r], sink_terms[2 * r + 1])
                t = 2 * c + r
                mix_buf[rows, t * LANES:(t + 1) * LANES] = (orow[:, :LANES] / den).astype(BF16)

    for e, (row0, clipped) in enumerate(((0, i == 0), (tm - POOL_HALO, i == n_tiles - 1))):
        tpos = i * tm + row0 + lax.broadcasted_iota(jnp.int32, (POOL_HALO, GROUP_W), 0)
        base = POOL_HALO
        for g, w in enumerate(POOL_WINDOWS):
            half = w // 2
            lanes = slice(g * GROUP_W, (g + 1) * GROUP_W)

            def wsum(lo, hi):
                acc = edge_buf[e, base + lo:base + lo + POOL_HALO, lanes]
                for k in range(lo + 1, hi + 1):
                    acc = acc + edge_buf[e, base + k:base + k + POOL_HALO, lanes]
                return acc

            def count(lo, hi):
                a = jnp.clip(tpos + lo, 0, seq)
                b = jnp.clip(tpos + hi + 1, 0, seq)
                return (b - a).astype(F32)

            mean = 0.5 * (wsum(-half, half - 1) / count(-half, half - 1)
                          + wsum(-half + 1, half) / count(-half + 1, half))
            fixed = mean - edge_buf[e, base:base + POOL_HALO, lanes]
            d_buf[row0:row0 + POOL_HALO, lanes] = jnp.where(
                clipped, fixed, d_buf[row0:row0 + POOL_HALO, lanes])

    for g in range(len(POOL_WINDOWS)):
        lanes = slice(g * GROUP_W, (g + 1) * GROUP_W)
        y = jnp.dot(d_buf[:, lanes].astype(BF16), poolw_ref[g], preferred_element_type=F32)
        mix_buf[:, Q_WIDTH + g * GROUP_W:Q_WIDTH + (g + 1) * GROUP_W] = (
            y * pscale_ref[:, lanes]).astype(BF16)


def _rope_tables(seq):
    pos = jnp.arange(seq, dtype=jnp.int32)
    inv_freq = ROPE_THETA ** (-jnp.arange(0, ROTARY_DIM, 2, dtype=F32) / ROTARY_DIM)
    ang = pos.astype(F32)[:, None] * inv_freq[None, :]
    cos, sin = jnp.cos(ang), jnp.sin(ang)
    half = ROTARY_DIM // 2
    rest = HEAD_DIM - ROTARY_DIM
    c = jnp.concatenate([cos, cos, jnp.ones((seq, rest), F32)], axis=1)
    sa = jnp.concatenate([-sin, jnp.zeros((seq, HEAD_DIM - half), F32)], axis=1)
    sb = jnp.concatenate([jnp.zeros((seq, half), F32), sin, jnp.zeros((seq, rest), F32)], axis=1)
    rep = LANES // HEAD_DIM
    return jnp.tile(c, (1, rep)), jnp.tile(sa, (1, rep)), jnp.tile(sb, (1, rep))


def _band_bias(seq):
    r = np.arange(QBLK)[:, None]
    s = np.arange(KEYS)[None, :]
    band = (s - r >= 0) & (s - r <= 2 * WINDOW)
    first = band & (s >= QBLK)
    last = band & (s < 2 * QBLK)
    out = np.stack([np.tile(np.where(m, 0.0, NEG), (1, 2)) for m in (first, band, last)])
    return jnp.asarray(out, dtype=F32).astype(BF16)


def _pool_band():
    out = np.zeros((len(POOL_WINDOWS), QBLK, 2 * QBLK), np.float32)
    r = np.arange(QBLK)
    for g, w in enumerate(POOL_WINDOWS):
        half = w // 2
        for k in range(-half, half + 1):
            out[g, r, r + POOL_MARGIN + k] = (0.5 if abs(k) == half else 1.0) / w
        out[g, r, r + POOL_MARGIN] -= 1.0
    return jnp.asarray(out).astype(BF16)


def _const_spec(shape):
    nd = len(shape)
    return pl.BlockSpec(shape, lambda *_: (0,) * nd, pipeline_mode=pl.Buffered(1))


def kernel(x, ffn1_norm, ffn1_w_gate, ffn1_w_up, ffn1_w_down, mix_norm, w_in, sink_logits,
           pool_w, pool_scale, w_out, ffn2_norm, ffn2_w_gate, ffn2_w_up, ffn2_w_down, final_norm):
    B, S, D = x.shape
    assert D == D_MODEL and S % TM == 0 and TM % QBLK == 0 and ffn1_norm.shape[0] == 1
    assert S % TM_IN == 0 and TM_IN % SUB_ROWS == 0
    tm = TM
    grid = (B, S // tm)
    in_width = w_in.shape[-1]
    params = pltpu.CompilerParams(dimension_semantics=("arbitrary", "arbitrary"),
                                  vmem_limit_bytes=VMEM_LIMIT)

    c_tab, sa_tab, sb_tab = _rope_tables(S)
    row = lambda g: g.reshape(1, -1).astype(F32)
    tile_spec = lambda width, rows=tm: pl.BlockSpec((1, rows, width), lambda b, i: (b, i, 0))
    tm_in = TM_IN
    tab_spec = pl.BlockSpec((tm_in, LANES), lambda b, i: (i, 0))

    h1, q, kt, v, pc = pl.pallas_call(
        _ffn_in_kernel,
        grid=(B, S // tm_in),
        in_specs=[
            tile_spec(D, tm_in),
            _const_spec((1, D)),
            _const_spec((D, D_FF)), _const_spec((D, D_FF)), _const_spec((D_FF, D)),
            _const_spec((1, D)),
            _const_spec((D, in_width)),
            tab_spec, tab_spec, tab_spec,
        ],
        out_specs=[
            tile_spec(D, tm_in),
            tile_spec(Q_WIDTH, tm_in),
            pl.BlockSpec((1, KV_WIDTH, tm_in), lambda b, i: (b, 0, i)),
            tile_spec(KV_WIDTH, tm_in),
            tile_spec(POOL_WIDTH, tm_in),
        ],
        out_shape=[
            jax.ShapeDtypeStruct((B, S, D), F32),
            jax.ShapeDtypeStruct((B, S, Q_WIDTH), BF16),
            jax.ShapeDtypeStruct((B, KV_WIDTH, S), BF16),
            jax.ShapeDtypeStruct((B, S, KV_WIDTH), BF16),
            jax.ShapeDtypeStruct((B, S, POOL_WIDTH), F32),
        ],
        compiler_params=params,
        name="ffn1_inproj",
    )(x, row(ffn1_norm[0]), ffn1_w_gate[0].astype(BF16), ffn1_w_up[0].astype(BF16),
      ffn1_w_down[0].astype(BF16), row(mix_norm[0]), w_in[0].astype(BF16),
      c_tab, sa_tab, sb_tab)

    qb = tm // QBLK
    pb = tm // POOL_HALO
    n_qb = S // QBLK
    n_pb = S // POOL_HALO
    mix_spec = pl.BlockSpec
    prev_q = lambda i: jnp.maximum(i * qb - 1, 0)
    next_q = lambda i: jnp.minimum((i + 1) * qb, n_qb - 1)
    prev_p = lambda i: jnp.maximum(i * pb - 1, 0)
    next_p = lambda i: jnp.minimum((i + 1) * pb, n_pb - 1)

    out = pl.pallas_call(
        functools.partial(_mix_ffn_kernel, tm=tm, seq=S),
        grid=grid,
        in_specs=[
            pl.BlockSpec(memory_space=pltpu.SMEM),
            tile_spec(D),
            mix_spec((1, tm, Q_WIDTH), lambda b, i: (b, i, 0)),
            mix_spec((1, KV_WIDTH, QBLK), lambda b, i: (b, 0, prev_q(i))),
            mix_spec((1, KV_WIDTH, tm), lambda b, i: (b, 0, i)),
            mix_spec((1, KV_WIDTH, QBLK), lambda b, i: (b, 0, next_q(i))),
            mix_spec((1, QBLK, KV_WIDTH), lambda b, i: (b, prev_q(i), 0)),
            mix_spec((1, tm, KV_WIDTH), lambda b, i: (b, i, 0)),
            mix_spec((1, QBLK, KV_WIDTH), lambda b, i: (b, next_q(i), 0)),
            mix_spec((1, POOL_HALO, POOL_WIDTH), lambda b, i: (b, prev_p(i), 0)),
            mix_spec((1, tm, POOL_WIDTH), lambda b, i: (b, i, 0)),
            mix_spec((1, POOL_HALO, POOL_WIDTH), lambda b, i: (b, next_p(i), 0)),
            _const_spec((3, QBLK, 2 * KEYS)),
            _const_spec((QBLK, QBLK)),
            _const_spec((len(POOL_WINDOWS), QBLK, 2 * QBLK)),
            _const_spec((len(POOL_WINDOWS), GROUP_W, GROUP_W)),
            _const_spec((1, POOL_WIDTH)),
            _const_spec((D, D)),
            _const_spec((1, D)),
            _const_spec((D, D_FF)), _const_spec((D, D_FF)), _const_spec((D_FF, D)),
            _const_spec((1, D)),
        ],
        out_specs=tile_spec(D),
        out_shape=jax.ShapeDtypeStruct((B, S, D), x.dtype),
        scratch_shapes=[
            pltpu.VMEM((KV_WIDTH, tm + 2 * QBLK), BF16),
            pltpu.VMEM((N_KV, 2, tm + 2 * QBLK, 2 * LANES), BF16),
            pltpu.VMEM((tm + 2 * POOL_MARGIN, 2 * POOL_WIDTH), BF16),
            pltpu.VMEM((2, 3 * POOL_HALO, POOL_WIDTH), F32),
            pltpu.VMEM((tm, POOL_WIDTH), F32),
            pltpu.VMEM((tm, D), BF16),
        ],
        compiler_params=params,
        name="mix_ffn2",
    )(sink_logits[0].astype(F32), h1, q, kt, kt, kt, v, v, v, pc, pc, pc,
      _band_bias(S), jnp.eye(QBLK, dtype=BF16), _pool_band(), pool_w[0].astype(BF16),
      row(pool_scale[0]), w_out[0].astype(BF16),
      row(ffn2_norm[0]), ffn2_w_gate[0].astype(BF16), ffn2_w_up[0].astype(BF16),
      ffn2_w_down[0].astype(BF16), row(final_norm))
    return out
```

```python
import functools

import numpy as np
import jax
import jax.numpy as jnp
from jax import lax
from jax.experimental import pallas as pl
from jax.experimental.pallas import tpu as pltpu

D_MODEL = 1024
HEAD_DIM = 64
N_HEADS = 8
N_KV = 2
Q_WIDTH = N_HEADS * HEAD_DIM
KV_WIDTH = N_KV * HEAD_DIM
WINDOW = 128
QBLK = 128
KEYS = 3 * QBLK
ROPE_THETA = 500000.0
ROTARY_DIM = HEAD_DIM // 4
POOL_WINDOWS = (2, 4, 8, 16)
POOL_HALO = 8
POOL_MARGIN = 64
POOL_WIDTH = 512
GROUP_W = 128
D_FF = 2816
EPS = 1e-6
LANES = 128
NEG = -1e30
LOG2E = 1.4426950408889634

TM = 512
TM_IN = 1024
SUB_ROWS = 256
VMEM_LIMIT = 58 * 1024 * 1024

F32 = jnp.float32
BF16 = jnp.bfloat16


def _rms(x, g):
    ms = jnp.mean(x * x, axis=-1, keepdims=True)
    return x * lax.rsqrt(ms + EPS) * g


def _swiglu(xn, wg_ref, wu_ref, wd_ref):
    gate = jnp.dot(xn, wg_ref[...], preferred_element_type=F32)
    up = jnp.dot(xn, wu_ref[...], preferred_element_type=F32)
    act = (gate * jax.nn.sigmoid(gate) * up).astype(BF16)
    return jnp.dot(act, wd_ref[...], preferred_element_type=F32)


def _rope(t, c, sa, sb):
    t_plus = pltpu.roll(t, LANES - ROTARY_DIM // 2, 1)
    t_minus = pltpu.roll(t, ROTARY_DIM // 2, 1)
    return t * c + t_plus * sa + t_minus * sb


def _ffn_in_kernel(x_ref, g1_ref, wg_ref, wu_ref, wd_ref, gm_ref, win_ref,
                   c_ref, sa_ref, sb_ref,
                   h_ref, q_ref, kt_ref, v_ref, pc_ref):
    tm = x_ref.shape[1]
    sub = SUB_ROWS
    scale = HEAD_DIM ** -0.5 * LOG2E
    def stage_norm(rows):
        return _rms(x_ref[0, rows], g1_ref[...]).astype(BF16)

    def stage_gate_up(xn):
        gate = jnp.dot(xn, wg_ref[...], preferred_element_type=F32)
        up = jnp.dot(xn, wu_ref[...], preferred_element_type=F32)
        return (gate * jax.nn.sigmoid(gate) * up).astype(BF16)

    def stage_down(rows, act):
        h = x_ref[0, rows] + 0.5 * jnp.dot(act, wd_ref[...], preferred_element_type=F32)
        h_ref[0, rows] = h
        return _rms(h, gm_ref[...]).astype(BF16)

    def stage_proj(hn):
        return jnp.dot(hn, win_ref[...], preferred_element_type=F32)

    def stage_out(rows, u):
        c, sa, sb = c_ref[rows], sa_ref[rows], sb_ref[rows]
        for t in range(Q_WIDTH // LANES):
            qt = _rope(u[:, t * LANES:(t + 1) * LANES], c, sa, sb)
            q_ref[0, rows, t * LANES:(t + 1) * LANES] = (qt * scale).astype(BF16)
        k = _rope(u[:, Q_WIDTH:Q_WIDTH + KV_WIDTH], c, sa, sb)
        kt_ref[0, :, rows] = k.T.astype(BF16)
        v_ref[0, rows] = u[:, Q_WIDTH + KV_WIDTH:Q_WIDTH + 2 * KV_WIDTH].astype(BF16)
        pc_ref[0, rows] = u[:, Q_WIDTH + 2 * KV_WIDTH:]

    for pair in range(tm // (2 * sub)):
        ra = slice((2 * pair) * sub, (2 * pair + 1) * sub)
        rb = slice((2 * pair + 1) * sub, (2 * pair + 2) * sub)
        xn_a, xn_b = stage_norm(ra), stage_norm(rb)
        act_a = stage_gate_up(xn_a)
        act_b = stage_gate_up(xn_b)
        hn_a = stage_down(ra, act_a)
        hn_b = stage_down(rb, act_b)
        u_a = stage_proj(hn_a)
        u_b = stage_proj(hn_b)
        stage_out(ra, u_a)
        stage_out(rb, u_b)


def _mix_ffn_kernel(sink_ref, h_ref, q_ref, ktp_ref, ktc_ref, ktn_ref,
                    vp_ref, vc_ref, vn_ref, pcp_ref, pcc_ref, pcn_ref,
                    bias_ref, eye_ref, band_ref, poolw_ref, pscale_ref, wout_ref,
                    g2_ref, wg_ref, wu_ref, wd_ref, gf_ref,
                    o_ref,
                    kt_buf, vab_buf, hl_buf, edge_buf, d_buf, mix_buf, *, tm, seq):
    i = pl.program_id(1)
    _mix_prep(ktp_ref, ktc_ref, ktn_ref, vp_ref, vc_ref, vn_ref, pcp_ref, pcc_ref, pcn_ref,
              kt_buf, vab_buf, hl_buf, edge_buf, i, tm=tm, seq=seq)
    mix_rows = functools.partial(
        _mix_rows, sink_ref, q_ref, bias_ref, eye_ref, band_ref, poolw_ref, pscale_ref,
        kt_buf, vab_buf, hl_buf, edge_buf, d_buf, mix_buf, i, tm=tm, seq=seq)

    sub = tm // 2
    sub_blocks = sub // QBLK
    halves = (slice(0, sub), slice(sub, tm))

    def stage_out_proj(rows):
        h = h_ref[0, rows] + jnp.dot(mix_buf[rows], wout_ref[...], preferred_element_type=F32)
        return h, _rms(h, g2_ref[...]).astype(BF16)

    def stage_gate_up(hn):
        gate = jnp.dot(hn, wg_ref[...], preferred_element_type=F32)
        up = jnp.dot(hn, wu_ref[...], preferred_element_type=F32)
        return (gate * jax.nn.sigmoid(gate) * up).astype(BF16)

    def stage_down(rows, h, act):
        h = h + 0.5 * jnp.dot(act, wd_ref[...], preferred_element_type=F32)
        o_ref[0, rows] = _rms(h, gf_ref[...])

    mix_rows(0, 2 * sub_blocks)
    hs =[stage_out_proj(rows) for rows in halves]
    acts = [stage_gate_up(hn) for _, hn in hs]
    for rows, (h, _), act in zip(halves, hs, acts):
        stage_down(rows, h, act)


def _mix_prep(ktp_ref, ktc_ref, ktn_ref, vp_ref, vc_ref, vn_ref, pcp_ref, pcc_ref, pcn_ref,
              kt_buf, vab_buf, hl_buf, edge_buf, i, *, tm, seq):
    n_tiles = seq // tm

    kt_buf[:, 0:QBLK] = ktp_ref[0]
    kt_buf[:, QBLK:QBLK + tm] = ktc_ref[0]
    kt_buf[:, QBLK + tm:] = ktn_ref[0]

    for lo, ref, n in ((0, vp_ref, QBLK), (QBLK, vc_ref, tm), (QBLK + tm, vn_ref, QBLK)):
        v = ref[0].astype(F32)
        vr = pltpu.roll(v, HEAD_DIM, 1)
        low = lax.broadcasted_iota(jnp.int32, v.shape, 1) < HEAD_DIM
        zero = jnp.zeros_like(v)
        ones_low = jnp.where(low, 1.0, 0.0).astype(BF16)
        ones_high = jnp.where(low, 0.0, 1.0).astype(BF16)
        vab_buf[0, 0, lo:lo + n, 0:LANES] = jnp.where(low, v, zero).astype(BF16)
        vab_buf[0, 1, lo:lo + n, 0:LANES] = jnp.where(low, zero, vr).astype(BF16)
        vab_buf[1, 0, lo:lo + n, 0:LANES] = jnp.where(low, vr, zero).astype(BF16)
        vab_buf[1, 1, lo:lo + n, 0:LANES] = jnp.where(low, zero, v).astype(BF16)
        for c in range(N_KV):
            vab_buf[c, 0, lo:lo + n, LANES:2 * LANES] = ones_low
            vab_buf[c, 1, lo:lo + n, LANES:2 * LANES] = ones_high

    prev_halo = jnp.where(i > 0, pcp_ref[0], 0.0)
    next_halo = jnp.where(i < n_tiles - 1, pcn_ref[0], 0.0)
    zpad = jnp.zeros((POOL_MARGIN - POOL_HALO, POOL_WIDTH), F32)
    for r0, nr, u in ((0, POOL_MARGIN, jnp.concatenate([zpad, prev_halo], axis=0)),
                      (POOL_MARGIN, tm, pcc_ref[0]),
                      (POOL_MARGIN + tm, POOL_MARGIN, jnp.concatenate([next_halo, zpad], axis=0))):
        hi = u.astype(BF16)
        lo = (u - hi.astype(F32)).astype(BF16)
        for g in range(len(POOL_WINDOWS)):
            lanes = slice(g * GROUP_W, (g + 1) * GROUP_W)
            hl_buf[r0:r0 + nr, 2 * g * GROUP_W:(2 * g + 1) * GROUP_W] = hi[:, lanes]
            hl_buf[r0:r0 + nr, (2 * g + 1) * GROUP_W:(2 * g + 2) * GROUP_W] = lo[:, lanes]

    edge_buf[0, 0:POOL_HALO] = prev_halo
    edge_buf[0, POOL_HALO:] = pcc_ref[0, 0:2 * POOL_HALO]
    edge_buf[1, 0:2 * POOL_HALO] = pcc_ref[0, tm - 2 * POOL_HALO:tm]
    edge_buf[1, 2 * POOL_HALO:] = next_halo


def _mix_rows(sink_ref, q_ref, bias_ref, eye_ref, band_ref, poolw_ref, pscale_ref,
              kt_buf, vab_buf, hl_buf, edge_buf, d_buf, mix_buf, i, j0, j1, *, tm, seq):
    n_tiles = seq // tm
    nq = tm // QBLK
    nb = seq // QBLK

    for n in range(j0, j1):
        for g in range(len(POOL_WINDOWS)):
            win = hl_buf[n * QBLK:n * QBLK + 2 * QBLK, 2 * g * GROUP_W:(2 * g + 2) * GROUP_W]
            dd = jnp.dot(band_ref[g], win, preferred_element_type=F32)
            d_buf[n * QBLK:(n + 1) * QBLK, g * GROUP_W:(g + 1) * GROUP_W] = (
                dd[:, :GROUP_W] + dd[:, GROUP_W:])

    low_out = lax.broadcasted_iota(jnp.int32, (QBLK, LANES), 1) < HEAD_DIM
    eye = eye_ref[...]
    def scores(j, c):
        blk = i * nq + j
        variant = jnp.where(blk == 0, 0, jnp.where(blk == nb - 1, 2, 1))
        bias_t = bias_ref[variant]
        rows = slice(j * QBLK, (j + 1) * QBLK)
        keys = slice(j * QBLK, j * QBLK + KEYS)
        lhs = jnp.concatenate(
            [jnp.concatenate([q_ref[0, rows, (2 * c) * LANES:(2 * c + 1) * LANES], eye], axis=1),
             jnp.concatenate([q_ref[0, rows, (2 * c + 1) * LANES:(2 * c + 2) * LANES], eye],
                             axis=1)], axis=0)
        kc = kt_buf[c * HEAD_DIM:(c + 1) * HEAD_DIM, keys]
        z = jnp.zeros_like(kc)
        rhs = jnp.concatenate(
            [jnp.concatenate([kc, z], axis=1), jnp.concatenate([z, kc], axis=1), bias_t],
            axis=0)
        return jnp.dot(lhs, rhs, preferred_element_type=F32)

    def finish(j, c, s):
        rows = slice(j * QBLK, (j + 1) * QBLK)
        keys = slice(j * QBLK, j * QBLK + KEYS)
        p_rows, sink_terms = [], []
        for r in range(2):
            p_cols = []
            for par in range(2):
                sq = s[r * QBLK:(r + 1) * QBLK, par * KEYS:(par + 1) * KEYS]
                sink = sink_ref[4 * c + 2 * r + par] * LOG2E
                m = jnp.maximum(jnp.max(sq, axis=-1, keepdims=True), sink)
                p_cols.append(jnp.exp2(sq - m).astype(BF16))
                sink_terms.append(jnp.exp2(sink - m))
            p_rows.append(jnp.concatenate(p_cols, axis=1))
        pmat = jnp.concatenate(p_rows, axis=0)
        vrhs = jnp.concatenate([vab_buf[c, 0, keys], vab_buf[c, 1, keys]], axis=0)
        o = jnp.dot(pmat, vrhs, preferred_element_type=F32)
        for r in range(2):
            orow = o[r * QBLK:(r + 1) * QBLK]
            den = orow[:, LANES:] + jnp.where(low_out, sink_terms[2 * r], sink_terms[2 * r + 1])
            t = 2 * c + r
            mix_buf[rows, t * LANES:(t + 1) * LANES] = (orow[:, :LANES] / den).astype(BF16)

    units = [(j, c) for j in range(j0, j1) for c in range(N_KV)]
    s_next = scores(*units[0])
    for n, (j, c) in enumerate(units):
        s_cur = s_next
        if n + 1 < len(units):
            s_next = scores(*units[n + 1])
        finish(j, c, s_cur)

    for e, (row0, clipped) in enumerate(((0, i == 0), (tm - POOL_HALO, i == n_tiles - 1))):
        if not j0 * QBLK <= row0 < j1 * QBLK:
            continue
        tpos = i * tm + row0 + lax.broadcasted_iota(jnp.int32, (POOL_HALO, GROUP_W), 0)
        base = POOL_HALO
        for g, w in enumerate(POOL_WINDOWS):
            half = w // 2
            lanes = slice(g * GROUP_W, (g + 1) * GROUP_W)

            def wsum(lo, hi):
                acc = edge_buf[e, base + lo:base + lo + POOL_HALO, lanes]
                for k in range(lo + 1, hi + 1):
                    acc = acc + edge_buf[e, base + k:base + k + POOL_HALO, lanes]
                return acc

            def count(lo, hi):
                a = jnp.clip(tpos + lo, 0, seq)
                b = jnp.clip(tpos + hi + 1, 0, seq)
                return (b - a).astype(F32)

            mean = 0.5 * (wsum(-half, half - 1) / count(-half, half - 1)
                          + wsum(-half + 1, half) / count(-half + 1, half))
            fixed = mean - edge_buf[e, base:base + POOL_HALO, lanes]
            d_buf[row0:row0 + POOL_HALO, lanes] = jnp.where(
                clipped, fixed, d_buf[row0:row0 + POOL_HALO, lanes])

    rows = slice(j0 * QBLK, j1 * QBLK)
    for g in range(len(POOL_WINDOWS)):
        lanes = slice(g * GROUP_W, (g + 1) * GROUP_W)
        y = jnp.dot(d_buf[rows, lanes].astype(BF16), poolw_ref[g], preferred_element_type=F32)
        mix_buf[rows, Q_WIDTH + g * GROUP_W:Q_WIDTH + (g + 1) * GROUP_W] = (
            y * pscale_ref[:, lanes]).astype(BF16)


def _rope_tables(seq):
    pos = jnp.arange(seq, dtype=jnp.int32)
    inv_freq = ROPE_THETA ** (-jnp.arange(0, ROTARY_DIM, 2, dtype=F32) / ROTARY_DIM)
    ang = pos.astype(F32)[:, None] * inv_freq[None, :]
    cos, sin = jnp.cos(ang), jnp.sin(ang)
    half = ROTARY_DIM // 2
    rest = HEAD_DIM - ROTARY_DIM
    c = jnp.concatenate([cos, cos, jnp.ones((seq, rest), F32)], axis=1)
    sa = jnp.concatenate([-sin, jnp.zeros((seq, HEAD_DIM - half), F32)], axis=1)
    sb = jnp.concatenate([jnp.zeros((seq, half), F32), sin, jnp.zeros((seq, rest), F32)], axis=1)
    rep = LANES // HEAD_DIM
    return jnp.tile(c, (1, rep)), jnp.tile(sa, (1, rep)), jnp.tile(sb, (1, rep))


def _band_bias(seq):
    r = np.arange(QBLK)[:, None]
    s = np.arange(KEYS)[None, :]
    band = (s - r >= 0) & (s - r <= 2 * WINDOW)
    first = band & (s >= QBLK)
    last = band & (s < 2 * QBLK)
    out = np.stack([np.tile(np.where(m, 0.0, NEG), (1, 2)) for m in (first, band, last)])
    return jnp.asarray(out, dtype=F32).astype(BF16)


def _pool_band():
    out = np.zeros((len(POOL_WINDOWS), QBLK, 2 * QBLK), np.float32)
    r = np.arange(QBLK)
    for g, w in enumerate(POOL_WINDOWS):
        half = w // 2
        for k in range(-half, half + 1):
            out[g, r, r + POOL_MARGIN + k] = (0.5 if abs(k) == half else 1.0) / w
        out[g, r, r + POOL_MARGIN] -= 1.0
    return jnp.asarray(out).astype(BF16)


def _const_spec(shape):
    nd = len(shape)
    return pl.BlockSpec(shape, lambda *_: (0,) * nd, pipeline_mode=pl.Buffered(1))


def kernel(x, ffn1_norm, ffn1_w_gate, ffn1_w_up, ffn1_w_down, mix_norm, w_in, sink_logits,
           pool_w, pool_scale, w_out, ffn2_norm, ffn2_w_gate, ffn2_w_up, ffn2_w_down, final_norm):
    B, S, D = x.shape
    assert D == D_MODEL and S % TM == 0 and TM % QBLK == 0 and ffn1_norm.shape[0] == 1
    assert S % TM_IN == 0 and TM_IN % SUB_ROWS == 0
    tm = TM
    grid = (B, S // tm)
    in_width = w_in.shape[-1]
    params = pltpu.CompilerParams(dimension_semantics=("arbitrary", "arbitrary"),
                                  vmem_limit_bytes=VMEM_LIMIT)

    c_tab, sa_tab, sb_tab = _rope_tables(S)
    row = lambda g: g.reshape(1, -1).astype(F32)
    tile_spec = lambda width, rows=tm: pl.BlockSpec((1, rows, width), lambda b, i: (b, i, 0))
    tm_in = TM_IN
    tab_spec = pl.BlockSpec((tm_in, LANES), lambda b, i: (i, 0))

    h1, q, kt, v, pc = pl.pallas_call(
        _ffn_in_kernel,
        grid=(B, S // tm_in),
        in_specs=[
            tile_spec(D, tm_in),
            _const_spec((1, D)),
            _const_spec((D, D_FF)), _const_spec((D, D_FF)), _const_spec((D_FF, D)),
            _const_spec((1, D)),
            _const_spec((D, in_width)),
            tab_spec, tab_spec, tab_spec,
        ],
        out_specs=[
            tile_spec(D, tm_in),
            tile_spec(Q_WIDTH, tm_in),
            pl.BlockSpec((1, KV_WIDTH, tm_in), lambda b, i: (b, 0, i)),
            tile_spec(KV_WIDTH, tm_in),
            tile_spec(POOL_WIDTH, tm_in),
        ],
        out_shape=[
            jax.ShapeDtypeStruct((B, S, D), F32),
            jax.ShapeDtypeStruct((B, S, Q_WIDTH), BF16),
            jax.ShapeDtypeStruct((B, KV_WIDTH, S), BF16),
            jax.ShapeDtypeStruct((B, S, KV_WIDTH), BF16),
            jax.ShapeDtypeStruct((B, S, POOL_WIDTH), F32),
        ],
        compiler_params=params,
        name="ffn1_inproj",
    )(x, row(ffn1_norm[0]), ffn1_w_gate[0].astype(BF16), ffn1_w_up[0].astype(BF16),
      ffn1_w_down[0].astype(BF16), row(mix_norm[0]), w_in[0].astype(BF16),
      c_tab, sa_tab, sb_tab)

    qb = tm // QBLK
    pb = tm // POOL_HALO
    n_qb = S // QBLK
    n_pb = S // POOL_HALO
    mix_spec = pl.BlockSpec
    prev_q = lambda i: jnp.maximum(i * qb - 1, 0)
    next_q = lambda i: jnp.minimum((i + 1) * qb, n_qb - 1)
    prev_p = lambda i: jnp.maximum(i * pb - 1, 0)
    next_p = lambda i: jnp.minimum((i + 1) * pb, n_pb - 1)

    out = pl.pallas_call(
        functools.partial(_mix_ffn_kernel, tm=tm, seq=S),
        grid=grid,
        in_specs=[
            pl.BlockSpec(memory_space=pltpu.SMEM),
            tile_spec(D),
            mix_spec((1, tm, Q_WIDTH), lambda b, i: (b, i, 0)),
            mix_spec((1, KV_WIDTH, QBLK), lambda b, i: (b, 0, prev_q(i))),
            mix_spec((1, KV_WIDTH, tm), lambda b, i: (b, 0, i)),
            mix_spec((1, KV_WIDTH, QBLK), lambda b, i: (b, 0, next_q(i))),
            mix_spec((1, QBLK, KV_WIDTH), lambda b, i: (b, prev_q(i), 0)),
            mix_spec((1, tm, KV_WIDTH), lambda b, i: (b, i, 0)),
            mix_spec((1, QBLK, KV_WIDTH), lambda b, i: (b, next_q(i), 0)),
            mix_spec((1, POOL_HALO, POOL_WIDTH), lambda b, i: (b, prev_p(i), 0)),
            mix_spec((1, tm, POOL_WIDTH), lambda b, i: (b, i, 0)),
            mix_spec((1, POOL_HALO, POOL_WIDTH), lambda b, i: (b, next_p(i), 0)),
            _const_spec((3, QBLK, 2 * KEYS)),
            _const_spec((QBLK, QBLK)),
            _const_spec((len(POOL_WINDOWS), QBLK, 2 * QBLK)),
            _const_spec((len(POOL_WINDOWS), GROUP_W, GROUP_W)),
            _const_spec((1, POOL_WIDTH)),
            _const_spec((D, D)),
            _const_spec((1, D)),
            _const_spec((D, D_FF)), _const_spec((D, D_FF)), _const_spec((D_FF, D)),
            _const_spec((1, D)),
        ],
        out_specs=tile_spec(D),
        out_shape=jax.ShapeDtypeStruct((B, S, D), x.dtype),
        scratch_shapes=[
            pltpu.VMEM((KV_WIDTH, tm + 2 * QBLK), BF16),
            pltpu.VMEM((N_KV, 2, tm + 2 * QBLK, 2 * LANES), BF16),
            pltpu.VMEM((tm + 2 * POOL_MARGIN, 2 * POOL_WIDTH), BF16),
            pltpu.VMEM((2, 3 * POOL_HALO, POOL_WIDTH), F32),
            pltpu.VMEM((tm, POOL_WIDTH), F32),
            pltpu.VMEM((tm, D), BF16),
        ],
        compiler_params=params,
        name="mix_ffn2",
    )(sink_logits[0].astype(F32), h1, q, kt, kt, kt, v, v, v, pc, pc, pc,
      _band_bias(S), jnp.eye(QBLK, dtype=BF16), _pool_band(), pool_w[0].astype(BF16),
      row(pool_scale[0]), w_out[0].astype(BF16),
      row(ffn2_norm[0]), ffn2_w_gate[0].astype(BF16), ffn2_w_up[0].astype(BF16),
      ffn2_w_down[0].astype(BF16), row(final_norm))
    return out
```

```python
import functools

import numpy as np
import jax
import jax.numpy as jnp
from jax import lax
from jax.experimental import pallas as pl
from jax.experimental.pallas import tpu as pltpu

D_MODEL = 1024
HEAD_DIM = 64
N_HEADS = 8
N_KV = 2
Q_WIDTH = N_HEADS * HEAD_DIM
KV_WIDTH = N_KV * HEAD_DIM
WINDOW = 128
QBLK = 128
KEYS = 3 * QBLK
ROPE_THETA = 500000.0
ROTARY_DIM = HEAD_DIM // 4
POOL_WINDOWS = (2, 4, 8, 16)
POOL_HALO = 8
POOL_MARGIN = 64
POOL_WIDTH = 512
GROUP_W = 128
D_FF = 2816
EPS = 1e-6
LANES = 128
NEG = -1e30
LOG2E = 1.4426950408889634

TM = 512
TM_IN = 1024
SUB_ROWS = 256
VMEM_LIMIT = 58 * 1024 * 1024

F32 = jnp.float32
BF16 = jnp.bfloat16


def _rms(x, g):
    ms = jnp.mean(x * x, axis=-1, keepdims=True)
    return x * lax.rsqrt(ms + EPS) * g


def _swiglu(xn, wg_ref, wu_ref, wd_ref):
    gate = jnp.dot(xn, wg_ref[...], preferred_element_type=F32)
    up = jnp.dot(xn, wu_ref[...], preferred_element_type=F32)
    act = (gate * jax.nn.sigmoid(gate) * up).astype(BF16)
    return jnp.dot(act, wd_ref[...], preferred_element_type=F32)


def _rope(t, c, sa, sb):
    t_plus = pltpu.roll(t, LANES - ROTARY_DIM // 2, 1)
    t_minus = pltpu.roll(t, ROTARY_DIM // 2, 1)
    return t * c + t_plus * sa + t_minus * sb


def _ffn_in_kernel(x_ref, g1_ref, wg_ref, wu_ref, wd_ref, gm_ref, win_ref,
                   c_ref, sa_ref, sb_ref,
                   h_ref, q_ref, kt_ref, v_ref, pc_ref):
    tm = x_ref.shape[1]
    sub = SUB_ROWS
    scale = HEAD_DIM ** -0.5 * LOG2E
    def stage_norm(rows):
        return _rms(x_ref[0, rows], g1_ref[...]).astype(BF16)

    def stage_gate_up(xn):
        gate = jnp.dot(xn, wg_ref[...], preferred_element_type=F32)
        up = jnp.dot(xn, wu_ref[...], preferred_element_type=F32)
        return (gate * jax.nn.sigmoid(gate) * up).astype(BF16)

    def stage_down(rows, act):
        h = x_ref[0, rows] + 0.5 * jnp.dot(act, wd_ref[...], preferred_element_type=F32)
        h_ref[0, rows] = h
        return _rms(h, gm_ref[...]).astype(BF16)

    def stage_proj(hn):
        return jnp.dot(hn, win_ref[...], preferred_element_type=F32)

    def stage_out(rows, u):
        c, sa, sb = c_ref[rows], sa_ref[rows], sb_ref[rows]
        for t in range(Q_WIDTH // LANES):
            qt = _rope(u[:, t * LANES:(t + 1) * LANES], c, sa, sb)
            q_ref[0, rows, t * LANES:(t + 1) * LANES] = (qt * scale).astype(BF16)
        k = _rope(u[:, Q_WIDTH:Q_WIDTH + KV_WIDTH], c, sa, sb)
        kt_ref[0, :, rows] = k.T.astype(BF16)
        v_ref[0, rows] = u[:, Q_WIDTH + KV_WIDTH:Q_WIDTH + 2 * KV_WIDTH].astype(BF16)
        pc_ref[0, rows] = u[:, Q_WIDTH + 2 * KV_WIDTH:]

    for pair in range(tm // (2 * sub)):
        ra = slice((2 * pair) * sub, (2 * pair + 1) * sub)
        rb = slice((2 * pair + 1) * sub, (2 * pair + 2) * sub)
        xn_a, xn_b = stage_norm(ra), stage_norm(rb)
        act_a = stage_gate_up(xn_a)
        act_b = stage_gate_up(xn_b)
        hn_a = stage_down(ra, act_a)
        hn_b = stage_down(rb, act_b)
        u_a = stage_proj(hn_a)
        u_b = stage_proj(hn_b)
        stage_out(ra, u_a)
        stage_out(rb, u_b)


def _mix_ffn_kernel(sink_ref, h_ref, q_ref, ktp_ref, ktc_ref, ktn_ref,
                    vp_ref, vc_ref, vn_ref, pcp_ref, pcc_ref, pcn_ref,
                    bias_ref, eye_ref, band_ref, poolw_ref, pscale_ref, wout_ref,
                    g2_ref, wg_ref, wu_ref, wd_ref, gf_ref,
                    o_ref,
                    kt_buf, vab_buf, hl_buf, edge_buf, d_buf, mix_buf, *, tm, seq, n_tiles_total):
    s = pl.program_id(0)
    n_tiles = seq // tm
    halves = (slice(0, tm // 2), slice(tm // 2, tm))

    def mix_pieces(slot, i):
        _mix_prep(ktp_ref, ktc_ref, ktn_ref, vp_ref, vc_ref, vn_ref, pcp_ref, pcc_ref, pcn_ref,
                  kt_buf, vab_buf, hl_buf, edge_buf, i, tm=tm, seq=seq)
        yield
        yield from _mix_rows(sink_ref, q_ref, bias_ref, eye_ref, band_ref, poolw_ref, pscale_ref,
                             kt_buf, vab_buf, hl_buf, edge_buf, d_buf, mix_buf.at[slot], i,
                             tm=tm, seq=seq)

    def ffn_pieces(slot):
        mix_in = mix_buf.at[slot]
        hs, gates, acts = [], [], []
        for rows in halves:
            h = h_ref[0, rows] + jnp.dot(mix_in[rows], wout_ref[...], preferred_element_type=F32)
            hs.append((h, _rms(h, g2_ref[...]).astype(BF16)))
            yield
        for _, hn in hs:
            gates.append(jnp.dot(hn, wg_ref[...], preferred_element_type=F32))
            yield
        for (_, hn), gate in zip(hs, gates):
            up = jnp.dot(hn, wu_ref[...], preferred_element_type=F32)
            acts.append((gate * jax.nn.sigmoid(gate) * up).astype(BF16))
            yield
        for rows, (h, _), act in zip(halves, hs, acts):
            h = h + 0.5 * jnp.dot(act, wd_ref[...], preferred_element_type=F32)
            o_ref[0, rows] = _rms(h, gf_ref[...])
            yield

    def run(gen):
        for _ in gen:
            pass

    def alternate(ffn, mix, n_ffn, n_mix):
        done = 0
        for k in range(n_ffn):
            while done * n_ffn < (k + 1) * n_mix and next(mix, StopIteration) is not StopIteration:
                done += 1
            next(ffn, None)
        run(mix)
        run(ffn)

    n_mix_pieces = (tm // QBLK) * N_KV + 4
    n_ffn_pieces = 8

    @pl.when(s == 0)
    def _():
        run(mix_pieces(0, 0))

    @pl.when((s > 0) & (s < n_tiles_total))
    def _():
        alternate(ffn_pieces(lax.rem(s - 1, 2)), mix_pieces(lax.rem(s, 2), lax.rem(s, n_tiles)),
                  n_ffn_pieces, n_mix_pieces)

    @pl.when(s == n_tiles_total)
    def _():
        run(ffn_pieces((n_tiles_total - 1) % 2))


def _mix_prep(ktp_ref, ktc_ref, ktn_ref, vp_ref, vc_ref, vn_ref, pcp_ref, pcc_ref, pcn_ref,
              kt_buf, vab_buf, hl_buf, edge_buf, i, *, tm, seq):
    n_tiles = seq // tm

    kt_buf[:, 0:QBLK] = ktp_ref[0]
    kt_buf[:, QBLK:QBLK + tm] = ktc_ref[0]
    kt_buf[:, QBLK + tm:] = ktn_ref[0]

    for lo, ref, n in ((0, vp_ref, QBLK), (QBLK, vc_ref, tm), (QBLK + tm, vn_ref, QBLK)):
        v = ref[0].astype(F32)
        vr = pltpu.roll(v, HEAD_DIM, 1)
        low = lax.broadcasted_iota(jnp.int32, v.shape, 1) < HEAD_DIM
        zero = jnp.zeros_like(v)
        ones_low = jnp.where(low, 1.0, 0.0).astype(BF16)
        ones_high = jnp.where(low, 0.0, 1.0).astype(BF16)
        vab_buf[0, 0, lo:lo + n, 0:LANES] = jnp.where(low, v, zero).astype(BF16)
        vab_buf[0, 1, lo:lo + n, 0:LANES] = jnp.where(low, zero, vr).astype(BF16)
        vab_buf[1, 0, lo:lo + n, 0:LANES] = jnp.where(low, vr, zero).astype(BF16)
        vab_buf[1, 1, lo:lo + n, 0:LANES] = jnp.where(low, zero, v).astype(BF16)
        for c in range(N_KV):
            vab_buf[c, 0, lo:lo + n, LANES:2 * LANES] = ones_low
            vab_buf[c, 1, lo:lo + n, LANES:2 * LANES] = ones_high

    prev_halo = jnp.where(i > 0, pcp_ref[0], 0.0)
    next_halo = jnp.where(i < n_tiles - 1, pcn_ref[0], 0.0)
    zpad = jnp.zeros((POOL_MARGIN - POOL_HALO, POOL_WIDTH), F32)
    for r0, nr, u in ((0, POOL_MARGIN, jnp.concatenate([zpad, prev_halo], axis=0)),
                      (POOL_MARGIN, tm, pcc_ref[0]),
                      (POOL_MARGIN + tm, POOL_MARGIN, jnp.concatenate([next_halo, zpad], axis=0))):
        hi = u.astype(BF16)
        lo = (u - hi.astype(F32)).astype(BF16)
        for g in range(len(POOL_WINDOWS)):
            lanes = slice(g * GROUP_W, (g + 1) * GROUP_W)
            hl_buf[r0:r0 + nr, 2 * g * GROUP_W:(2 * g + 1) * GROUP_W] = hi[:, lanes]
            hl_buf[r0:r0 + nr, (2 * g + 1) * GROUP_W:(2 * g + 2) * GROUP_W] = lo[:, lanes]

    edge_buf[0, 0:POOL_HALO] = prev_halo
    edge_buf[0, POOL_HALO:] = pcc_ref[0, 0:2 * POOL_HALO]
    edge_buf[1, 0:2 * POOL_HALO] = pcc_ref[0, tm - 2 * POOL_HALO:tm]
    edge_buf[1, 2 * POOL_HALO:] = next_halo


def _mix_rows(sink_ref, q_ref, bias_ref, eye_ref, band_ref, poolw_ref, pscale_ref,
              kt_buf, vab_buf, hl_buf, edge_buf, d_buf, mix_buf, i, *, tm, seq):
    n_tiles = seq // tm
    nq = tm // QBLK
    nb = seq // QBLK
    j0, j1 = 0, nq

    for n in range(j0, j1):
        for g in range(len(POOL_WINDOWS)):
            win = hl_buf[n * QBLK:n * QBLK + 2 * QBLK, 2 * g * GROUP_W:(2 * g + 2) * GROUP_W]
            dd = jnp.dot(band_ref[g], win, preferred_element_type=F32)
            d_buf[n * QBLK:(n + 1) * QBLK, g * GROUP_W:(g + 1) * GROUP_W] = (
                dd[:, :GROUP_W] + dd[:, GROUP_W:])
    yield

    low_out =lax.broadcasted_iota(jnp.int32, (QBLK, LANES), 1) < HEAD_DIM
    eye = eye_ref[...]
    def scores(j, c):
        blk = i * nq + j
        variant = jnp.where(blk == 0, 0, jnp.where(blk == nb - 1, 2, 1))
        bias_t = bias_ref[variant]
        rows = slice(j * QBLK, (j + 1) * QBLK)
        keys = slice(j * QBLK, j * QBLK + KEYS)
        lhs = jnp.concatenate(
            [jnp.concatenate([q_ref[0, rows, (2 * c) * LANES:(2 * c + 1) * LANES], eye], axis=1),
             jnp.concatenate([q_ref[0, rows, (2 * c + 1) * LANES:(2 * c + 2) * LANES], eye],
                             axis=1)], axis=0)
        kc = kt_buf[c * HEAD_DIM:(c + 1) * HEAD_DIM, keys]
        z = jnp.zeros_like(kc)
        rhs = jnp.concatenate(
            [jnp.concatenate([kc, z], axis=1), jnp.concatenate([z, kc], axis=1), bias_t],
            axis=0)
        return jnp.dot(lhs, rhs, preferred_element_type=F32)

    def finish(j, c, s):
        rows = slice(j * QBLK, (j + 1) * QBLK)
        keys = slice(j * QBLK, j * QBLK + KEYS)
        p_rows, sink_terms = [], []
        for r in range(2):
            p_cols = []
            for par in range(2):
                sq = s[r * QBLK:(r + 1) * QBLK, par * KEYS:(par + 1) * KEYS]
                sink = sink_ref[4 * c + 2 * r + par] * LOG2E
                m = jnp.maximum(jnp.max(sq, axis=-1, keepdims=True), sink)
                p_cols.append(jnp.exp2(sq - m).astype(BF16))
                sink_terms.append(jnp.exp2(sink - m))
            p_rows.append(jnp.concatenate(p_cols, axis=1))
        pmat = jnp.concatenate(p_rows, axis=0)
        vrhs = jnp.concatenate([vab_buf[c, 0, keys], vab_buf[c, 1, keys]], axis=0)
        o = jnp.dot(pmat, vrhs, preferred_element_type=F32)
        for r in range(2):
            orow = o[r * QBLK:(r + 1) * QBLK]
            den = orow[:, LANES:] + jnp.where(low_out, sink_terms[2 * r], sink_terms[2 * r + 1])
            t = 2 * c + r
            mix_buf[rows, t * LANES:(t + 1) * LANES] = (orow[:, :LANES] / den).astype(BF16)

    units = [(j, c) for j in range(j0, j1) for c in range(N_KV)]
    s_next = scores(*units[0])
    yield
    for n, (j, c) in enumerate(units):
        s_cur = s_next
        if n + 1 < len(units):
            s_next = scores(*units[n + 1])
        yield
        finish(j, c, s_cur)
    yield

    for e, (row0, clipped) in enumerate(((0, i == 0), (tm - POOL_HALO, i == n_tiles - 1))):
        if not j0 * QBLK <= row0 < j1 * QBLK:
            continue
        tpos = i * tm + row0 + lax.broadcasted_iota(jnp.int32, (POOL_HALO, GROUP_W), 0)
        base = POOL_HALO
        for g, w in enumerate(POOL_WINDOWS):
            half = w // 2
            lanes = slice(g * GROUP_W, (g + 1) * GROUP_W)

            def wsum(lo, hi):
                acc = edge_buf[e, base + lo:base + lo + POOL_HALO, lanes]
                for k in range(lo + 1, hi + 1):
                    acc = acc + edge_buf[e, base + k:base + k + POOL_HALO, lanes]
                return acc

            def count(lo, hi):
                a = jnp.clip(tpos + lo, 0, seq)
                b = jnp.clip(tpos + hi + 1, 0, seq)
                return (b - a).astype(F32)

            mean = 0.5 * (wsum(-half, half - 1) / count(-half, half - 1)
                          + wsum(-half + 1, half) / count(-half + 1, half))
            fixed = mean - edge_buf[e, base:base + POOL_HALO, lanes]
            d_buf[row0:row0 + POOL_HALO, lanes] = jnp.where(
                clipped, fixed, d_buf[row0:row0 + POOL_HALO, lanes])

    rows = slice(j0 * QBLK, j1 * QBLK)
    for g in range(len(POOL_WINDOWS)):
        lanes = slice(g * GROUP_W, (g + 1) * GROUP_W)
        y = jnp.dot(d_buf[rows, lanes].astype(BF16), poolw_ref[g], preferred_element_type=F32)
        mix_buf[rows, Q_WIDTH + g * GROUP_W:Q_WIDTH + (g + 1) * GROUP_W] = (
            y * pscale_ref[:, lanes]).astype(BF16)


def _rope_tables(seq):
    pos = jnp.arange(seq, dtype=jnp.int32)
    inv_freq = ROPE_THETA ** (-jnp.arange(0, ROTARY_DIM, 2, dtype=F32) / ROTARY_DIM)
    ang = pos.astype(F32)[:, None] * inv_freq[None, :]
    cos, sin = jnp.cos(ang), jnp.sin(ang)
    half = ROTARY_DIM // 2
    rest = HEAD_DIM - ROTARY_DIM
    c = jnp.concatenate([cos, cos, jnp.ones((seq, rest), F32)], axis=1)
    sa = jnp.concatenate([-sin, jnp.zeros((seq, HEAD_DIM - half), F32)], axis=1)
    sb = jnp.concatenate([jnp.zeros((seq, half), F32), sin, jnp.zeros((seq, rest), F32)], axis=1)
    rep = LANES // HEAD_DIM
    return jnp.tile(c, (1, rep)), jnp.tile(sa, (1, rep)), jnp.tile(sb, (1, rep))


def _band_bias(seq):
    r = np.arange(QBLK)[:, None]
    s = np.arange(KEYS)[None, :]
    band = (s - r >= 0) & (s - r <= 2 * WINDOW)
    first = band & (s >= QBLK)
    last = band & (s < 2 * QBLK)
    out = np.stack([np.tile(np.where(m, 0.0, NEG), (1, 2)) for m in (first, band, last)])
    return jnp.asarray(out, dtype=F32).astype(BF16)


def _pool_band():
    out = np.zeros((len(POOL_WINDOWS), QBLK, 2 * QBLK), np.float32)
    r = np.arange(QBLK)
    for g, w in enumerate(POOL_WINDOWS):
        half = w // 2
        for k in range(-half, half + 1):
            out[g, r, r + POOL_MARGIN + k] = (0.5 if abs(k) == half else 1.0) / w
        out[g, r, r + POOL_MARGIN] -= 1.0
    return jnp.asarray(out).astype(BF16)


def _const_spec(shape):
    nd = len(shape)
    return pl.BlockSpec(shape, lambda *_: (0,) * nd, pipeline_mode=pl.Buffered(1))


def kernel(x, ffn1_norm, ffn1_w_gate, ffn1_w_up, ffn1_w_down, mix_norm, w_in, sink_logits,
           pool_w, pool_scale, w_out, ffn2_norm, ffn2_w_gate, ffn2_w_up, ffn2_w_down, final_norm):
    B, S, D = x.shape
    assert D == D_MODEL and S % TM == 0 and TM % QBLK == 0 and ffn1_norm.shape[0] == 1
    assert S % TM_IN == 0 and TM_IN % SUB_ROWS == 0
    tm = TM
    grid = (B, S // tm)
    in_width = w_in.shape[-1]
    params = pltpu.CompilerParams(dimension_semantics=("arbitrary", "arbitrary"),
                                  vmem_limit_bytes=VMEM_LIMIT)

    c_tab, sa_tab, sb_tab = _rope_tables(S)
    row = lambda g: g.reshape(1, -1).astype(F32)
    tile_spec = lambda width, rows=tm: pl.BlockSpec((1, rows, width), lambda b, i: (b, i, 0))
    tm_in = TM_IN
    tab_spec = pl.BlockSpec((tm_in, LANES), lambda b, i: (i, 0))

    h1, q, kt, v, pc = pl.pallas_call(
        _ffn_in_kernel,
        grid=(B, S // tm_in),
        in_specs=[
            tile_spec(D, tm_in),
            _const_spec((1, D)),
            _const_spec((D, D_FF)), _const_spec((D, D_FF)), _const_spec((D_FF, D)),
            _const_spec((1, D)),
            _const_spec((D, in_width)),
            tab_spec, tab_spec, tab_spec,
        ],
        out_specs=[
            tile_spec(D, tm_in),
            tile_spec(Q_WIDTH, tm_in),
            pl.BlockSpec((1, KV_WIDTH, tm_in), lambda b, i: (b, 0, i)),
            tile_spec(KV_WIDTH, tm_in),
            tile_spec(POOL_WIDTH, tm_in),
        ],
        out_shape=[
            jax.ShapeDtypeStruct((B, S, D), F32),
            jax.ShapeDtypeStruct((B, S, Q_WIDTH), BF16),
            jax.ShapeDtypeStruct((B, KV_WIDTH, S), BF16),
            jax.ShapeDtypeStruct((B, S, KV_WIDTH), BF16),
            jax.ShapeDtypeStruct((B, S, POOL_WIDTH), F32),
        ],
        compiler_params=params,
        name="ffn1_inproj",
    )(x, row(ffn1_norm[0]), ffn1_w_gate[0].astype(BF16), ffn1_w_up[0].astype(BF16),
      ffn1_w_down[0].astype(BF16), row(mix_norm[0]), w_in[0].astype(BF16),
      c_tab, sa_tab, sb_tab)

    qb = tm // QBLK
    pb = tm // POOL_HALO
    n_qb = S // QBLK
    n_pb = S // POOL_HALO
    nt = S // tm
    n_tiles_total = B * nt

    def mix_spec(shape, index):
        def index_map(s):
            t = jnp.minimum(s, n_tiles_total - 1)
            return index(t // nt, t % nt)
        return pl.BlockSpec(shape, index_map)

    def ffn_index(s):
        t = jnp.maximum(s - 1, 0)
        return (t // nt, t % nt, 0)

    prev_q = lambda i: jnp.maximum(i * qb - 1, 0)
    next_q = lambda i: jnp.minimum((i + 1) * qb, n_qb - 1)
    prev_p = lambda i: jnp.maximum(i * pb - 1, 0)
    next_p = lambda i: jnp.minimum((i + 1) * pb, n_pb - 1)

    out = pl.pallas_call(
        functools.partial(_mix_ffn_kernel, tm=tm, seq=S, n_tiles_total=n_tiles_total),
        grid=(n_tiles_total + 1,),
        in_specs=[
            pl.BlockSpec(memory_space=pltpu.SMEM),
            pl.BlockSpec((1, tm, D), ffn_index),
            mix_spec((1, tm, Q_WIDTH), lambda b, i: (b, i, 0)),
            mix_spec((1, KV_WIDTH, QBLK), lambda b, i: (b, 0, prev_q(i))),
            mix_spec((1, KV_WIDTH, tm), lambda b, i: (b, 0, i)),
            mix_spec((1, KV_WIDTH, QBLK), lambda b, i: (b, 0, next_q(i))),
            mix_spec((1, QBLK, KV_WIDTH), lambda b, i: (b, prev_q(i), 0)),
            mix_spec((1, tm, KV_WIDTH), lambda b, i: (b, i, 0)),
            mix_spec((1, QBLK, KV_WIDTH), lambda b, i: (b, next_q(i), 0)),
            mix_spec((1, POOL_HALO, POOL_WIDTH), lambda b, i: (b, prev_p(i), 0)),
            mix_spec((1, tm, POOL_WIDTH), lambda b, i: (b, i, 0)),
            mix_spec((1, POOL_HALO, POOL_WIDTH), lambda b, i: (b, next_p(i), 0)),
            _const_spec((3, QBLK, 2 * KEYS)),
            _const_spec((QBLK, QBLK)),
            _const_spec((len(POOL_WINDOWS), QBLK, 2 * QBLK)),
            _const_spec((len(POOL_WINDOWS), GROUP_W, GROUP_W)),
            _const_spec((1, POOL_WIDTH)),
            _const_spec((D, D)),
            _const_spec((1, D)),
            _const_spec((D, D_FF)), _const_spec((D, D_FF)), _const_spec((D_FF, D)),
            _const_spec((1, D)),
        ],
        out_specs=pl.BlockSpec((1, tm, D), ffn_index),
        out_shape=jax.ShapeDtypeStruct((B, S, D), x.dtype),
        scratch_shapes=[
            pltpu.VMEM((KV_WIDTH, tm + 2 * QBLK), BF16),
            pltpu.VMEM((N_KV, 2, tm + 2 * QBLK, 2 * LANES), BF16),
            pltpu.VMEM((tm + 2 * POOL_MARGIN, 2 * POOL_WIDTH), BF16),
            pltpu.VMEM((2, 3 * POOL_HALO, POOL_WIDTH), F32),
            pltpu.VMEM((tm, POOL_WIDTH), F32),
            pltpu.VMEM((2, tm, D), BF16),
        ],
        compiler_params=pltpu.CompilerParams(dimension_semantics=("arbitrary",),
                                             vmem_limit_bytes=VMEM_LIMIT),
        name="mix_ffn2",
    )(sink_logits[0].astype(F32), h1, q, kt, kt, kt, v, v, v, pc, pc, pc,
      _band_bias(S), jnp.eye(QBLK, dtype=BF16), _pool_band(), pool_w[0].astype(BF16),
      row(pool_scale[0]), w_out[0].astype(BF16),
      row(ffn2_norm[0]), ffn2_w_gate[0].astype(BF16), ffn2_w_up[0].astype(BF16),
      ffn2_w_down[0].astype(BF16), row(final_norm))
    return out
```

```python
import functools

import numpy as np
import jax
import jax.numpy as jnp
from jax import lax
from jax.experimental import pallas as pl
from jax.experimental.pallas import tpu as pltpu

D_MODEL = 1024
HEAD_DIM = 64
N_HEADS = 8
N_KV = 2
Q_WIDTH = N_HEADS * HEAD_DIM
KV_WIDTH = N_KV * HEAD_DIM
WINDOW = 128
QBLK = 128
KEYS = 3 * QBLK
ROPE_THETA = 500000.0
ROTARY_DIM = HEAD_DIM // 4
POOL_WINDOWS = (2, 4, 8, 16)
POOL_HALO = 8
POOL_MARGIN = 64
POOL_WIDTH = 512
GROUP_W = 128
D_FF = 2816
EPS = 1e-6
LANES = 128
NEG = -1e30
LOG2E = 1.4426950408889634

TM = 512
TM_IN = 1024
SUB_ROWS = 256
CAST_ROWS = 64
VMEM_LIMIT = 58 * 1024 * 1024

F32 = jnp.float32
BF16 = jnp.bfloat16


def _rms(x, g):
    ms = jnp.mean(x * x, axis=-1, keepdims=True)
    return x * lax.rsqrt(ms + EPS) * g


def _swiglu(xn, wg_ref, wu_ref, wd_ref):
    gate = jnp.dot(xn, wg_ref[...], preferred_element_type=F32)
    up = jnp.dot(xn, wu_ref[...], preferred_element_type=F32)
    act = (gate * jax.nn.sigmoid(gate) * up).astype(BF16)
    return jnp.dot(act, wd_ref[...], preferred_element_type=F32)


def _load_weight_bf16(src, dst, stage, sem):
    n_rows, width = src.shape
    n_chunks = n_rows // CAST_ROWS

    def chunk_copy(k, slot):
        return pltpu.make_async_copy(src.at[pl.ds(k * CAST_ROWS, CAST_ROWS)],
                                     stage.at[slot, :, pl.ds(0, width)], sem.at[slot])

    chunk_copy(0, 0).start()

    def body(k, carry):
        slot = lax.rem(k, 2)

        @pl.when(k + 1 < n_chunks)
        def _():
            chunk_copy(k + 1, 1 - slot).start()

        chunk_copy(k, slot).wait()
        r0 = pl.multiple_of(k * CAST_ROWS, CAST_ROWS)
        dst[pl.ds(r0, CAST_ROWS), :] = stage[slot, :, 0:width].astype(BF16)
        return carry

    lax.fori_loop(0, n_chunks, body, 0)


def _rope(t, c, sa, sb):
    t_plus = pltpu.roll(t, LANES - ROTARY_DIM // 2, 1)
    t_minus = pltpu.roll(t, ROTARY_DIM // 2, 1)
    return t * c + t_plus * sa + t_minus * sb


def _ffn_in_kernel(x_ref, g1_ref, wg_hbm, wu_hbm, wd_hbm, gm_ref, win_hbm,
                   c_ref, sa_ref, sb_ref,
                   h_ref, q_ref, kt_ref, v_ref, pc_ref,
                   wg_ref, wu_ref, wd_ref, win_ref, stage, sem):
    @pl.when((pl.program_id(0) == 0) & (pl.program_id(1) == 0))
    def _():
        for src, dst in ((wg_hbm, wg_ref), (wu_hbm, wu_ref), (wd_hbm, wd_ref), (win_hbm, win_ref)):
            _load_weight_bf16(src, dst, stage, sem)

    tm = x_ref.shape[1]
    sub = SUB_ROWS
    scale = HEAD_DIM ** -0.5 * LOG2E
    def stage_norm(rows):
        return _rms(x_ref[0, rows], g1_ref[...]).astype(BF16)

    def stage_gate_up(xn):
        gate = jnp.dot(xn, wg_ref[...], preferred_element_type=F32)
        up = jnp.dot(xn, wu_ref[...], preferred_element_type=F32)
        return (gate * jax.nn.sigmoid(gate) * up).astype(BF16)

    def stage_down(rows, act):
        h = x_ref[0, rows] + 0.5 * jnp.dot(act, wd_ref[...], preferred_element_type=F32)
        h_ref[0, rows] = h
        return _rms(h, gm_ref[...]).astype(BF16)

    def stage_proj(hn):
        return jnp.dot(hn, win_ref[...], preferred_element_type=F32)

    def stage_out(rows, u):
        c, sa, sb = c_ref[rows], sa_ref[rows], sb_ref[rows]
        for t in range(Q_WIDTH // LANES):
            qt = _rope(u[:, t * LANES:(t + 1) * LANES], c, sa, sb)
            q_ref[0, rows, t * LANES:(t + 1) * LANES] = (qt * scale).astype(BF16)
        k = _rope(u[:, Q_WIDTH:Q_WIDTH + KV_WIDTH], c, sa, sb)
        kt_ref[0, :, rows] = k.T.astype(BF16)
        v_ref[0, rows] = u[:, Q_WIDTH + KV_WIDTH:Q_WIDTH + 2 * KV_WIDTH].astype(BF16)
        pc_ref[0, rows] = u[:, Q_WIDTH + 2 * KV_WIDTH:]

    for pair in range(tm // (2 * sub)):
        ra = slice((2 * pair) * sub, (2 * pair + 1) * sub)
        rb = slice((2 * pair + 1) * sub, (2 * pair + 2) * sub)
        xn_a, xn_b = stage_norm(ra), stage_norm(rb)
        act_a = stage_gate_up(xn_a)
        act_b = stage_gate_up(xn_b)
        hn_a = stage_down(ra, act_a)
        hn_b = stage_down(rb, act_b)
        u_a = stage_proj(hn_a)
        u_b = stage_proj(hn_b)
        stage_out(ra, u_a)
        stage_out(rb, u_b)


def _mix_ffn_kernel(sink_ref, h_ref, q_ref, ktp_ref, ktc_ref, ktn_ref,
                    vp_ref, vc_ref, vn_ref, pcp_ref, pcc_ref, pcn_ref,
                    bias_ref, eye_ref, band_ref, poolw_ref, pscale_ref, wout_hbm,
                    g2_ref, wg_hbm, wu_hbm, wd_hbm, gf_ref,
                    o_ref,
                    kt_buf, vab_buf, hl_buf, edge_buf, d_buf, mix_buf,
                    wout_ref, wg_ref, wu_ref, wd_ref, stage, sem, *, tm, seq):
    @pl.when((pl.program_id(0) == 0) & (pl.program_id(1) == 0))
    def _():
        for src, dst in ((wout_hbm, wout_ref), (wg_hbm, wg_ref), (wu_hbm, wu_ref),
                         (wd_hbm, wd_ref)):
            _load_weight_bf16(src, dst, stage, sem)

    i = pl.program_id(1)
    _mix_prep(ktp_ref, ktc_ref, ktn_ref, vp_ref, vc_ref, vn_ref, pcp_ref, pcc_ref, pcn_ref,
              kt_buf, vab_buf, hl_buf, edge_buf, i, tm=tm, seq=seq)
    mix_rows = functools.partial(
        _mix_rows, sink_ref, q_ref, bias_ref, eye_ref, band_ref, poolw_ref, pscale_ref,
        kt_buf, vab_buf, hl_buf, edge_buf, d_buf, mix_buf, i, tm=tm, seq=seq)

    sub = tm // 2
    sub_blocks = sub // QBLK
    halves = (slice(0, sub), slice(sub, tm))

    def stage_out_proj(rows):
        h = h_ref[0, rows] + jnp.dot(mix_buf[rows], wout_ref[...], preferred_element_type=F32)
        return h, _rms(h, g2_ref[...]).astype(BF16)

    def stage_gate_up(hn):
        gate = jnp.dot(hn, wg_ref[...], preferred_element_type=F32)
        up = jnp.dot(hn, wu_ref[...], preferred_element_type=F32)
        return (gate * jax.nn.sigmoid(gate) * up).astype(BF16)

    def stage_down(rows, h, act):
        h = h + 0.5 * jnp.dot(act, wd_ref[...], preferred_element_type=F32)
        o_ref[0, rows] = _rms(h, gf_ref[...])

    mix_rows(0, 2 * sub_blocks)
    hs = [stage_out_proj(rows) for rows in halves]
    acts = [stage_gate_up(hn) for _, hn in hs]
    for rows, (h, _), act in zip(halves, hs, acts):
        stage_down(rows, h, act)


def _mix_prep(ktp_ref, ktc_ref, ktn_ref, vp_ref, vc_ref, vn_ref, pcp_ref, pcc_ref, pcn_ref,
              kt_buf, vab_buf, hl_buf, edge_buf, i, *, tm, seq):
    n_tiles = seq // tm

    kt_buf[:, 0:QBLK] = ktp_ref[0]
    kt_buf[:, QBLK:QBLK + tm] = ktc_ref[0]
    kt_buf[:, QBLK + tm:] = ktn_ref[0]

    for lo, ref, n in ((0, vp_ref, QBLK), (QBLK, vc_ref, tm), (QBLK + tm, vn_ref, QBLK)):
        v = ref[0].astype(F32)
        vr = pltpu.roll(v, HEAD_DIM, 1)
        low = lax.broadcasted_iota(jnp.int32, v.shape, 1) < HEAD_DIM
        zero = jnp.zeros_like(v)
        ones_low = jnp.where(low, 1.0, 0.0).astype(BF16)
        ones_high = jnp.where(low, 0.0, 1.0).astype(BF16)
        vab_buf[0, 0, lo:lo + n, 0:LANES] = jnp.where(low, v, zero).astype(BF16)
        vab_buf[0, 1, lo:lo + n, 0:LANES] = jnp.where(low, zero, vr).astype(BF16)
        vab_buf[1, 0, lo:lo + n, 0:LANES] = jnp.where(low, vr, zero).astype(BF16)
        vab_buf[1, 1, lo:lo + n, 0:LANES] = jnp.where(low, zero, v).astype(BF16)
        for c in range(N_KV):
            vab_buf[c, 0, lo:lo + n, LANES:2 * LANES] = ones_low
            vab_buf[c, 1, lo:lo + n, LANES:2 * LANES] = ones_high

    prev_halo = jnp.where(i > 0, pcp_ref[0], 0.0)
    next_halo = jnp.where(i < n_tiles - 1, pcn_ref[0], 0.0)
    zpad = jnp.zeros((POOL_MARGIN - POOL_HALO, POOL_WIDTH), F32)
    for r0, nr, u in ((0, POOL_MARGIN, jnp.concatenate([zpad, prev_halo], axis=0)),
                      (POOL_MARGIN, tm, pcc_ref[0]),
                      (POOL_MARGIN + tm, POOL_MARGIN, jnp.concatenate([next_halo, zpad], axis=0))):
        hi = u.astype(BF16)
        lo = (u - hi.astype(F32)).astype(BF16)
        for g in range(len(POOL_WINDOWS)):
            lanes = slice(g * GROUP_W, (g + 1) * GROUP_W)
            hl_buf[r0:r0 + nr, 2 * g * GROUP_W:(2 * g + 1) * GROUP_W] = hi[:, lanes]
            hl_buf[r0:r0 + nr, (2 * g + 1) * GROUP_W:(2 * g + 2) * GROUP_W] = lo[:, lanes]

    edge_buf[0, 0:POOL_HALO] = prev_halo
    edge_buf[0, POOL_HALO:] = pcc_ref[0, 0:2 * POOL_HALO]
    edge_buf[1, 0:2 * POOL_HALO] = pcc_ref[0, tm - 2 * POOL_HALO:tm]
    edge_buf[1, 2 * POOL_HALO:] = next_halo


def _mix_rows(sink_ref, q_ref, bias_ref, eye_ref, band_ref, poolw_ref, pscale_ref,
              kt_buf, vab_buf, hl_buf, edge_buf, d_buf, mix_buf, i, j0, j1, *, tm, seq):
    n_tiles = seq // tm
    nq = tm // QBLK
    nb = seq // QBLK

    for n in range(j0, j1):
        for g in range(len(POOL_WINDOWS)):
            win = hl_buf[n * QBLK:n * QBLK + 2 * QBLK, 2 * g * GROUP_W:(2 * g + 2) * GROUP_W]
            dd = jnp.dot(band_ref[g], win, preferred_element_type=F32)
            d_buf[n * QBLK:(n + 1) * QBLK, g * GROUP_W:(g + 1) * GROUP_W] = (
                dd[:, :GROUP_W] + dd[:, GROUP_W:])

    low_out = lax.broadcasted_iota(jnp.int32, (QBLK, LANES), 1) < HEAD_DIM
    eye = eye_ref[...]
    def scores(j, c):
        blk = i * nq + j
        variant = jnp.where(blk == 0, 0, jnp.where(blk == nb - 1, 2, 1))
        bias_t = bias_ref[variant]
        rows = slice(j * QBLK, (j + 1) * QBLK)
        keys = slice(j * QBLK, j * QBLK + KEYS)
        lhs = jnp.concatenate(
            [jnp.concatenate([q_ref[0, rows, (2 * c) * LANES:(2 * c + 1) * LANES], eye], axis=1),
             jnp.concatenate([q_ref[0, rows, (2 * c + 1) * LANES:(2 * c + 2) * LANES], eye],
                             axis=1)], axis=0)
        kc = kt_buf[c * HEAD_DIM:(c + 1) * HEAD_DIM, keys]
        z = jnp.zeros_like(kc)
        rhs = jnp.concatenate(
            [jnp.concatenate([kc, z], axis=1), jnp.concatenate([z, kc], axis=1), bias_t],
            axis=0)
        return jnp.dot(lhs, rhs, preferred_element_type=F32)

    def finish(j, c, s):
        rows = slice(j * QBLK, (j + 1) * QBLK)
        keys = slice(j * QBLK, j * QBLK + KEYS)
        p_rows, sink_terms = [], []
        for r in range(2):
            p_cols = []
            for par in range(2):
                sq = s[r * QBLK:(r + 1) * QBLK, par * KEYS:(par + 1) * KEYS]
                sink = sink_ref[4 * c + 2 * r + par] * LOG2E
                m = jnp.maximum(jnp.max(sq, axis=-1, keepdims=True), sink)
                p_cols.append(jnp.exp2(sq - m).astype(BF16))
                sink_terms.append(jnp.exp2(sink - m))
            p_rows.append(jnp.concatenate(p_cols, axis=1))
        pmat = jnp.concatenate(p_rows, axis=0)
        vrhs = jnp.concatenate([vab_buf[c, 0, keys], vab_buf[c, 1, keys]], axis=0)
        o = jnp.dot(pmat, vrhs, preferred_element_type=F32)
        for r in range(2):
            orow = o[r * QBLK:(r + 1) * QBLK]
            den = orow[:, LANES:] + jnp.where(low_out, sink_terms[2 * r], sink_terms[2 * r + 1])
            t = 2 * c + r
            mix_buf[rows, t * LANES:(t + 1) * LANES] = (orow[:, :LANES] / den).astype(BF16)

    units = [(j, c) for j in range(j0, j1) for c in range(N_KV)]
    s_next = scores(*units[0])
    for n, (j, c) in enumerate(units):
        s_cur = s_next
        if n + 1 < len(units):
            s_next = scores(*units[n + 1])
        finish(j, c, s_cur)

    for e, (row0, clipped) in enumerate(((0, i == 0), (tm - POOL_HALO, i == n_tiles - 1))):
        if not j0 * QBLK <= row0 < j1 * QBLK:
            continue
        tpos = i * tm + row0 + lax.broadcasted_iota(jnp.int32, (POOL_HALO, GROUP_W), 0)
        base = POOL_HALO
        for g, w in enumerate(POOL_WINDOWS):
            half = w // 2
            lanes = slice(g * GROUP_W, (g + 1) * GROUP_W)

            def wsum(lo, hi):
                acc = edge_buf[e, base + lo:base + lo + POOL_HALO, lanes]
                for k in range(lo + 1, hi + 1):
                    acc = acc + edge_buf[e, base + k:base + k + POOL_HALO, lanes]
                return acc

            def count(lo, hi):
                a = jnp.clip(tpos + lo, 0, seq)
                b = jnp.clip(tpos + hi + 1, 0, seq)
                return (b - a).astype(F32)

            mean = 0.5 * (wsum(-half, half - 1) / count(-half, half - 1)
                          + wsum(-half + 1, half) / count(-half + 1, half))
            fixed = mean - edge_buf[e, base:base + POOL_HALO, lanes]
            d_buf[row0:row0 + POOL_HALO, lanes] = jnp.where(
                clipped, fixed, d_buf[row0:row0 + POOL_HALO, lanes])

    rows = slice(j0 * QBLK, j1 * QBLK)
    for g in range(len(POOL_WINDOWS)):
        lanes = slice(g * GROUP_W, (g + 1) * GROUP_W)
        y = jnp.dot(d_buf[rows, lanes].astype(BF16), poolw_ref[g], preferred_element_type=F32)
        mix_buf[rows, Q_WIDTH + g * GROUP_W:Q_WIDTH + (g + 1) * GROUP_W] = (
            y * pscale_ref[:, lanes]).astype(BF16)


def _rope_tables(seq):
    pos = jnp.arange(seq, dtype=jnp.int32)
    inv_freq = ROPE_THETA ** (-jnp.arange(0, ROTARY_DIM, 2, dtype=F32) / ROTARY_DIM)
    ang = pos.astype(F32)[:, None] * inv_freq[None, :]
    cos, sin = jnp.cos(ang), jnp.sin(ang)
    half = ROTARY_DIM // 2
    rest = HEAD_DIM - ROTARY_DIM
    c = jnp.concatenate([cos, cos, jnp.ones((seq, rest), F32)], axis=1)
    sa = jnp.concatenate([-sin, jnp.zeros((seq, HEAD_DIM - half), F32)], axis=1)
    sb = jnp.concatenate([jnp.zeros((seq, half), F32), sin, jnp.zeros((seq, rest), F32)], axis=1)
    rep = LANES // HEAD_DIM
    return jnp.tile(c, (1, rep)), jnp.tile(sa, (1, rep)), jnp.tile(sb, (1, rep))


def _band_bias(seq):
    r = np.arange(QBLK)[:, None]
    s = np.arange(KEYS)[None, :]
    band = (s - r >= 0) & (s - r <= 2 * WINDOW)
    first = band & (s >= QBLK)
    last = band & (s < 2 * QBLK)
    out = np.stack([np.tile(np.where(m, 0.0, NEG), (1, 2)) for m in (first, band, last)])
    return jnp.asarray(out, dtype=F32).astype(BF16)


def _pool_band():
    out = np.zeros((len(POOL_WINDOWS), QBLK, 2 * QBLK), np.float32)
    r = np.arange(QBLK)
    for g, w in enumerate(POOL_WINDOWS):
        half = w // 2
        for k in range(-half, half + 1):
            out[g, r, r + POOL_MARGIN + k] = (0.5 if abs(k) == half else 1.0) / w
        out[g, r, r + POOL_MARGIN] -= 1.0
    return jnp.asarray(out).astype(BF16)


def _const_spec(shape):
    nd = len(shape)
    return pl.BlockSpec(shape, lambda *_: (0,) * nd, pipeline_mode=pl.Buffered(1))


def kernel(x, ffn1_norm, ffn1_w_gate, ffn1_w_up, ffn1_w_down, mix_norm, w_in, sink_logits,
           pool_w, pool_scale, w_out, ffn2_norm, ffn2_w_gate, ffn2_w_up, ffn2_w_down, final_norm):
    B, S, D = x.shape
    assert D == D_MODEL and S % TM == 0 and TM % QBLK == 0 and ffn1_norm.shape[0] == 1
    assert S % TM_IN == 0 and TM_IN % SUB_ROWS == 0
    tm = TM
    grid = (B, S // tm)
    in_width = w_in.shape[-1]
    params = pltpu.CompilerParams(dimension_semantics=("arbitrary", "arbitrary"),
                                  vmem_limit_bytes=VMEM_LIMIT)

    c_tab, sa_tab, sb_tab = _rope_tables(S)
    row = lambda g: g.reshape(1, -1).astype(F32)
    tile_spec = lambda width, rows=tm: pl.BlockSpec((1, rows, width), lambda b, i: (b, i, 0))
    tm_in = TM_IN
    tab_spec = pl.BlockSpec((tm_in, LANES), lambda b, i: (i, 0))
    hbm_spec = pl.BlockSpec(memory_space=pl.ANY)

    h1, q, kt, v, pc = pl.pallas_call(
        _ffn_in_kernel,
        grid=(B, S // tm_in),
        in_specs=[
            tile_spec(D, tm_in),
            _const_spec((1, D)),
            hbm_spec, hbm_spec, hbm_spec,
            _const_spec((1, D)),
            hbm_spec,
            tab_spec, tab_spec, tab_spec,
        ],
        out_specs=[
            tile_spec(D, tm_in),
            tile_spec(Q_WIDTH, tm_in),
            pl.BlockSpec((1, KV_WIDTH, tm_in), lambda b, i: (b, 0, i)),
            tile_spec(KV_WIDTH, tm_in),
            tile_spec(POOL_WIDTH, tm_in),
        ],
        out_shape=[
            jax.ShapeDtypeStruct((B, S, D), F32),
            jax.ShapeDtypeStruct((B, S, Q_WIDTH), BF16),
            jax.ShapeDtypeStruct((B, KV_WIDTH, S), BF16),
            jax.ShapeDtypeStruct((B, S, KV_WIDTH), BF16),
            jax.ShapeDtypeStruct((B, S, POOL_WIDTH), F32),
        ],
        scratch_shapes=[
            pltpu.VMEM((D, D_FF), BF16), pltpu.VMEM((D, D_FF), BF16), pltpu.VMEM((D_FF, D), BF16),
            pltpu.VMEM((D, in_width), BF16),
            pltpu.VMEM((2, CAST_ROWS, D_FF), F32),
            pltpu.SemaphoreType.DMA((2,)),
        ],
        compiler_params=params,
        name="ffn1_inproj",
    )(x, row(ffn1_norm[0]), ffn1_w_gate[0], ffn1_w_up[0], ffn1_w_down[0], row(mix_norm[0]),
      w_in[0], c_tab, sa_tab, sb_tab)

    qb = tm // QBLK
    pb = tm // POOL_HALO
    n_qb = S // QBLK
    n_pb = S // POOL_HALO
    mix_spec = pl.BlockSpec
    prev_q = lambda i: jnp.maximum(i * qb - 1, 0)
    next_q = lambda i: jnp.minimum((i + 1) * qb, n_qb - 1)
    prev_p = lambda i: jnp.maximum(i * pb - 1, 0)
    next_p = lambda i: jnp.minimum((i + 1) * pb, n_pb - 1)

    out = pl.pallas_call(
        functools.partial(_mix_ffn_kernel, tm=tm, seq=S),
        grid=grid,
        in_specs=[
            pl.BlockSpec(memory_space=pltpu.SMEM),
            tile_spec(D),
            mix_spec((1, tm, Q_WIDTH), lambda b, i: (b, i, 0)),
            mix_spec((1, KV_WIDTH, QBLK), lambda b, i: (b, 0, prev_q(i))),
            mix_spec((1, KV_WIDTH, tm), lambda b, i: (b, 0, i)),
            mix_spec((1, KV_WIDTH, QBLK), lambda b, i: (b, 0, next_q(i))),
            mix_spec((1, QBLK, KV_WIDTH), lambda b, i: (b, prev_q(i), 0)),
            mix_spec((1, tm, KV_WIDTH), lambda b, i: (b, i, 0)),
            mix_spec((1, QBLK, KV_WIDTH), lambda b, i: (b, next_q(i), 0)),
            mix_spec((1, POOL_HALO, POOL_WIDTH), lambda b, i: (b, prev_p(i), 0)),
            mix_spec((1, tm, POOL_WIDTH), lambda b, i: (b, i, 0)),
            mix_spec((1, POOL_HALO, POOL_WIDTH), lambda b, i: (b, next_p(i), 0)),
            _const_spec((3, QBLK, 2 * KEYS)),
            _const_spec((QBLK, QBLK)),
            _const_spec((len(POOL_WINDOWS), QBLK, 2 * QBLK)),
            _const_spec((len(POOL_WINDOWS), GROUP_W, GROUP_W)),
            _const_spec((1, POOL_WIDTH)),
            hbm_spec,
            _const_spec((1, D)),
            hbm_spec, hbm_spec, hbm_spec,
            _const_spec((1, D)),
        ],
        out_specs=tile_spec(D),
        out_shape=jax.ShapeDtypeStruct((B, S, D), x.dtype),
        scratch_shapes=[
            pltpu.VMEM((KV_WIDTH, tm + 2 * QBLK), BF16),
            pltpu.VMEM((N_KV, 2, tm + 2 * QBLK, 2 * LANES), BF16),
            pltpu.VMEM((tm + 2 * POOL_MARGIN, 2 * POOL_WIDTH), BF16),
            pltpu.VMEM((2, 3 * POOL_HALO, POOL_WIDTH), F32),
            pltpu.VMEM((tm, POOL_WIDTH), F32),
            pltpu.VMEM((tm, D), BF16),
            pltpu.VMEM((D, D), BF16),
            pltpu.VMEM((D, D_FF), BF16), pltpu.VMEM((D, D_FF), BF16), pltpu.VMEM((D_FF, D), BF16),
            pltpu.VMEM((2, CAST_ROWS, D_FF), F32),
            pltpu.SemaphoreType.DMA((2,)),
        ],
        compiler_params=params,
        name="mix_ffn2",
    )(sink_logits[0].astype(F32), h1, q, kt, kt, kt, v, v, v, pc, pc, pc,
      _band_bias(S), jnp.eye(QBLK, dtype=BF16), _pool_band(), pool_w[0].astype(BF16),
      row(pool_scale[0]), w_out[0],
      row(ffn2_norm[0]), ffn2_w_gate[0], ffn2_w_up[0], ffn2_w_down[0], row(final_norm))
    return out
```

```python
import functools

import numpy as np
import jax
import jax.numpy as jnp
from jax import lax
from jax.experimental import pallas as pl
from jax.experimental.pallas import tpu as pltpu

D_MODEL = 1024
HEAD_DIM = 64
N_HEADS = 8
N_KV = 2
Q_WIDTH = N_HEADS * HEAD_DIM
KV_WIDTH = N_KV * HEAD_DIM
WINDOW = 128
QBLK = 128
KEYS = 3 * QBLK
ROPE_THETA = 500000.0
ROTARY_DIM = HEAD_DIM // 4
POOL_WINDOWS = (2, 4, 8, 16)
POOL_HALO = 8
POOL_MARGIN = 64
POOL_WIDTH = 512
GROUP_W = 128
D_FF = 2816
EPS = 1e-6
LANES = 128
NEG = -1e30
LOG2E = 1.4426950408889634

TM = 512
TM_IN = 512
SUB_ROWS = 256
CAST_ROWS = 256
CAST_SLOTS = 3
VMEM_LIMIT = 58 * 1024 * 1024

F32 = jnp.float32
BF16 = jnp.bfloat16


def _rms(x, g):
    ms = jnp.mean(x * x, axis=-1, keepdims=True)
    return x * lax.rsqrt(ms + EPS) * g


def _swiglu(xn, wg_ref, wu_ref, wd_ref):
    gate = jnp.dot(xn, wg_ref[...], preferred_element_type=F32)
    up = jnp.dot(xn, wu_ref[...], preferred_element_type=F32)
    act = (gate * jax.nn.sigmoid(gate) * up).astype(BF16)
    return jnp.dot(act, wd_ref[...], preferred_element_type=F32)


def _load_weight_bf16(src, dst, stage, sem):
    n_rows, width = src.shape
    n_chunks = n_rows // CAST_ROWS
    ahead = CAST_SLOTS - 1

    def chunk_copy(k, slot):
        return pltpu.make_async_copy(src.at[pl.ds(k * CAST_ROWS, CAST_ROWS)],
                                     stage.at[slot, :, pl.ds(0, width)], sem.at[slot])

    for k in range(min(ahead, n_chunks)):
        chunk_copy(k, k).start()

    def body(k, carry):
        slot = lax.rem(k, CAST_SLOTS)

        @pl.when(k + ahead < n_chunks)
        def _():
            chunk_copy(k + ahead, lax.rem(k + ahead, CAST_SLOTS)).start()

        chunk_copy(k, slot).wait()
        r0 = pl.multiple_of(k * CAST_ROWS, CAST_ROWS)
        dst[pl.ds(r0, CAST_ROWS), :] = stage[slot, :, 0:width].astype(BF16)
        return carry

    lax.fori_loop(0, n_chunks, body, 0)


def _load_weights_bf16(pairs):
    def scoped(stage, sem):
        for src, dst in pairs:
            _load_weight_bf16(src, dst, stage, sem)

    pl.run_scoped(scoped, pltpu.VMEM((CAST_SLOTS, CAST_ROWS, D_FF), F32),
                  pltpu.SemaphoreType.DMA((CAST_SLOTS,)))


def _rope(t, c, sa, sb):
    t_plus = pltpu.roll(t, LANES - ROTARY_DIM // 2, 1)
    t_minus = pltpu.roll(t, ROTARY_DIM // 2, 1)
    return t * c + t_plus * sa + t_minus * sb


def _ffn_in_kernel(x_ref, g1_ref, wg_hbm, wu_hbm, wd_hbm, gm_ref, win_hbm,
                   c_ref, sa_ref, sb_ref,
                   h_ref, q_ref, kt_ref, v_ref, pc_ref,
                   wg_ref, wu_ref, wd_ref, win_ref):
    @pl.when((pl.program_id(0) == 0) & (pl.program_id(1) == 0))
    def _():
        _load_weights_bf16(((wg_hbm, wg_ref), (wu_hbm, wu_ref), (wd_hbm, wd_ref),
                            (win_hbm, win_ref)))

    tm = x_ref.shape[1]
    sub = SUB_ROWS
    scale = HEAD_DIM ** -0.5 * LOG2E
    def stage_norm(rows):
        return _rms(x_ref[0, rows], g1_ref[...]).astype(BF16)

    def stage_gate_up(xn):
        gate = jnp.dot(xn, wg_ref[...], preferred_element_type=F32)
        up = jnp.dot(xn, wu_ref[...], preferred_element_type=F32)
        return (gate * jax.nn.sigmoid(gate) * up).astype(BF16)

    def stage_down(rows, act):
        h = x_ref[0, rows] + 0.5 * jnp.dot(act, wd_ref[...], preferred_element_type=F32)
        h_ref[0, rows] = h
        return _rms(h, gm_ref[...]).astype(BF16)

    def stage_proj(hn):
        return jnp.dot(hn, win_ref[...], preferred_element_type=F32)

    def stage_out(rows, u):
        c, sa, sb = c_ref[rows], sa_ref[rows], sb_ref[rows]
        for t in range(Q_WIDTH // LANES):
            qt = _rope(u[:, t * LANES:(t + 1) * LANES], c, sa, sb)
            q_ref[0, rows, t * LANES:(t + 1) * LANES] = (qt * scale).astype(BF16)
        k = _rope(u[:, Q_WIDTH:Q_WIDTH + KV_WIDTH], c, sa, sb)
        kt_ref[0, :, rows] = k.T.astype(BF16)
        v_ref[0, rows] = u[:, Q_WIDTH + KV_WIDTH:Q_WIDTH + 2 * KV_WIDTH].astype(BF16)
        pc_ref[0, rows] = u[:, Q_WIDTH + 2 * KV_WIDTH:]

    for pair in range(tm // (2 * sub)):
        ra = slice((2 * pair) * sub, (2 * pair + 1) * sub)
        rb = slice((2 * pair + 1) * sub, (2 * pair + 2) * sub)
        xn_a, xn_b = stage_norm(ra), stage_norm(rb)
        act_a = stage_gate_up(xn_a)
        act_b = stage_gate_up(xn_b)
        hn_a = stage_down(ra, act_a)
        hn_b = stage_down(rb, act_b)
        u_a = stage_proj(hn_a)
        u_b = stage_proj(hn_b)
        stage_out(ra, u_a)
        stage_out(rb, u_b)


def _mix_ffn_kernel(sink_ref, h_ref, q_ref, ktp_ref, ktc_ref, ktn_ref,
                    vp_ref, vc_ref, vn_ref, pcp_ref, pcc_ref, pcn_ref,
                    bias_ref, eye_ref, band_ref, poolw_ref, pscale_ref, wout_hbm,
                    g2_ref, wg_hbm, wu_hbm, wd_hbm, gf_ref,
                    o_ref,
                    kt_buf, vab_buf, hl_buf, edge_buf, d_buf, mix_buf,
                    wout_ref, wg_ref, wu_ref, wd_ref, *, tm, seq):
    @pl.when((pl.program_id(0) == 0) & (pl.program_id(1) == 0))
    def _():
        _load_weights_bf16(((wout_hbm, wout_ref), (wg_hbm, wg_ref), (wu_hbm, wu_ref),
                            (wd_hbm, wd_ref)))

    i = pl.program_id(1)
    _mix_prep(ktp_ref, ktc_ref, ktn_ref, vp_ref, vc_ref, vn_ref, pcp_ref, pcc_ref, pcn_ref,
              kt_buf, vab_buf, hl_buf, edge_buf, i, tm=tm, seq=seq)
    mix_rows = functools.partial(
        _mix_rows, sink_ref, q_ref, bias_ref, eye_ref, band_ref, poolw_ref, pscale_ref,
        kt_buf, vab_buf, hl_buf, edge_buf, d_buf, mix_buf, i, tm=tm, seq=seq)

    sub = tm // 2
    sub_blocks = sub // QBLK
    halves = (slice(0, sub), slice(sub, tm))

    def stage_out_proj(rows):
        h = h_ref[0, rows] + jnp.dot(mix_buf[rows], wout_ref[...], preferred_element_type=F32)
        return h, _rms(h, g2_ref[...]).astype(BF16)

    def stage_gate_up(hn):
        gate = jnp.dot(hn, wg_ref[...], preferred_element_type=F32)
        up = jnp.dot(hn, wu_ref[...], preferred_element_type=F32)
        return (gate * jax.nn.sigmoid(gate) * up).astype(BF16)

    def stage_down(rows, h, act):
        h = h + 0.5 * jnp.dot(act, wd_ref[...], preferred_element_type=F32)
        o_ref[0, rows] = _rms(h, gf_ref[...])

    mix_rows(0, 2 * sub_blocks)
    hs = [stage_out_proj(rows) for rows in halves]
    acts = [stage_gate_up(hn) for _, hn in hs]
    for rows, (h, _), act in zip(halves, hs, acts):
        stage_down(rows, h, act)


def _mix_prep(ktp_ref, ktc_ref, ktn_ref, vp_ref, vc_ref, vn_ref, pcp_ref, pcc_ref, pcn_ref,
              kt_buf, vab_buf, hl_buf, edge_buf, i, *, tm, seq):
    n_tiles = seq // tm

    kt_buf[:, 0:QBLK] = ktp_ref[0]
    kt_buf[:, QBLK:QBLK + tm] = ktc_ref[0]
    kt_buf[:, QBLK + tm:] = ktn_ref[0]

    for lo, ref, n in ((0, vp_ref, QBLK), (QBLK, vc_ref, tm), (QBLK + tm, vn_ref, QBLK)):
        v = ref[0].astype(F32)
        vr = pltpu.roll(v, HEAD_DIM, 1)
        low = lax.broadcasted_iota(jnp.int32, v.shape, 1) < HEAD_DIM
        zero = jnp.zeros_like(v)
        ones_low = jnp.where(low, 1.0, 0.0).astype(BF16)
        ones_high = jnp.where(low, 0.0, 1.0).astype(BF16)
        vab_buf[0, 0, lo:lo + n, 0:LANES] = jnp.where(low, v, zero).astype(BF16)
        vab_buf[0, 1, lo:lo + n, 0:LANES] = jnp.where(low, zero, vr).astype(BF16)
        vab_buf[1, 0, lo:lo + n, 0:LANES] = jnp.where(low, vr, zero).astype(BF16)
        vab_buf[1, 1, lo:lo + n, 0:LANES] = jnp.where(low, zero, v).astype(BF16)
        for c in range(N_KV):
            vab_buf[c, 0, lo:lo + n, LANES:2 * LANES] = ones_low
            vab_buf[c, 1, lo:lo + n, LANES:2 * LANES] = ones_high

    prev_halo = jnp.where(i > 0, pcp_ref[0], 0.0)
    next_halo = jnp.where(i < n_tiles - 1, pcn_ref[0], 0.0)
    zpad = jnp.zeros((POOL_MARGIN - POOL_HALO, POOL_WIDTH), F32)
    for r0, nr, u in ((0, POOL_MARGIN, jnp.concatenate([zpad, prev_halo], axis=0)),
                      (POOL_MARGIN, tm, pcc_ref[0]),
                      (POOL_MARGIN + tm, POOL_MARGIN, jnp.concatenate([next_halo, zpad], axis=0))):
        hi = u.astype(BF16)
        lo = (u - hi.astype(F32)).astype(BF16)
        for g in range(len(POOL_WINDOWS)):
            lanes = slice(g * GROUP_W, (g + 1) * GROUP_W)
            hl_buf[r0:r0 + nr, 2 * g * GROUP_W:(2 * g + 1) * GROUP_W] = hi[:, lanes]
            hl_buf[r0:r0 + nr, (2 * g + 1) * GROUP_W:(2 * g + 2) * GROUP_W] = lo[:, lanes]

    edge_buf[0, 0:POOL_HALO] = prev_halo
    edge_buf[0, POOL_HALO:] = pcc_ref[0, 0:2 * POOL_HALO]
    edge_buf[1, 0:2 * POOL_HALO] = pcc_ref[0, tm - 2 * POOL_HALO:tm]
    edge_buf[1, 2 * POOL_HALO:] = next_halo


def _mix_rows(sink_ref, q_ref, bias_ref, eye_ref, band_ref, poolw_ref, pscale_ref,
              kt_buf, vab_buf, hl_buf, edge_buf, d_buf, mix_buf, i, j0, j1, *, tm, seq):
    n_tiles = seq // tm
    nq = tm // QBLK
    nb = seq // QBLK

    for n in range(j0, j1):
        for g in range(len(POOL_WINDOWS)):
            win = hl_buf[n * QBLK:n * QBLK + 2 * QBLK, 2 * g * GROUP_W:(2 * g + 2) * GROUP_W]
            dd = jnp.dot(band_ref[g], win, preferred_element_type=F32)
            d_buf[n * QBLK:(n + 1) * QBLK, g * GROUP_W:(g + 1) * GROUP_W] = (
                dd[:, :GROUP_W] + dd[:, GROUP_W:])

    low_out = lax.broadcasted_iota(jnp.int32, (QBLK, LANES), 1) < HEAD_DIM
    eye = eye_ref[...]
    def scores(j, c):
        blk = i * nq + j
        variant = jnp.where(blk == 0, 0, jnp.where(blk == nb - 1, 2, 1))
        bias_t = bias_ref[variant]
        rows = slice(j * QBLK, (j + 1) * QBLK)
        keys = slice(j * QBLK, j * QBLK + KEYS)
        lhs = jnp.concatenate(
            [jnp.concatenate([q_ref[0, rows, (2 * c) * LANES:(2 * c + 1) * LANES], eye], axis=1),
             jnp.concatenate([q_ref[0, rows, (2 * c + 1) * LANES:(2 * c + 2) * LANES], eye],
                             axis=1)], axis=0)
        kc = kt_buf[c * HEAD_DIM:(c + 1) * HEAD_DIM, keys]
        z = jnp.zeros_like(kc)
        rhs = jnp.concatenate(
            [jnp.concatenate([kc, z], axis=1), jnp.concatenate([z, kc], axis=1), bias_t],
            axis=0)
        return jnp.dot(lhs, rhs, preferred_element_type=F32)

    def finish(j, c, s):
        rows = slice(j * QBLK, (j + 1) * QBLK)
        keys = slice(j * QBLK, j * QBLK + KEYS)
        p_rows, sink_terms = [], []
        for r in range(2):
            p_cols = []
            for par in range(2):
                sq = s[r * QBLK:(r + 1) * QBLK, par * KEYS:(par + 1) * KEYS]
                sink = sink_ref[4 * c + 2 * r + par] * LOG2E
                m = jnp.maximum(jnp.max(sq, axis=-1, keepdims=True), sink)
                p_cols.append(jnp.exp2(sq - m).astype(BF16))
                sink_terms.append(jnp.exp2(sink - m))
            p_rows.append(jnp.concatenate(p_cols, axis=1))
        pmat = jnp.concatenate(p_rows, axis=0)
        vrhs = jnp.concatenate([vab_buf[c, 0, keys], vab_buf[c, 1, keys]], axis=0)
        o = jnp.dot(pmat, vrhs, preferred_element_type=F32)
        for r in range(2):
            orow = o[r * QBLK:(r + 1) * QBLK]
            den = orow[:, LANES:] + jnp.where(low_out, sink_terms[2 * r], sink_terms[2 * r + 1])
            t = 2 * c + r
            mix_buf[rows, t * LANES:(t + 1) * LANES] = (orow[:, :LANES] / den).astype(BF16)

    units = [(j, c) for j in range(j0, j1) for c in range(N_KV)]
    s_next = scores(*units[0])
    for n, (j, c) in enumerate(units):
        s_cur = s_next
        if n + 1 < len(units):
            s_next = scores(*units[n + 1])
        finish(j, c, s_cur)

    for e, (row0, clipped) in enumerate(((0, i == 0), (tm - POOL_HALO, i == n_tiles - 1))):
        if not j0 * QBLK <= row0 < j1 * QBLK:
            continue
        tpos = i * tm + row0 + lax.broadcasted_iota(jnp.int32, (POOL_HALO, GROUP_W), 0)
        base = POOL_HALO
        for g, w in enumerate(POOL_WINDOWS):
            half = w // 2
            lanes = slice(g * GROUP_W, (g + 1) * GROUP_W)

            def wsum(lo, hi):
                acc = edge_buf[e, base + lo:base + lo + POOL_HALO, lanes]
                for k in range(lo + 1, hi + 1):
                    acc = acc + edge_buf[e, base + k:base + k + POOL_HALO, lanes]
                return acc

            def count(lo, hi):
                a = jnp.clip(tpos + lo, 0, seq)
                b = jnp.clip(tpos + hi + 1, 0, seq)
                return (b - a).astype(F32)

            mean = 0.5 * (wsum(-half, half - 1) / count(-half, half - 1)
                          + wsum(-half + 1, half) / count(-half + 1, half))
            fixed = mean - edge_buf[e, base:base + POOL_HALO, lanes]
            d_buf[row0:row0 + POOL_HALO, lanes] = jnp.where(
                clipped, fixed, d_buf[row0:row0 + POOL_HALO, lanes])

    rows = slice(j0 * QBLK, j1 * QBLK)
    for g in range(len(POOL_WINDOWS)):
        lanes = slice(g * GROUP_W, (g + 1) * GROUP_W)
        y = jnp.dot(d_buf[rows, lanes].astype(BF16), poolw_ref[g], preferred_element_type=F32)
        mix_buf[rows, Q_WIDTH + g * GROUP_W:Q_WIDTH + (g + 1) * GROUP_W] = (
            y * pscale_ref[:, lanes]).astype(BF16)


def _rope_tables(seq):
    pos = jnp.arange(seq, dtype=jnp.int32)
    inv_freq = ROPE_THETA ** (-jnp.arange(0, ROTARY_DIM, 2, dtype=F32) / ROTARY_DIM)
    ang = pos.astype(F32)[:, None] * inv_freq[None, :]
    cos, sin = jnp.cos(ang), jnp.sin(ang)
    half = ROTARY_DIM // 2
    rest = HEAD_DIM - ROTARY_DIM
    c = jnp.concatenate([cos, cos, jnp.ones((seq, rest), F32)], axis=1)
    sa = jnp.concatenate([-sin, jnp.zeros((seq, HEAD_DIM - half), F32)], axis=1)
    sb = jnp.concatenate([jnp.zeros((seq, half), F32), sin, jnp.zeros((seq, rest), F32)], axis=1)
    rep = LANES // HEAD_DIM
    return jnp.tile(c, (1, rep)), jnp.tile(sa, (1, rep)), jnp.tile(sb, (1, rep))


def _band_bias(seq):
    r = np.arange(QBLK)[:, None]
    s = np.arange(KEYS)[None, :]
    band = (s - r >= 0) & (s - r <= 2 * WINDOW)
    first = band & (s >= QBLK)
    last = band & (s < 2 * QBLK)
    out = np.stack([np.tile(np.where(m, 0.0, NEG), (1, 2)) for m in (first, band, last)])
    return jnp.asarray(out, dtype=F32).astype(BF16)


def _pool_band():
    out = np.zeros((len(POOL_WINDOWS), QBLK, 2 * QBLK), np.float32)
    r = np.arange(QBLK)
    for g, w in enumerate(POOL_WINDOWS):
        half = w // 2
        for k in range(-half, half + 1):
            out[g, r, r + POOL_MARGIN + k] = (0.5 if abs(k) == half else 1.0) / w
        out[g, r, r + POOL_MARGIN] -= 1.0
    return jnp.asarray(out).astype(BF16)


def _const_spec(shape):
    nd = len(shape)
    return pl.BlockSpec(shape, lambda *_: (0,) * nd, pipeline_mode=pl.Buffered(1))


def kernel(x, ffn1_norm, ffn1_w_gate, ffn1_w_up, ffn1_w_down, mix_norm, w_in, sink_logits,
           pool_w, pool_scale, w_out, ffn2_norm, ffn2_w_gate, ffn2_w_up, ffn2_w_down, final_norm):
    B, S, D = x.shape
    assert D == D_MODEL and S % TM == 0 and TM % QBLK == 0 and ffn1_norm.shape[0] == 1
    assert S % TM_IN == 0 and TM_IN % SUB_ROWS == 0
    tm = TM
    grid = (B, S // tm)
    in_width = w_in.shape[-1]
    params = pltpu.CompilerParams(dimension_semantics=("arbitrary", "arbitrary"),
                                  vmem_limit_bytes=VMEM_LIMIT)

    c_tab, sa_tab, sb_tab = _rope_tables(S)
    row = lambda g: g.reshape(1, -1).astype(F32)
    tile_spec = lambda width, rows=tm: pl.BlockSpec((1, rows, width), lambda b, i: (b, i, 0))
    tm_in = TM_IN
    tab_spec = pl.BlockSpec((tm_in, LANES), lambda b, i: (i, 0))
    hbm_spec = pl.BlockSpec(memory_space=pl.ANY)

    h1, q, kt, v, pc = pl.pallas_call(
        _ffn_in_kernel,
        grid=(B, S // tm_in),
        in_specs=[
            tile_spec(D, tm_in),
            _const_spec((1, D)),
            hbm_spec, hbm_spec, hbm_spec,
            _const_spec((1, D)),
            hbm_spec,
            tab_spec, tab_spec, tab_spec,
        ],
        out_specs=[
            tile_spec(D, tm_in),
            tile_spec(Q_WIDTH, tm_in),
            pl.BlockSpec((1, KV_WIDTH, tm_in), lambda b, i: (b, 0, i)),
            tile_spec(KV_WIDTH, tm_in),
            tile_spec(POOL_WIDTH, tm_in),
        ],
        out_shape=[
            jax.ShapeDtypeStruct((B, S, D), F32),
            jax.ShapeDtypeStruct((B, S, Q_WIDTH), BF16),
            jax.ShapeDtypeStruct((B, KV_WIDTH, S), BF16),
            jax.ShapeDtypeStruct((B, S, KV_WIDTH), BF16),
            jax.ShapeDtypeStruct((B, S, POOL_WIDTH), F32),
        ],
        scratch_shapes=[
            pltpu.VMEM((D, D_FF), BF16), pltpu.VMEM((D, D_FF), BF16), pltpu.VMEM((D_FF, D), BF16),
            pltpu.VMEM((D, in_width), BF16),
        ],
        compiler_params=params,
        name="ffn1_inproj",
    )(x, row(ffn1_norm[0]), ffn1_w_gate[0], ffn1_w_up[0], ffn1_w_down[0], row(mix_norm[0]),
      w_in[0], c_tab, sa_tab, sb_tab)

    qb = tm // QBLK
    pb = tm // POOL_HALO
    n_qb = S // QBLK
    n_pb = S // POOL_HALO
    mix_spec = pl.BlockSpec
    prev_q = lambda i: jnp.maximum(i * qb - 1, 0)
    next_q = lambda i: jnp.minimum((i + 1) * qb, n_qb - 1)
    prev_p = lambda i: jnp.maximum(i * pb - 1, 0)
    next_p = lambda i: jnp.minimum((i + 1) * pb, n_pb - 1)

    out = pl.pallas_call(
        functools.partial(_mix_ffn_kernel, tm=tm, seq=S),
        grid=grid,
        in_specs=[
            pl.BlockSpec(memory_space=pltpu.SMEM),
            tile_spec(D),
            mix_spec((1, tm, Q_WIDTH), lambda b, i: (b, i, 0)),
            mix_spec((1, KV_WIDTH, QBLK), lambda b, i: (b, 0, prev_q(i))),
            mix_spec((1, KV_WIDTH, tm), lambda b, i: (b, 0, i)),
            mix_spec((1, KV_WIDTH, QBLK), lambda b, i: (b, 0, next_q(i))),
            mix_spec((1, QBLK, KV_WIDTH), lambda b, i: (b, prev_q(i), 0)),
            mix_spec((1, tm, KV_WIDTH), lambda b, i: (b, i, 0)),
            mix_spec((1, QBLK, KV_WIDTH), lambda b, i: (b, next_q(i), 0)),
            mix_spec((1, POOL_HALO, POOL_WIDTH), lambda b, i: (b, prev_p(i), 0)),
            mix_spec((1, tm, POOL_WIDTH), lambda b, i: (b, i, 0)),
            mix_spec((1, POOL_HALO, POOL_WIDTH), lambda b, i: (b, next_p(i), 0)),
            _const_spec((3, QBLK, 2 * KEYS)),
            _const_spec((QBLK, QBLK)),
            _const_spec((len(POOL_WINDOWS), QBLK, 2 * QBLK)),
            _const_spec((len(POOL_WINDOWS), GROUP_W, GROUP_W)),
            _const_spec((1, POOL_WIDTH)),
            hbm_spec,
            _const_spec((1, D)),
            hbm_spec, hbm_spec, hbm_spec,
            _const_spec((1, D)),
        ],
        out_specs=tile_spec(D),
        out_shape=jax.ShapeDtypeStruct((B, S, D), x.dtype),
        scratch_shapes=[
            pltpu.VMEM((KV_WIDTH, tm + 2 * QBLK), BF16),
            pltpu.VMEM((N_KV, 2, tm + 2 * QBLK, 2 * LANES), BF16),
            pltpu.VMEM((tm + 2 * POOL_MARGIN, 2 * POOL_WIDTH), BF16),
            pltpu.VMEM((2, 3 * POOL_HALO, POOL_WIDTH), F32),
            pltpu.VMEM((tm, POOL_WIDTH), F32),
            pltpu.VMEM((tm, D), BF16),
            pltpu.VMEM((D, D), BF16),
            pltpu.VMEM((D, D_FF), BF16), pltpu.VMEM((D, D_FF), BF16), pltpu.VMEM((D_FF, D), BF16),
        ],
        compiler_params=params,
        name="mix_ffn2",
    )(sink_logits[0].astype(F32), h1, q, kt, kt, kt, v, v, v, pc, pc, pc,
      _band_bias(S), jnp.eye(QBLK, dtype=BF16), _pool_band(), pool_w[0].astype(BF16),
      row(pool_scale[0]), w_out[0],
      row(ffn2_norm[0]), ffn2_w_gate[0], ffn2_w_up[0], ffn2_w_down[0], row(final_norm))
    return out
```

```python
import functools

import numpy as np
import jax
import jax.numpy as jnp
from jax import lax
from jax.experimental import pallas as pl
from jax.experimental.pallas import tpu as pltpu

D_MODEL = 1024
HEAD_DIM = 64
N_HEADS = 8
N_KV = 2
Q_WIDTH = N_HEADS * HEAD_DIM
KV_WIDTH = N_KV * HEAD_DIM
WINDOW = 128
QBLK = 128
KEYS = 3 * QBLK
ROPE_THETA = 500000.0
ROTARY_DIM = HEAD_DIM // 4
POOL_WINDOWS = (2, 4, 8, 16)
POOL_HALO = 8
POOL_MARGIN = 64
POOL_WIDTH = 512
GROUP_W = 128
D_FF = 2816
EPS = 1e-6
LANES = 128
NEG = -1e30
LOG2E = 1.4426950408889634

TM = 512
TM_IN = 512
SUB_ROWS = 256
CAST_ROWS = 256
CAST_SLOTS = 3
VMEM_LIMIT = 58 * 1024 * 1024

F32 = jnp.float32
BF16 = jnp.bfloat16


def _rms(x, g):
    ms = jnp.mean(x * x, axis=-1, keepdims=True)
    return x * lax.rsqrt(ms + EPS) * g


def _swiglu(xn, wg_ref, wu_ref, wd_ref):
    gate = jnp.dot(xn, wg_ref[...], preferred_element_type=F32)
    up = jnp.dot(xn, wu_ref[...], preferred_element_type=F32)
    act = (gate * jax.nn.sigmoid(gate) * up).astype(BF16)
    return jnp.dot(act, wd_ref[...], preferred_element_type=F32)


def _load_weight_bf16(src, dst, stage, sem):
    n_rows, width = src.shape
    n_chunks = n_rows // CAST_ROWS
    ahead = CAST_SLOTS - 1

    def chunk_copy(k, slot):
        return pltpu.make_async_copy(src.at[pl.ds(k * CAST_ROWS, CAST_ROWS)],
                                     stage.at[slot, :, pl.ds(0, width)], sem.at[slot])

    for k in range(min(ahead, n_chunks)):
        chunk_copy(k, k).start()

    def body(k, carry):
        slot = lax.rem(k, CAST_SLOTS)

        @pl.when(k + ahead < n_chunks)
        def _():
            chunk_copy(k + ahead, lax.rem(k + ahead, CAST_SLOTS)).start()

        chunk_copy(k, slot).wait()
        r0 = pl.multiple_of(k * CAST_ROWS, CAST_ROWS)
        dst[pl.ds(r0, CAST_ROWS), :] = stage[slot, :, 0:width].astype(BF16)
        return carry

    lax.fori_loop(0, n_chunks, body, 0)


def _load_weights_bf16(pairs):
    def scoped(stage, sem):
        for src, dst in pairs:
            _load_weight_bf16(src, dst, stage, sem)

    pl.run_scoped(scoped, pltpu.VMEM((CAST_SLOTS, CAST_ROWS, D_FF), F32),
                  pltpu.SemaphoreType.DMA((CAST_SLOTS,)))


def _rope(t, c, sa, sb):
    t_plus = pltpu.roll(t, LANES - ROTARY_DIM // 2, 1)
    t_minus = pltpu.roll(t, ROTARY_DIM // 2, 1)
    return t * c + t_plus * sa + t_minus * sb


def _ffn_in_kernel(x_ref, g1_ref, wg_hbm, wu_hbm, wd_hbm, gm_ref, win_hbm,
                   c_ref, sa_ref, sb_ref,
                   h_ref, q_ref, kt_ref, v_ref, pc_ref,
                   wg_ref, wu_ref, wd_ref, win_ref):
    @pl.when((pl.program_id(0) == 0) & (pl.program_id(1) == 0))
    def _():
        _load_weights_bf16(((wg_hbm, wg_ref), (wu_hbm, wu_ref), (wd_hbm, wd_ref),
                            (win_hbm, win_ref)))

    tm = x_ref.shape[1]
    sub = SUB_ROWS
    scale = HEAD_DIM ** -0.5 * LOG2E
    def stage_norm(rows):
        return _rms(x_ref[0, rows], g1_ref[...]).astype(BF16)

    def stage_gate_up(xn):
        gate = jnp.dot(xn, wg_ref[...], preferred_element_type=F32)
        up = jnp.dot(xn, wu_ref[...], preferred_element_type=F32)
        return (gate * jax.nn.sigmoid(gate) * up).astype(BF16)

    def stage_down(rows, act):
        h = x_ref[0, rows] + 0.5 * jnp.dot(act, wd_ref[...], preferred_element_type=F32)
        h_ref[0, rows] = h
        return _rms(h, gm_ref[...]).astype(BF16)

    def stage_proj(hn):
        return jnp.dot(hn, win_ref[...], preferred_element_type=F32)

    def stage_out(rows, u):
        c, sa, sb = c_ref[rows], sa_ref[rows], sb_ref[rows]
        for t in range(Q_WIDTH // LANES):
            qt = _rope(u[:, t * LANES:(t + 1) * LANES], c, sa, sb)
            q_ref[0, rows, t * LANES:(t + 1) * LANES] = (qt * scale).astype(BF16)
        k = _rope(u[:, Q_WIDTH:Q_WIDTH + KV_WIDTH], c, sa, sb)
        kt_ref[0, :, rows] = k.T.astype(BF16)
        v_ref[0, rows] = u[:, Q_WIDTH + KV_WIDTH:Q_WIDTH + 2 * KV_WIDTH].astype(BF16)
        pc_ref[0, rows] = u[:, Q_WIDTH + 2 * KV_WIDTH:]

    for pair in range(tm // (2 * sub)):
        ra = slice((2 * pair) * sub, (2 * pair + 1) * sub)
        rb = slice((2 * pair + 1) * sub, (2 * pair + 2) * sub)
        xn_a, xn_b = stage_norm(ra), stage_norm(rb)
        act_a = stage_gate_up(xn_a)
        act_b = stage_gate_up(xn_b)
        hn_a = stage_down(ra, act_a)
        hn_b = stage_down(rb, act_b)
        u_a = stage_proj(hn_a)
        u_b = stage_proj(hn_b)
        stage_out(ra, u_a)
        stage_out(rb, u_b)


def _mix_ffn_kernel(sink_ref, h_ref, q_ref, ktp_ref, ktc_ref, ktn_ref,
                    vp_ref, vc_ref, vn_ref, pcp_ref, pcc_ref, pcn_ref,
                    bias_ref, eye_ref, band_ref, poolw_ref, pscale_ref, wout_hbm,
                    g2_ref, wg_hbm, wu_hbm, wd_hbm, gf_ref,
                    o_ref,
                    kt_buf, vab_buf, hl_buf, edge_buf, d_buf, mix_buf,
                    wout_ref, wg_ref, wu_ref, wd_ref, *, tm, seq):
    @pl.when((pl.program_id(0) == 0) & (pl.program_id(1) == 0))
    def _():
        _load_weights_bf16(((wout_hbm, wout_ref), (wg_hbm, wg_ref), (wu_hbm, wu_ref),
                            (wd_hbm, wd_ref)))

    i = pl.program_id(1)
    _prep_keys(ktp_ref, ktc_ref, ktn_ref, kt_buf, tm=tm)
    _prep_values(vp_ref, vc_ref, vn_ref, vab_buf, tm=tm)
    _prep_pool(pcp_ref, pcc_ref, pcn_ref, hl_buf, edge_buf, i, tm=tm, seq=seq)
    mix_rows = functools.partial(
        _mix_rows, sink_ref, q_ref, bias_ref, eye_ref, band_ref, poolw_ref, pscale_ref,
        kt_buf, vab_buf, hl_buf, edge_buf, d_buf, mix_buf, i, tm=tm, seq=seq)

    sub = tm // 2
    sub_blocks = sub // QBLK
    halves = (slice(0, sub), slice(sub, tm))

    def stage_out_proj(rows):
        h = h_ref[0, rows] + jnp.dot(mix_buf[rows], wout_ref[...], preferred_element_type=F32)
        return h, _rms(h, g2_ref[...]).astype(BF16)

    def stage_gate_up(hn):
        gate = jnp.dot(hn, wg_ref[...], preferred_element_type=F32)
        up = jnp.dot(hn, wu_ref[...], preferred_element_type=F32)
        return (gate * jax.nn.sigmoid(gate) * up).astype(BF16)

    def stage_down(rows, h, act):
        h = h + 0.5 * jnp.dot(act, wd_ref[...], preferred_element_type=F32)
        o_ref[0, rows] = _rms(h, gf_ref[...])

    mix_rows(0, 2 * sub_blocks)
    hs = [stage_out_proj(rows) for rows in halves]
    acts = [stage_gate_up(hn) for _, hn in hs]
    for rows, (h, _), act in zip(halves, hs, acts):
        stage_down(rows, h, act)


def _prep_keys(ktp_ref, ktc_ref, ktn_ref, kt_buf, *, tm):
    kt_buf[:, 0:QBLK] = ktp_ref[0]
    kt_buf[:, QBLK:QBLK + tm] = ktc_ref[0]
    kt_buf[:, QBLK + tm:] = ktn_ref[0]


def _prep_values(vp_ref, vc_ref, vn_ref, vab_buf, *, tm):
    for lo, ref, n in ((0, vp_ref, QBLK), (QBLK, vc_ref, tm), (QBLK + tm, vn_ref, QBLK)):
        v = ref[0].astype(F32)
        vr = pltpu.roll(v, HEAD_DIM, 1)
        low = lax.broadcasted_iota(jnp.int32, v.shape, 1) < HEAD_DIM
        zero = jnp.zeros_like(v)
        ones_low = jnp.where(low, 1.0, 0.0).astype(BF16)
        ones_high = jnp.where(low, 0.0, 1.0).astype(BF16)
        vab_buf[0, 0, lo:lo + n, 0:LANES] = jnp.where(low, v, zero).astype(BF16)
        vab_buf[0, 1, lo:lo + n, 0:LANES] = jnp.where(low, zero, vr).astype(BF16)
        vab_buf[1, 0, lo:lo + n, 0:LANES] = jnp.where(low, vr, zero).astype(BF16)
        vab_buf[1, 1, lo:lo + n, 0:LANES] = jnp.where(low, zero, v).astype(BF16)
        for c in range(N_KV):
            vab_buf[c, 0, lo:lo + n, LANES:2 * LANES] = ones_low
            vab_buf[c, 1, lo:lo + n, LANES:2 * LANES] = ones_high


def _prep_pool(pcp_ref, pcc_ref, pcn_ref, hl_buf, edge_buf, i, *, tm, seq):
    n_tiles = seq // tm
    prev_halo = jnp.where(i > 0, pcp_ref[0], 0.0)
    next_halo = jnp.where(i < n_tiles - 1, pcn_ref[0], 0.0)
    zpad = jnp.zeros((POOL_MARGIN - POOL_HALO, POOL_WIDTH), F32)
    for r0, nr, u in ((0, POOL_MARGIN, jnp.concatenate([zpad, prev_halo], axis=0)),
                      (POOL_MARGIN, tm, pcc_ref[0]),
                      (POOL_MARGIN + tm, POOL_MARGIN, jnp.concatenate([next_halo, zpad], axis=0))):
        hi = u.astype(BF16)
        lo = (u - hi.astype(F32)).astype(BF16)
        for g in range(len(POOL_WINDOWS)):
            lanes = slice(g * GROUP_W, (g + 1) * GROUP_W)
            hl_buf[r0:r0 + nr, 2 * g * GROUP_W:(2 * g + 1) * GROUP_W] = hi[:, lanes]
            hl_buf[r0:r0 + nr, (2 * g + 1) * GROUP_W:(2 * g + 2) * GROUP_W] = lo[:, lanes]

    edge_buf[0, 0:POOL_HALO] = prev_halo
    edge_buf[0, POOL_HALO:] = pcc_ref[0, 0:2 * POOL_HALO]
    edge_buf[1, 0:2 * POOL_HALO] = pcc_ref[0, tm - 2 * POOL_HALO:tm]
    edge_buf[1, 2 * POOL_HALO:] = next_halo


def _mix_rows(sink_ref, q_ref, bias_ref, eye_ref, band_ref, poolw_ref, pscale_ref,
              kt_buf, vab_buf, hl_buf, edge_buf, d_buf, mix_buf, i, j0, j1, *, tm, seq):
    n_tiles = seq // tm
    nq = tm // QBLK
    nb = seq // QBLK

    def band_block(n):
        for g in range(len(POOL_WINDOWS)):
            win = hl_buf[n * QBLK:n * QBLK + 2 * QBLK, 2 * g * GROUP_W:(2 * g + 2) * GROUP_W]
            dd = jnp.dot(band_ref[g], win, preferred_element_type=F32)
            d_buf[n * QBLK:(n + 1) * QBLK, g * GROUP_W:(g + 1) * GROUP_W] = (
                dd[:, :GROUP_W] + dd[:, GROUP_W:])

    low_out = lax.broadcasted_iota(jnp.int32, (QBLK, LANES), 1) < HEAD_DIM
    eye = eye_ref[...]

    def scores(j, c):
        blk = i * nq + j
        variant = jnp.where(blk == 0, 0, jnp.where(blk == nb - 1, 2, 1))
        bias_t = bias_ref[variant]
        rows = slice(j * QBLK, (j + 1) * QBLK)
        keys = slice(j * QBLK, j * QBLK + KEYS)
        lhs = jnp.concatenate(
            [jnp.concatenate([q_ref[0, rows, (2 * c) * LANES:(2 * c + 1) * LANES], eye], axis=1),
             jnp.concatenate([q_ref[0, rows, (2 * c + 1) * LANES:(2 * c + 2) * LANES], eye],
                             axis=1)], axis=0)
        kc = kt_buf[c * HEAD_DIM:(c + 1) * HEAD_DIM, keys]
        z = jnp.zeros_like(kc)
        rhs = jnp.concatenate(
            [jnp.concatenate([kc, z], axis=1), jnp.concatenate([z, kc], axis=1), bias_t],
            axis=0)
        return jnp.dot(lhs, rhs, preferred_element_type=F32)

    def softmax(c, s):
        p_rows, sink_terms = [], []
        for r in range(2):
            p_cols = []
            for par in range(2):
                sq = s[r * QBLK:(r + 1) * QBLK, par * KEYS:(par + 1) * KEYS]
                sink = sink_ref[4 * c + 2 * r + par] * LOG2E
                m = jnp.maximum(jnp.max(sq, axis=-1, keepdims=True), sink)
                p_cols.append(jnp.exp2(sq - m).astype(BF16))
                sink_terms.append(jnp.exp2(sink - m))
            p_rows.append(jnp.concatenate(p_cols, axis=1))
        return jnp.concatenate(p_rows, axis=0), sink_terms

    def weighted_values(j, c, pmat, sink_terms):
        rows = slice(j * QBLK, (j + 1) * QBLK)
        keys = slice(j * QBLK, j * QBLK + KEYS)
        vrhs = jnp.concatenate([vab_buf[c, 0, keys], vab_buf[c, 1, keys]], axis=0)
        o = jnp.dot(pmat, vrhs, preferred_element_type=F32)
        for r in range(2):
            orow = o[r * QBLK:(r + 1) * QBLK]
            den = orow[:, LANES:] + jnp.where(low_out, sink_terms[2 * r], sink_terms[2 * r + 1])
            t = 2 * c + r
            mix_buf[rows, t * LANES:(t + 1) * LANES] = (orow[:, :LANES] / den).astype(BF16)

    units = [(j, c) for j in range(j0, j1) for c in range(N_KV)]
    for n in range(j0, j1):
        band_block(n)
    s_next = scores(*units[0])
    for n, (j, c) in enumerate(units):
        s_cur = s_next
        if n + 1 < len(units):
            s_next = scores(*units[n + 1])
        weighted_values(j, c, *softmax(c, s_cur))

    for e, (row0, clipped) in enumerate(((0, i == 0), (tm - POOL_HALO, i == n_tiles - 1))):
        if not j0 * QBLK <= row0 < j1 * QBLK:
            continue
        tpos = i * tm + row0 + lax.broadcasted_iota(jnp.int32, (POOL_HALO, GROUP_W), 0)
        base = POOL_HALO
        for g, w in enumerate(POOL_WINDOWS):
            half = w // 2
            lanes = slice(g * GROUP_W, (g + 1) * GROUP_W)

            def wsum(lo, hi):
                acc = edge_buf[e, base + lo:base + lo + POOL_HALO, lanes]
                for k in range(lo + 1, hi + 1):
                    acc = acc + edge_buf[e, base + k:base + k + POOL_HALO, lanes]
                return acc

            def count(lo, hi):
                a = jnp.clip(tpos + lo, 0, seq)
                b = jnp.clip(tpos + hi + 1, 0, seq)
                return (b - a).astype(F32)

            mean = 0.5 * (wsum(-half, half - 1) / count(-half, half - 1)
                          + wsum(-half + 1, half) / count(-half + 1, half))
            fixed = mean - edge_buf[e, base:base + POOL_HALO, lanes]
            d_buf[row0:row0 + POOL_HALO, lanes] = jnp.where(
                clipped, fixed, d_buf[row0:row0 + POOL_HALO, lanes])

    rows = slice(j0 * QBLK, j1 * QBLK)
    for g in range(len(POOL_WINDOWS)):
        lanes = slice(g * GROUP_W, (g + 1) * GROUP_W)
        y = jnp.dot(d_buf[rows, lanes].astype(BF16), poolw_ref[g], preferred_element_type=F32)
        mix_buf[rows, Q_WIDTH + g * GROUP_W:Q_WIDTH + (g + 1) * GROUP_W] = (
            y * pscale_ref[:, lanes]).astype(BF16)


def _rope_tables(seq):
    f32 = np.float32
    inv_freq = f32(ROPE_THETA) ** (-np.arange(0, ROTARY_DIM, 2, dtype=f32) / f32(ROTARY_DIM))
    ang = np.arange(seq, dtype=f32)[:, None] * inv_freq[None, :]
    cos, sin = np.cos(ang.astype(np.float64)).astype(f32), np.sin(ang.astype(np.float64)).astype(f32)
    half = ROTARY_DIM // 2
    rest = HEAD_DIM - ROTARY_DIM
    c = np.concatenate([cos, cos, np.ones((seq, rest), f32)], axis=1)
    sa = np.concatenate([-sin, np.zeros((seq, HEAD_DIM - half), f32)], axis=1)
    sb = np.concatenate([np.zeros((seq, half), f32), sin, np.zeros((seq, rest), f32)], axis=1)
    rep = LANES // HEAD_DIM
    return tuple(jnp.asarray(np.tile(t, (1, rep))) for t in (c, sa, sb))


def _band_bias(seq):
    r = np.arange(QBLK)[:, None]
    s = np.arange(KEYS)[None, :]
    band = (s - r >= 0) & (s - r <= 2 * WINDOW)
    first = band & (s >= QBLK)
    last = band & (s < 2 * QBLK)
    out = np.stack([np.tile(np.where(m, 0.0, NEG), (1, 2)) for m in (first, band, last)])
    return jnp.asarray(out, dtype=F32).astype(BF16)


def _pool_band():
    out = np.zeros((len(POOL_WINDOWS), QBLK, 2 * QBLK), np.float32)
    r = np.arange(QBLK)
    for g, w in enumerate(POOL_WINDOWS):
        half = w // 2
        for k in range(-half, half + 1):
            out[g, r, r + POOL_MARGIN + k] = (0.5 if abs(k) == half else 1.0) / w
        out[g, r, r + POOL_MARGIN] -= 1.0
    return jnp.asarray(out).astype(BF16)


def _const_spec(shape):
    nd = len(shape)
    return pl.BlockSpec(shape, lambda *_: (0,) * nd, pipeline_mode=pl.Buffered(1))


def kernel(x, ffn1_norm, ffn1_w_gate, ffn1_w_up, ffn1_w_down, mix_norm, w_in, sink_logits,
           pool_w, pool_scale, w_out, ffn2_norm, ffn2_w_gate, ffn2_w_up, ffn2_w_down, final_norm):
    B, S, D = x.shape
    assert D == D_MODEL and S % TM == 0 and TM % QBLK == 0 and ffn1_norm.shape[0] == 1
    assert S % TM_IN == 0 and TM_IN % SUB_ROWS == 0
    tm = TM
    grid = (B, S // tm)
    in_width = w_in.shape[-1]
    params = pltpu.CompilerParams(dimension_semantics=("arbitrary", "arbitrary"),
                                  vmem_limit_bytes=VMEM_LIMIT)

    c_tab, sa_tab, sb_tab = _rope_tables(S)
    row = lambda g: g.reshape(1, -1).astype(F32)
    tile_spec = lambda width, rows=tm: pl.BlockSpec((1, rows, width), lambda b, i: (b, i, 0))
    tm_in = TM_IN
    tab_spec = pl.BlockSpec((tm_in, LANES), lambda b, i: (i, 0))
    hbm_spec = pl.BlockSpec(memory_space=pl.ANY)

    h1, q, kt, v, pc = pl.pallas_call(
        _ffn_in_kernel,
        grid=(B, S // tm_in),
        in_specs=[
            tile_spec(D, tm_in),
            _const_spec((1, D)),
            hbm_spec, hbm_spec, hbm_spec,
            _const_spec((1, D)),
            hbm_spec,
            tab_spec, tab_spec, tab_spec,
        ],
        out_specs=[
            tile_spec(D, tm_in),
            tile_spec(Q_WIDTH, tm_in),
            pl.BlockSpec((1, KV_WIDTH, tm_in), lambda b, i: (b, 0, i)),
            tile_spec(KV_WIDTH, tm_in),
            tile_spec(POOL_WIDTH, tm_in),
        ],
        out_shape=[
            jax.ShapeDtypeStruct((B, S, D), F32),
            jax.ShapeDtypeStruct((B, S, Q_WIDTH), BF16),
            jax.ShapeDtypeStruct((B, KV_WIDTH, S), BF16),
            jax.ShapeDtypeStruct((B, S, KV_WIDTH), BF16),
            jax.ShapeDtypeStruct((B, S, POOL_WIDTH), F32),
        ],
        scratch_shapes=[
            pltpu.VMEM((D, D_FF), BF16), pltpu.VMEM((D, D_FF), BF16), pltpu.VMEM((D_FF, D), BF16),
            pltpu.VMEM((D, in_width), BF16),
        ],
        compiler_params=params,
        name="ffn1_inproj",
    )(x, row(ffn1_norm[0]), ffn1_w_gate[0], ffn1_w_up[0], ffn1_w_down[0], row(mix_norm[0]),
      w_in[0], c_tab, sa_tab, sb_tab)

    qb = tm // QBLK
    pb = tm // POOL_HALO
    n_qb = S // QBLK
    n_pb = S // POOL_HALO
    mix_spec = pl.BlockSpec
    prev_q = lambda i: jnp.maximum(i * qb - 1, 0)
    next_q = lambda i: jnp.minimum((i + 1) * qb, n_qb - 1)
    prev_p = lambda i: jnp.maximum(i * pb - 1, 0)
    next_p = lambda i: jnp.minimum((i + 1) * pb, n_pb - 1)

    out = pl.pallas_call(
        functools.partial(_mix_ffn_kernel, tm=tm, seq=S),
        grid=grid,
        in_specs=[
            pl.BlockSpec(memory_space=pltpu.SMEM),
            tile_spec(D),
            mix_spec((1, tm, Q_WIDTH), lambda b, i: (b, i, 0)),
            mix_spec((1, KV_WIDTH, QBLK), lambda b, i: (b, 0, prev_q(i))),
            mix_spec((1, KV_WIDTH, tm), lambda b, i: (b, 0, i)),
            mix_spec((1, KV_WIDTH, QBLK), lambda b, i: (b, 0, next_q(i))),
            mix_spec((1, QBLK, KV_WIDTH), lambda b, i: (b, prev_q(i), 0)),
            mix_spec((1, tm, KV_WIDTH), lambda b, i: (b, i, 0)),
            mix_spec((1, QBLK, KV_WIDTH), lambda b, i: (b, next_q(i), 0)),
            mix_spec((1, POOL_HALO, POOL_WIDTH), lambda b, i: (b, prev_p(i), 0)),
            mix_spec((1, tm, POOL_WIDTH), lambda b, i: (b, i, 0)),
            mix_spec((1, POOL_HALO, POOL_WIDTH), lambda b, i: (b, next_p(i), 0)),
            _const_spec((3, QBLK, 2 * KEYS)),
            _const_spec((QBLK, QBLK)),
            _const_spec((len(POOL_WINDOWS), QBLK, 2 * QBLK)),
            _const_spec((len(POOL_WINDOWS), GROUP_W, GROUP_W)),
            _const_spec((1, POOL_WIDTH)),
            hbm_spec,
            _const_spec((1, D)),
            hbm_spec, hbm_spec, hbm_spec,
            _const_spec((1, D)),
        ],
        out_specs=tile_spec(D),
        out_shape=jax.ShapeDtypeStruct((B, S, D), x.dtype),
        scratch_shapes=[
            pltpu.VMEM((KV_WIDTH, tm + 2 * QBLK), BF16),
            pltpu.VMEM((N_KV, 2, tm + 2 * QBLK, 2 * LANES), BF16),
            pltpu.VMEM((tm + 2 * POOL_MARGIN, 2 * POOL_WIDTH), BF16),
            pltpu.VMEM((2, 3 * POOL_HALO, POOL_WIDTH), F32),
            pltpu.VMEM((tm, POOL_WIDTH), F32),
            pltpu.VMEM((tm, D), BF16),
            pltpu.VMEM((D, D), BF16),
            pltpu.VMEM((D, D_FF), BF16), pltpu.VMEM((D, D_FF), BF16), pltpu.VMEM((D_FF, D), BF16),
        ],
        compiler_params=params,
        name="mix_ffn2",
    )(sink_logits[0].astype(F32), h1, q, kt, kt, kt, v, v, v, pc, pc, pc,
      _band_bias(S), jnp.asarray(np.eye(QBLK, dtype=np.float32)).astype(BF16), _pool_band(),
      pool_w[0].astype(BF16),
      row(pool_scale[0]), w_out[0],
      row(ffn2_norm[0]), ffn2_w_gate[0], ffn2_w_up[0], ffn2_w_down[0], row(final_norm))
    return out
```

```python
import functools

import numpy as np
import jax
import jax.numpy as jnp
from jax import lax
from jax.experimental import pallas as pl
from jax.experimental.pallas import tpu as pltpu

D_MODEL = 1024
HEAD_DIM = 64
N_HEADS = 8
N_KV = 2
Q_WIDTH = N_HEADS * HEAD_DIM
KV_WIDTH = N_KV * HEAD_DIM
WINDOW = 128
QBLK = 128
KEYS = 3 * QBLK
ROPE_THETA = 500000.0
ROTARY_DIM = HEAD_DIM // 4
POOL_WINDOWS = (2, 4, 8, 16)
POOL_HALO = 8
POOL_MARGIN = 64
POOL_WIDTH = 512
GROUP_W = 128
D_FF = 2816
EPS = 1e-6
LANES = 128
NEG = -1e30
LOG2E = 1.4426950408889634

TM = 512
TM_IN = 512
SUB_ROWS = 256
CAST_ROWS = 256
CAST_SLOTS = 3
VMEM_LIMIT = 58 * 1024 * 1024

F32 = jnp.float32
BF16 = jnp.bfloat16


def _rms(x, g):
    ms = jnp.mean(x * x, axis=-1, keepdims=True)
    return x * lax.rsqrt(ms + EPS) * g


def _swiglu(xn, wg_ref, wu_ref, wd_ref):
    gate = jnp.dot(xn, wg_ref[...], preferred_element_type=F32)
    up = jnp.dot(xn, wu_ref[...], preferred_element_type=F32)
    act = (gate * jax.nn.sigmoid(gate) * up).astype(BF16)
    return jnp.dot(act, wd_ref[...], preferred_element_type=F32)


def _load_weight_bf16(src, dst, stage, sem):
    n_rows, width = src.shape
    n_chunks = n_rows // CAST_ROWS
    ahead = CAST_SLOTS - 1

    def chunk_copy(k, slot):
        return pltpu.make_async_copy(src.at[pl.ds(k * CAST_ROWS, CAST_ROWS)],
                                     stage.at[slot, :, pl.ds(0, width)], sem.at[slot])

    for k in range(min(ahead, n_chunks)):
        chunk_copy(k, k).start()

    def body(k, carry):
        slot = lax.rem(k, CAST_SLOTS)

        @pl.when(k + ahead < n_chunks)
        def _():
            chunk_copy(k + ahead, lax.rem(k + ahead, CAST_SLOTS)).start()

        chunk_copy(k, slot).wait()
        r0 = pl.multiple_of(k * CAST_ROWS, CAST_ROWS)
        dst[pl.ds(r0, CAST_ROWS), :] = stage[slot, :, 0:width].astype(BF16)
        return carry

    lax.fori_loop(0, n_chunks, body, 0)


def _load_weights_bf16(pairs):
    def scoped(stage, sem):
        for src, dst in pairs:
            _load_weight_bf16(src, dst, stage, sem)

    pl.run_scoped(scoped, pltpu.VMEM((CAST_SLOTS, CAST_ROWS, D_FF), F32),
                  pltpu.SemaphoreType.DMA((CAST_SLOTS,)))


def _rope(t, c, sa, sb):
    t_plus = pltpu.roll(t, LANES - ROTARY_DIM // 2, 1)
    t_minus = pltpu.roll(t, ROTARY_DIM // 2, 1)
    return t * c + t_plus * sa + t_minus * sb


def _ffn_in_kernel(x_ref, g1_ref, wg_hbm, wu_hbm, wd_hbm, gm_ref, win_hbm,
                   c_ref, sa_ref, sb_ref,
                   h_ref, q_ref, kt_ref, v_ref, pc_ref,
                   wg_ref, wu_ref, wd_ref, win_ref):
    @pl.when((pl.program_id(0) == 0) & (pl.program_id(1) == 0))
    def _():
        _load_weights_bf16(((wg_hbm, wg_ref), (wu_hbm, wu_ref), (wd_hbm, wd_ref),
                            (win_hbm, win_ref)))

    tm = x_ref.shape[1]
    sub = SUB_ROWS
    scale = HEAD_DIM ** -0.5 * LOG2E
    def stage_norm(rows):
        return _rms(x_ref[0, rows], g1_ref[...]).astype(BF16)

    def stage_gate_up(xn):
        gate = jnp.dot(xn, wg_ref[...], preferred_element_type=F32)
        up = jnp.dot(xn, wu_ref[...], preferred_element_type=F32)
        return (gate * jax.nn.sigmoid(gate) * up).astype(BF16)

    def stage_down(rows, act):
        h = x_ref[0, rows] + 0.5 * jnp.dot(act, wd_ref[...], preferred_element_type=F32)
        h_ref[0, rows] = h
        return _rms(h, gm_ref[...]).astype(BF16)

    def stage_proj(hn):
        return jnp.dot(hn, win_ref[...], preferred_element_type=F32)

    def stage_out(rows, u):
        c, sa, sb = c_ref[rows], sa_ref[rows], sb_ref[rows]
        for t in range(Q_WIDTH // LANES):
            qt = _rope(u[:, t * LANES:(t + 1) * LANES], c, sa, sb)
            q_ref[0, rows, t * LANES:(t + 1) * LANES] = (qt * scale).astype(BF16)
        k = _rope(u[:, Q_WIDTH:Q_WIDTH + KV_WIDTH], c, sa, sb)
        kt_ref[0, :, rows] = k.T.astype(BF16)
        v_ref[0, rows] = u[:, Q_WIDTH + KV_WIDTH:Q_WIDTH + 2 * KV_WIDTH].astype(BF16)
        pc_ref[0, rows] = u[:, Q_WIDTH + 2 * KV_WIDTH:]

    for pair in range(tm // (2 * sub)):
        ra = slice((2 * pair) * sub, (2 * pair + 1) * sub)
        rb = slice((2 * pair + 1) * sub, (2 * pair + 2) * sub)
        xn_a, xn_b = stage_norm(ra), stage_norm(rb)
        act_a = stage_gate_up(xn_a)
        act_b = stage_gate_up(xn_b)
        hn_a = stage_down(ra, act_a)
        hn_b = stage_down(rb, act_b)
        u_a = stage_proj(hn_a)
        u_b = stage_proj(hn_b)
        stage_out(ra, u_a)
        stage_out(rb, u_b)


def _mix_ffn_kernel(sink_ref, h_ref, q_ref, ktp_ref, ktc_ref, ktn_ref,
                    vp_ref, vc_ref, vn_ref, pcp_ref, pcc_ref, pcn_ref,
                    bias_ref, eye_ref, band_ref, poolw_ref, pscale_ref, wout_hbm,
                    g2_ref, wg_hbm, wu_hbm, wd_hbm, gf_ref,
                    o_ref,
                    kt_buf, vab_buf, hl_buf, edge_buf, d_buf, mix_buf,
                    wout_ref, wg_ref, wu_ref, wd_ref, *, tm, seq):
    @pl.when((pl.program_id(0) == 0) & (pl.program_id(1) == 0))
    def _():
        _load_weights_bf16(((wout_hbm, wout_ref), (wg_hbm, wg_ref), (wu_hbm, wu_ref),
                            (wd_hbm, wd_ref)))

    i = pl.program_id(1)
    _prep_keys(ktp_ref, ktc_ref, ktn_ref, kt_buf, tm=tm)
    _prep_values(vp_ref, vc_ref, vn_ref, vab_buf, tm=tm)
    _prep_pool(pcp_ref, pcc_ref, pcn_ref, hl_buf, edge_buf, i, tm=tm, seq=seq)
    mix_rows = functools.partial(
        _mix_rows, sink_ref, q_ref, bias_ref, eye_ref, band_ref, poolw_ref, pscale_ref,
        kt_buf, vab_buf, hl_buf, edge_buf, d_buf, mix_buf, i, tm=tm, seq=seq)

    sub = tm // 2
    sub_blocks = sub // QBLK
    halves = (slice(0, sub), slice(sub, tm))

    def stage_out_proj(rows):
        h = h_ref[0, rows] + jnp.dot(mix_buf[rows], wout_ref[...], preferred_element_type=F32)
        return h, _rms(h, g2_ref[...]).astype(BF16)

    def stage_gate_up(hn):
        gate = jnp.dot(hn, wg_ref[...], preferred_element_type=F32)
        up = jnp.dot(hn, wu_ref[...], preferred_element_type=F32)
        return (gate * jax.nn.sigmoid(gate) * up).astype(BF16)

    def stage_down(rows, h, act):
        h = h + 0.5 * jnp.dot(act, wd_ref[...], preferred_element_type=F32)
        o_ref[0, rows] = _rms(h, gf_ref[...])

    mix_rows(0, 2 * sub_blocks)
    hs = [stage_out_proj(rows) for rows in halves]
    acts = [stage_gate_up(hn) for _, hn in hs]
    for rows, (h, _), act in zip(halves, hs, acts):
        stage_down(rows, h, act)


def _prep_keys(ktp_ref, ktc_ref, ktn_ref, kt_buf, *, tm):
    kt_buf[:, 0:QBLK] = ktp_ref[0]
    kt_buf[:, QBLK:QBLK + tm] = ktc_ref[0]
    kt_buf[:, QBLK + tm:] = ktn_ref[0]


def _prep_values(vp_ref, vc_ref, vn_ref, vab_buf, *, tm):
    for lo, ref, n in ((0, vp_ref, QBLK), (QBLK, vc_ref, tm), (QBLK + tm, vn_ref, QBLK)):
        v = ref[0].astype(F32)
        vr = pltpu.roll(v, HEAD_DIM, 1)
        low = lax.broadcasted_iota(jnp.int32, v.shape, 1) < HEAD_DIM
        zero = jnp.zeros_like(v)
        ones_low = jnp.where(low, 1.0, 0.0).astype(BF16)
        ones_high = jnp.where(low, 0.0, 1.0).astype(BF16)
        vab_buf[0, 0, lo:lo + n, 0:LANES] = jnp.where(low, v, zero).astype(BF16)
        vab_buf[0, 1, lo:lo + n, 0:LANES] = jnp.where(low, zero, vr).astype(BF16)
        vab_buf[1, 0, lo:lo + n, 0:LANES] = jnp.where(low, vr, zero).astype(BF16)
        vab_buf[1, 1, lo:lo + n, 0:LANES] = jnp.where(low, zero, v).astype(BF16)
        for c in range(N_KV):
            vab_buf[c, 0, lo:lo + n, LANES:2 * LANES] = ones_low
            vab_buf[c, 1, lo:lo + n, LANES:2 * LANES] = ones_high


def _prep_pool(pcp_ref, pcc_ref, pcn_ref, hl_buf, edge_buf, i, *, tm, seq):
    n_tiles = seq // tm
    prev_halo = jnp.where(i > 0, pcp_ref[0], 0.0)
    next_halo = jnp.where(i < n_tiles - 1, pcn_ref[0], 0.0)
    zpad = jnp.zeros((POOL_MARGIN - POOL_HALO, POOL_WIDTH), F32)
    for r0, nr, u in ((0, POOL_MARGIN, jnp.concatenate([zpad, prev_halo], axis=0)),
                      (POOL_MARGIN, tm, pcc_ref[0]),
                      (POOL_MARGIN + tm, POOL_MARGIN, jnp.concatenate([next_halo, zpad], axis=0))):
        hi = u.astype(BF16)
        lo = (u - hi.astype(F32)).astype(BF16)
        for g in range(len(POOL_WINDOWS)):
            lanes = slice(g * GROUP_W, (g + 1) * GROUP_W)
            hl_buf[r0:r0 + nr, 2 * g * GROUP_W:(2 * g + 1) * GROUP_W] = hi[:, lanes]
            hl_buf[r0:r0 + nr, (2 * g + 1) * GROUP_W:(2 * g + 2) * GROUP_W] = lo[:, lanes]

    edge_buf[0, 0:POOL_HALO] = prev_halo
    edge_buf[0, POOL_HALO:] = pcc_ref[0, 0:2 * POOL_HALO]
    edge_buf[1, 0:2 * POOL_HALO] = pcc_ref[0, tm - 2 * POOL_HALO:tm]
    edge_buf[1, 2 * POOL_HALO:] = next_halo


def _mix_rows(sink_ref, q_ref, bias_ref, eye_ref, band_ref, poolw_ref, pscale_ref,
              kt_buf, vab_buf, hl_buf, edge_buf, d_buf, mix_buf, i, j0, j1, *, tm, seq):
    n_tiles = seq // tm
    nq = tm // QBLK
    nb = seq // QBLK

    def band_block(n):
        for g in range(len(POOL_WINDOWS)):
            win = hl_buf[n * QBLK:n * QBLK + 2 * QBLK, 2 * g * GROUP_W:(2 * g + 2) * GROUP_W]
            dd = jnp.dot(band_ref[g], win, preferred_element_type=F32)
            d_buf[n * QBLK:(n + 1) * QBLK, g * GROUP_W:(g + 1) * GROUP_W] = (
                dd[:, :GROUP_W] + dd[:, GROUP_W:])

    low_out = lax.broadcasted_iota(jnp.int32, (QBLK, LANES), 1) < HEAD_DIM
    eye = eye_ref[...]

    def scores(j, c):
        blk = i * nq + j
        variant = jnp.where(blk == 0, 0, jnp.where(blk == nb - 1, 2, 1))
        bias_t = bias_ref[variant]
        rows = slice(j * QBLK, (j + 1) * QBLK)
        keys = slice(j * QBLK, j * QBLK + KEYS)
        lhs = jnp.concatenate(
            [jnp.concatenate([q_ref[0, rows, (2 * c) * LANES:(2 * c + 1) * LANES], eye], axis=1),
             jnp.concatenate([q_ref[0, rows, (2 * c + 1) * LANES:(2 * c + 2) * LANES], eye],
                             axis=1)], axis=0)
        kc = kt_buf[c * HEAD_DIM:(c + 1) * HEAD_DIM, keys]
        z = jnp.zeros_like(kc)
        rhs = jnp.concatenate(
            [jnp.concatenate([kc, z], axis=1), jnp.concatenate([z, kc], axis=1), bias_t],
            axis=0)
        return jnp.dot(lhs, rhs, preferred_element_type=F32)

    def softmax(c, s):
        p_rows, sink_terms = [], []
        for r in range(2):
            p_cols = []
            for par in range(2):
                sq = s[r * QBLK:(r + 1) * QBLK, par * KEYS:(par + 1) * KEYS]
                sink = sink_ref[4 * c + 2 * r + par] * LOG2E
                m = jnp.maximum(jnp.max(sq, axis=-1, keepdims=True), sink)
                p = jnp.exp2(sq - m)
                p_cols.append(p.astype(BF16))
                sink_terms.append(1.0 / (jnp.sum(p, axis=-1, keepdims=True) + jnp.exp2(sink - m)))
            p_rows.append(jnp.concatenate(p_cols, axis=1))
        return jnp.concatenate(p_rows, axis=0), sink_terms

    def weighted_values(j, c, pmat, sink_terms):
        rows = slice(j * QBLK, (j + 1) * QBLK)
        keys = slice(j * QBLK, j * QBLK + KEYS)
        vrhs = jnp.concatenate([vab_buf[c, 0, keys, 0:LANES], vab_buf[c, 1, keys, 0:LANES]], axis=0)
        o = jnp.dot(pmat, vrhs, preferred_element_type=F32)
        for r in range(2):
            orow = o[r * QBLK:(r + 1) * QBLK]
            inv = jnp.where(low_out, sink_terms[2 * r], sink_terms[2 * r + 1])
            t = 2 * c + r
            mix_buf[rows, t * LANES:(t + 1) * LANES] = (orow * inv).astype(BF16)

    units = [(j, c) for j in range(j0, j1) for c in range(N_KV)]
    for n in range(j0, j1):
        band_block(n)
    s_next = scores(*units[0])
    for n, (j, c) in enumerate(units):
        s_cur = s_next
        if n + 1 < len(units):
            s_next = scores(*units[n + 1])
        weighted_values(j, c, *softmax(c, s_cur))

    for e, (row0, clipped) in enumerate(((0, i == 0), (tm - POOL_HALO, i == n_tiles - 1))):
        if not j0 * QBLK <= row0 < j1 * QBLK:
            continue
        tpos = i * tm + row0 + lax.broadcasted_iota(jnp.int32, (POOL_HALO, GROUP_W), 0)
        base = POOL_HALO
        for g, w in enumerate(POOL_WINDOWS):
            half = w // 2
            lanes = slice(g * GROUP_W, (g + 1) * GROUP_W)

            def wsum(lo, hi):
                acc = edge_buf[e, base + lo:base + lo + POOL_HALO, lanes]
                for k in range(lo + 1, hi + 1):
                    acc = acc + edge_buf[e, base + k:base + k + POOL_HALO, lanes]
                return acc

            def count(lo, hi):
                a = jnp.clip(tpos + lo, 0, seq)
                b = jnp.clip(tpos + hi + 1, 0, seq)
                return (b - a).astype(F32)

            mean = 0.5 * (wsum(-half, half - 1) / count(-half, half - 1)
                          + wsum(-half + 1, half) / count(-half + 1, half))
            fixed = mean - edge_buf[e, base:base + POOL_HALO, lanes]
            d_buf[row0:row0 + POOL_HALO, lanes] = jnp.where(
                clipped, fixed, d_buf[row0:row0 + POOL_HALO, lanes])

    rows = slice(j0 * QBLK, j1 * QBLK)
    for g in range(len(POOL_WINDOWS)):
        lanes = slice(g * GROUP_W, (g + 1) * GROUP_W)
        y = jnp.dot(d_buf[rows, lanes].astype(BF16), poolw_ref[g], preferred_element_type=F32)
        mix_buf[rows, Q_WIDTH + g * GROUP_W:Q_WIDTH + (g + 1) * GROUP_W] = (
            y * pscale_ref[:, lanes]).astype(BF16)


def _rope_tables(seq):
    f32 = np.float32
    inv_freq = f32(ROPE_THETA) ** (-np.arange(0, ROTARY_DIM, 2, dtype=f32) / f32(ROTARY_DIM))
    ang = np.arange(seq, dtype=f32)[:, None] * inv_freq[None, :]
    cos, sin = np.cos(ang.astype(np.float64)).astype(f32), np.sin(ang.astype(np.float64)).astype(f32)
    half = ROTARY_DIM // 2
    rest = HEAD_DIM - ROTARY_DIM
    c = np.concatenate([cos, cos, np.ones((seq, rest), f32)], axis=1)
    sa = np.concatenate([-sin, np.zeros((seq, HEAD_DIM - half), f32)], axis=1)
    sb = np.concatenate([np.zeros((seq, half), f32), sin, np.zeros((seq, rest), f32)], axis=1)
    rep = LANES // HEAD_DIM
    return tuple(jnp.asarray(np.tile(t, (1, rep))) for t in (c, sa, sb))


def _band_bias(seq):
    r = np.arange(QBLK)[:, None]
    s = np.arange(KEYS)[None, :]
    band = (s - r >= 0) & (s - r <= 2 * WINDOW)
    first = band & (s >= QBLK)
    last = band & (s < 2 * QBLK)
    out = np.stack([np.tile(np.where(m, 0.0, NEG), (1, 2)) for m in (first, band, last)])
    return jnp.asarray(out, dtype=F32).astype(BF16)


def _pool_band():
    out = np.zeros((len(POOL_WINDOWS), QBLK, 2 * QBLK), np.float32)
    r = np.arange(QBLK)
    for g, w in enumerate(POOL_WINDOWS):
        half = w // 2
        for k in range(-half, half + 1):
            out[g, r, r + POOL_MARGIN + k] = (0.5 if abs(k) == half else 1.0) / w
        out[g, r, r + POOL_MARGIN] -= 1.0
    return jnp.asarray(out).astype(BF16)


def _const_spec(shape):
    nd = len(shape)
    return pl.BlockSpec(shape, lambda *_: (0,) * nd, pipeline_mode=pl.Buffered(1))


def kernel(x, ffn1_norm, ffn1_w_gate, ffn1_w_up, ffn1_w_down, mix_norm, w_in, sink_logits,
           pool_w, pool_scale, w_out, ffn2_norm, ffn2_w_gate, ffn2_w_up, ffn2_w_down, final_norm):
    B, S, D = x.shape
    assert D == D_MODEL and S % TM == 0 and TM % QBLK == 0 and ffn1_norm.shape[0] == 1
    assert S % TM_IN == 0 and TM_IN % SUB_ROWS == 0
    tm = TM
    grid = (B, S // tm)
    in_width = w_in.shape[-1]
    params = pltpu.CompilerParams(dimension_semantics=("arbitrary", "arbitrary"),
                                  vmem_limit_bytes=VMEM_LIMIT)

    c_tab, sa_tab, sb_tab = _rope_tables(S)
    row = lambda g: g.reshape(1, -1).astype(F32)
    tile_spec = lambda width, rows=tm: pl.BlockSpec((1, rows, width), lambda b, i: (b, i, 0))
    tm_in = TM_IN
    tab_spec = pl.BlockSpec((tm_in, LANES), lambda b, i: (i, 0))
    hbm_spec = pl.BlockSpec(memory_space=pl.ANY)

    h1, q, kt, v, pc = pl.pallas_call(
        _ffn_in_kernel,
        grid=(B, S // tm_in),
        in_specs=[
            tile_spec(D, tm_in),
            _const_spec((1, D)),
            hbm_spec, hbm_spec, hbm_spec,
            _const_spec((1, D)),
            hbm_spec,
            tab_spec, tab_spec, tab_spec,
        ],
        out_specs=[
            tile_spec(D, tm_in),
            tile_spec(Q_WIDTH, tm_in),
            pl.BlockSpec((1, KV_WIDTH, tm_in), lambda b, i: (b, 0, i)),
            tile_spec(KV_WIDTH, tm_in),
            tile_spec(POOL_WIDTH, tm_in),
        ],
        out_shape=[
            jax.ShapeDtypeStruct((B, S, D), F32),
            jax.ShapeDtypeStruct((B, S, Q_WIDTH), BF16),
            jax.ShapeDtypeStruct((B, KV_WIDTH, S), BF16),
            jax.ShapeDtypeStruct((B, S, KV_WIDTH), BF16),
            jax.ShapeDtypeStruct((B, S, POOL_WIDTH), F32),
        ],
        scratch_shapes=[
            pltpu.VMEM((D, D_FF), BF16), pltpu.VMEM((D, D_FF), BF16), pltpu.VMEM((D_FF, D), BF16),
            pltpu.VMEM((D, in_width), BF16),
        ],
        compiler_params=params,
        name="ffn1_inproj",
    )(x, row(ffn1_norm[0]), ffn1_w_gate[0], ffn1_w_up[0], ffn1_w_down[0], row(mix_norm[0]),
      w_in[0], c_tab, sa_tab, sb_tab)

    qb = tm // QBLK
    pb = tm // POOL_HALO
    n_qb = S // QBLK
    n_pb = S // POOL_HALO
    mix_spec = pl.BlockSpec
    prev_q = lambda i: jnp.maximum(i * qb - 1, 0)
    next_q = lambda i: jnp.minimum((i + 1) * qb, n_qb - 1)
    prev_p = lambda i: jnp.maximum(i * pb - 1, 0)
    next_p = lambda i: jnp.minimum((i + 1) * pb, n_pb - 1)

    out = pl.pallas_call(
        functools.partial(_mix_ffn_kernel, tm=tm, seq=S),
        grid=grid,
        in_specs=[
            pl.BlockSpec(memory_space=pltpu.SMEM),
            tile_spec(D),
            mix_spec((1, tm, Q_WIDTH), lambda b, i: (b, i, 0)),
            mix_spec((1, KV_WIDTH, QBLK), lambda b, i: (b, 0, prev_q(i))),
            mix_spec((1, KV_WIDTH, tm), lambda b, i: (b, 0, i)),
            mix_spec((1, KV_WIDTH, QBLK), lambda b, i: (b, 0, next_q(i))),
            mix_spec((1, QBLK, KV_WIDTH), lambda b, i: (b, prev_q(i), 0)),
            mix_spec((1, tm, KV_WIDTH), lambda b, i: (b, i, 0)),
            mix_spec((1, QBLK, KV_WIDTH), lambda b, i: (b, next_q(i), 0)),
            mix_spec((1, POOL_HALO, POOL_WIDTH), lambda b, i: (b, prev_p(i), 0)),
            mix_spec((1, tm, POOL_WIDTH), lambda b, i: (b, i, 0)),
            mix_spec((1, POOL_HALO, POOL_WIDTH), lambda b, i: (b, next_p(i), 0)),
            _const_spec((3, QBLK, 2 * KEYS)),
            _const_spec((QBLK, QBLK)),
            _const_spec((len(POOL_WINDOWS), QBLK, 2 * QBLK)),
            _const_spec((len(POOL_WINDOWS), GROUP_W, GROUP_W)),
            _const_spec((1, POOL_WIDTH)),
            hbm_spec,
            _const_spec((1, D)),
            hbm_spec, hbm_spec, hbm_spec,
            _const_spec((1, D)),
        ],
        out_specs=tile_spec(D),
        out_shape=jax.ShapeDtypeStruct((B, S, D), x.dtype),
        scratch_shapes=[
            pltpu.VMEM((KV_WIDTH, tm + 2 * QBLK), BF16),
            pltpu.VMEM((N_KV, 2, tm + 2 * QBLK, 2 * LANES), BF16),
            pltpu.VMEM((tm + 2 * POOL_MARGIN, 2 * POOL_WIDTH), BF16),
            pltpu.VMEM((2, 3 * POOL_HALO, POOL_WIDTH), F32),
            pltpu.VMEM((tm, POOL_WIDTH), F32),
            pltpu.VMEM((tm, D), BF16),
            pltpu.VMEM((D, D), BF16),
            pltpu.VMEM((D, D_FF), BF16), pltpu.VMEM((D, D_FF), BF16), pltpu.VMEM((D_FF, D), BF16),
        ],
        compiler_params=params,
        name="mix_ffn2",
    )(sink_logits[0].astype(F32), h1, q, kt, kt, kt, v, v, v, pc, pc, pc,
      _band_bias(S), jnp.asarray(np.eye(QBLK, dtype=np.float32)).astype(BF16), _pool_band(),
      pool_w[0].astype(BF16),
      row(pool_scale[0]), w_out[0],
      row(ffn2_norm[0]), ffn2_w_gate[0], ffn2_w_up[0], ffn2_w_down[0], row(final_norm))
    return out
```

```python
import functools

import numpy as np
import jax
import jax.numpy as jnp
from jax import lax
from jax.experimental import pallas as pl
from jax.experimental.pallas import tpu as pltpu

D_MODEL = 1024
HEAD_DIM = 64
N_HEADS = 8
N_KV = 2
Q_WIDTH = N_HEADS * HEAD_DIM
KV_WIDTH = N_KV * HEAD_DIM
WINDOW = 128
QBLK = 128
KEYS = 3 * QBLK
ROPE_THETA = 500000.0
ROTARY_DIM = HEAD_DIM // 4
POOL_WINDOWS = (2, 4, 8, 16)
POOL_HALO = 8
POOL_MARGIN = 64
POOL_WIDTH = 512
GROUP_W = 128
D_FF = 2816
EPS = 1e-6
LANES = 128
NEG = -1e30
LOG2E = 1.4426950408889634

TM = 512
TM_IN = 512
SUB_ROWS = 256
CAST_ROWS = 256
CAST_SLOTS = 3
VMEM_LIMIT = 58 * 1024 * 1024

F32 = jnp.float32
BF16 = jnp.bfloat16


def _rms(x, g):
    ms = jnp.mean(x * x, axis=-1, keepdims=True)
    return x * lax.rsqrt(ms + EPS) * g


def _swiglu(xn, wg_ref, wu_ref, wd_ref):
    gate = jnp.dot(xn, wg_ref[...], preferred_element_type=F32)
    up = jnp.dot(xn, wu_ref[...], preferred_element_type=F32)
    act = (gate * jax.nn.sigmoid(gate) * up).astype(BF16)
    return jnp.dot(act, wd_ref[...], preferred_element_type=F32)


def _load_weight_bf16(src, dst, stage, sem):
    n_rows, width = src.shape
    n_chunks = n_rows // CAST_ROWS
    ahead = CAST_SLOTS - 1

    def chunk_copy(k, slot):
        return pltpu.make_async_copy(src.at[pl.ds(k * CAST_ROWS, CAST_ROWS)],
                                     stage.at[slot, :, pl.ds(0, width)], sem.at[slot])

    for k in range(min(ahead, n_chunks)):
        chunk_copy(k, k).start()

    def body(k, carry):
        slot = lax.rem(k, CAST_SLOTS)

        @pl.when(k + ahead < n_chunks)
        def _():
            chunk_copy(k + ahead, lax.rem(k + ahead, CAST_SLOTS)).start()

        chunk_copy(k, slot).wait()
        r0 = pl.multiple_of(k * CAST_ROWS, CAST_ROWS)
        dst[pl.ds(r0, CAST_ROWS), :] = stage[slot, :, 0:width].astype(BF16)
        return carry

    lax.fori_loop(0, n_chunks, body, 0)


def _load_weights_bf16(pairs):
    def scoped(stage, sem):
        for src, dst in pairs:
            _load_weight_bf16(src, dst, stage, sem)

    pl.run_scoped(scoped, pltpu.VMEM((CAST_SLOTS, CAST_ROWS, D_FF), F32),
                  pltpu.SemaphoreType.DMA((CAST_SLOTS,)))


def _rope(t, c, sa, sb):
    t_plus = pltpu.roll(t, LANES - ROTARY_DIM // 2, 1)
    t_minus = pltpu.roll(t, ROTARY_DIM // 2, 1)
    return t * c + t_plus * sa + t_minus * sb


def _ffn_in_kernel(x_ref, g1_ref, wg_hbm, wu_hbm, wd_hbm, gm_ref, win_hbm,
                   c_ref, sa_ref, sb_ref,
                   h_ref, q_ref, kt_ref, v_ref, pc_ref,
                   wg_ref, wu_ref, wd_ref, win_ref):
    @pl.when((pl.program_id(0) == 0) & (pl.program_id(1) == 0))
    def _():
        _load_weights_bf16(((wg_hbm, wg_ref), (wu_hbm, wu_ref), (wd_hbm, wd_ref),
                            (win_hbm, win_ref)))

    tm = x_ref.shape[1]
    sub = SUB_ROWS
    scale = HEAD_DIM ** -0.5 * LOG2E
    def stage_norm(rows):
        return _rms(x_ref[0, rows], g1_ref[...]).astype(BF16)

    def stage_gate_up(xn):
        gate = jnp.dot(xn, wg_ref[...], preferred_element_type=F32)
        up = jnp.dot(xn, wu_ref[...], preferred_element_type=F32)
        return (gate * jax.nn.sigmoid(gate) * up).astype(BF16)

    def stage_down(rows, act):
        h = x_ref[0, rows] + 0.5 * jnp.dot(act, wd_ref[...], preferred_element_type=F32)
        h_ref[0, rows] = h
        return _rms(h, gm_ref[...]).astype(BF16)

    def stage_proj(hn):
        return jnp.dot(hn, win_ref[...], preferred_element_type=F32)

    def stage_out(rows, u):
        c, sa, sb = c_ref[rows], sa_ref[rows], sb_ref[rows]
        for t in range(Q_WIDTH // LANES):
            qt = _rope(u[:, t * LANES:(t + 1) * LANES], c, sa, sb)
            q_ref[0, rows, t * LANES:(t + 1) * LANES] = (qt * scale).astype(BF16)
        k = _rope(u[:, Q_WIDTH:Q_WIDTH + KV_WIDTH], c, sa, sb)
        kt_ref[0, :, rows] = k.T.astype(BF16)
        v_ref[0, rows] = u[:, Q_WIDTH + KV_WIDTH:Q_WIDTH + 2 * KV_WIDTH].astype(BF16)
        pc_ref[0, rows] = u[:, Q_WIDTH + 2 * KV_WIDTH:]

    for pair in range(tm // (2 * sub)):
        ra = slice((2 * pair) * sub, (2 * pair + 1) * sub)
        rb = slice((2 * pair + 1) * sub, (2 * pair + 2) * sub)
        xn_a, xn_b = stage_norm(ra), stage_norm(rb)
        act_a = stage_gate_up(xn_a)
        act_b = stage_gate_up(xn_b)
        hn_a = stage_down(ra, act_a)
        hn_b = stage_down(rb, act_b)
        u_a = stage_proj(hn_a)
        u_b = stage_proj(hn_b)
        stage_out(ra, u_a)
        stage_out(rb, u_b)


def _mix_ffn_kernel(sink_ref, h_ref, q_ref, ktp_ref, ktc_ref, ktn_ref,
                    vp_ref, vc_ref, vn_ref, pcp_ref, pcc_ref, pcn_ref,
                    bias_ref, eye_ref, band_ref, poolw_ref, pscale_ref, wout_hbm,
                    g2_ref, wg_hbm, wu_hbm, wd_hbm, gf_ref,
                    o_ref,
                    kt_buf, vab_buf, hl_buf, edge_buf, d_buf, mix_buf,
                    wout_ref, wg_ref, wu_ref, wd_ref, *, tm, seq):
    @pl.when((pl.program_id(0) == 0) & (pl.program_id(1) == 0))
    def _():
        _load_weights_bf16(((wout_hbm, wout_ref), (wg_hbm, wg_ref), (wu_hbm, wu_ref),
                            (wd_hbm, wd_ref)))

    i = pl.program_id(1)
    _prep_keys(ktp_ref, ktc_ref, ktn_ref, kt_buf, tm=tm)
    _prep_values(vp_ref, vc_ref, vn_ref, vab_buf, tm=tm)
    _prep_pool(pcp_ref, pcc_ref, pcn_ref, hl_buf, edge_buf, i, tm=tm, seq=seq)
    mix_rows = functools.partial(
        _mix_rows, sink_ref, q_ref, bias_ref, eye_ref, band_ref, poolw_ref, pscale_ref,
        kt_buf, vab_buf, hl_buf, edge_buf, d_buf, mix_buf, i, tm=tm, seq=seq)

    sub = tm // 2
    sub_blocks = sub // QBLK
    halves = (slice(0, sub), slice(sub, tm))

    def stage_out_proj(rows):
        h = h_ref[0, rows] + jnp.dot(mix_buf[rows], wout_ref[...], preferred_element_type=F32)
        return h, _rms(h, g2_ref[...]).astype(BF16)

    def stage_gate_up(hn):
        gate = jnp.dot(hn, wg_ref[...], preferred_element_type=F32)
        up = jnp.dot(hn, wu_ref[...], preferred_element_type=F32)
        return (gate * jax.nn.sigmoid(gate) * up).astype(BF16)

    def stage_down(rows, h, act):
        h = h + 0.5 * jnp.dot(act, wd_ref[...], preferred_element_type=F32)
        o_ref[0, rows] = _rms(h, gf_ref[...])

    mix_rows(0, 2 * sub_blocks)
    hs = [stage_out_proj(rows) for rows in halves]
    acts = [stage_gate_up(hn) for _, hn in hs]
    for rows, (h, _), act in zip(halves, hs, acts):
        stage_down(rows, h, act)


def _prep_keys(ktp_ref, ktc_ref, ktn_ref, kt_buf, *, tm):
    kt_buf[:, 0:QBLK] = ktp_ref[0]
    kt_buf[:, QBLK:QBLK + tm] = ktc_ref[0]
    kt_buf[:, QBLK + tm:] = ktn_ref[0]


def _prep_values(vp_ref, vc_ref, vn_ref, vab_buf, *, tm):
    for lo, ref, n in ((0, vp_ref, QBLK), (QBLK, vc_ref, tm), (QBLK + tm, vn_ref, QBLK)):
        v = ref[0].astype(F32)
        vr = pltpu.roll(v, HEAD_DIM, 1)
        low = lax.broadcasted_iota(jnp.int32, v.shape, 1) < HEAD_DIM
        zero = jnp.zeros_like(v)
        ones_low = jnp.where(low, 1.0, 0.0).astype(BF16)
        ones_high = jnp.where(low, 0.0, 1.0).astype(BF16)
        vab_buf[0, 0, lo:lo + n, 0:LANES] = jnp.where(low, v, zero).astype(BF16)
        vab_buf[0, 1, lo:lo + n, 0:LANES] = jnp.where(low, zero, vr).astype(BF16)
        vab_buf[1, 0, lo:lo + n, 0:LANES] = jnp.where(low, vr, zero).astype(BF16)
        vab_buf[1, 1, lo:lo + n, 0:LANES] = jnp.where(low, zero, v).astype(BF16)
        for c in range(N_KV):
            vab_buf[c, 0, lo:lo + n, LANES:2 * LANES] = ones_low
            vab_buf[c, 1, lo:lo + n, LANES:2 * LANES] = ones_high


def _prep_pool(pcp_ref, pcc_ref, pcn_ref, hl_buf, edge_buf, i, *, tm, seq):
    n_tiles = seq // tm
    prev_halo = jnp.where(i > 0, pcp_ref[0], 0.0)
    next_halo = jnp.where(i < n_tiles - 1, pcn_ref[0], 0.0)
    zpad = jnp.zeros((POOL_MARGIN - POOL_HALO, POOL_WIDTH), F32)
    for r0, nr, u in ((0, POOL_MARGIN, jnp.concatenate([zpad, prev_halo], axis=0)),
                      (POOL_MARGIN, tm, pcc_ref[0]),
                      (POOL_MARGIN + tm, POOL_MARGIN, jnp.concatenate([next_halo, zpad], axis=0))):
        hi = u.astype(BF16)
        lo = (u - hi.astype(F32)).astype(BF16)
        for g in range(len(POOL_WINDOWS)):
            lanes = slice(g * GROUP_W, (g + 1) * GROUP_W)
            hl_buf[r0:r0 + nr, 2 * g * GROUP_W:(2 * g + 1) * GROUP_W] = hi[:, lanes]
            hl_buf[r0:r0 + nr, (2 * g + 1) * GROUP_W:(2 * g + 2) * GROUP_W] = lo[:, lanes]

    edge_buf[0, 0:POOL_HALO] = prev_halo
    edge_buf[0, POOL_HALO:] = pcc_ref[0, 0:2 * POOL_HALO]
    edge_buf[1, 0:2 * POOL_HALO] = pcc_ref[0, tm - 2 * POOL_HALO:tm]
    edge_buf[1, 2 * POOL_HALO:] = next_halo


def _mix_rows(sink_ref, q_ref, bias_ref, eye_ref, band_ref, poolw_ref, pscale_ref,
              kt_buf, vab_buf, hl_buf, edge_buf, d_buf, mix_buf, i, j0, j1, *, tm, seq):
    n_tiles = seq // tm
    nq = tm // QBLK
    nb = seq // QBLK

    def band_block(n):
        for g in range(len(POOL_WINDOWS)):
            win = hl_buf[n * QBLK:n * QBLK + 2 * QBLK, 2 * g * GROUP_W:(2 * g + 2) * GROUP_W]
            dd = jnp.dot(band_ref[g], win, preferred_element_type=F32)
            d_buf[n * QBLK:(n + 1) * QBLK, g * GROUP_W:(g + 1) * GROUP_W] = (
                dd[:, :GROUP_W] + dd[:, GROUP_W:])

    low_out = lax.broadcasted_iota(jnp.int32, (QBLK, LANES), 1) < HEAD_DIM
    eye = eye_ref[...]

    def scores(j, c):
        rows = slice(j * QBLK, (j + 1) * QBLK)
        keys = slice(j * QBLK, j * QBLK + KEYS)
        lhs = jnp.concatenate(
            [q_ref[0, rows, (2 * c) * LANES:(2 * c + 1) * LANES],
             q_ref[0, rows, (2 * c + 1) * LANES:(2 * c + 2) * LANES]], axis=0)
        kc = kt_buf[c * HEAD_DIM:(c + 1) * HEAD_DIM, keys]
        z = jnp.zeros_like(kc)
        rhs = jnp.concatenate(
            [jnp.concatenate([kc, z], axis=1), jnp.concatenate([z, kc], axis=1)], axis=0)
        return jnp.dot(lhs, rhs, preferred_element_type=F32)

    def softmax(j, c, s):
        blk = i * nq + j
        variant = jnp.where(blk == 0, 0, jnp.where(blk == nb - 1, 2, 1))
        bias = bias_ref[variant]
        p_rows, sink_terms = [], []
        for r in range(2):
            p_cols = []
            for par in range(2):
                sq = s[r * QBLK:(r + 1) * QBLK, par * KEYS:(par + 1) * KEYS] + bias
                sink = sink_ref[4 * c + 2 * r + par] * LOG2E
                m = jnp.maximum(jnp.max(sq, axis=-1, keepdims=True), sink)
                p = jnp.exp2(sq - m)
                p_cols.append(p.astype(BF16))
                sink_terms.append(1.0 / (jnp.sum(p, axis=-1, keepdims=True) + jnp.exp2(sink - m)))
            p_rows.append(jnp.concatenate(p_cols, axis=1))
        return jnp.concatenate(p_rows, axis=0), sink_terms

    def weighted_values(j, c, pmat, sink_terms):
        rows = slice(j * QBLK, (j + 1) * QBLK)
        keys = slice(j * QBLK, j * QBLK + KEYS)
        vrhs = jnp.concatenate([vab_buf[c, 0, keys, 0:LANES], vab_buf[c, 1, keys, 0:LANES]], axis=0)
        o = jnp.dot(pmat, vrhs, preferred_element_type=F32)
        for r in range(2):
            orow = o[r * QBLK:(r + 1) * QBLK]
            inv = jnp.where(low_out, sink_terms[2 * r], sink_terms[2 * r + 1])
            t = 2 * c + r
            mix_buf[rows, t * LANES:(t + 1) * LANES] = (orow * inv).astype(BF16)

    units = [(j, c) for j in range(j0, j1) for c in range(N_KV)]
    for n in range(j0, j1):
        band_block(n)
    s_next = scores(*units[0])
    for n, (j, c) in enumerate(units):
        s_cur = s_next
        if n + 1 < len(units):
            s_next = scores(*units[n + 1])
        weighted_values(j, c, *softmax(j, c, s_cur))

    for e, (row0, clipped) in enumerate(((0, i == 0), (tm - POOL_HALO, i == n_tiles - 1))):
        if not j0 * QBLK <= row0 < j1 * QBLK:
            continue
        tpos = i * tm + row0 + lax.broadcasted_iota(jnp.int32, (POOL_HALO, GROUP_W), 0)
        base = POOL_HALO
        for g, w in enumerate(POOL_WINDOWS):
            half = w // 2
            lanes = slice(g * GROUP_W, (g + 1) * GROUP_W)

            def wsum(lo, hi):
                acc = edge_buf[e, base + lo:base + lo + POOL_HALO, lanes]
                for k in range(lo + 1, hi + 1):
                    acc = acc + edge_buf[e, base + k:base + k + POOL_HALO, lanes]
                return acc

            def count(lo, hi):
                a = jnp.clip(tpos + lo, 0, seq)
                b = jnp.clip(tpos + hi + 1, 0, seq)
                return (b - a).astype(F32)

            mean = 0.5 * (wsum(-half, half - 1) / count(-half, half - 1)
                          + wsum(-half + 1, half) / count(-half + 1, half))
            fixed = mean - edge_buf[e, base:base + POOL_HALO, lanes]
            d_buf[row0:row0 + POOL_HALO, lanes] = jnp.where(
                clipped, fixed, d_buf[row0:row0 + POOL_HALO, lanes])

    rows = slice(j0 * QBLK, j1 * QBLK)
    for g in range(len(POOL_WINDOWS)):
        lanes = slice(g * GROUP_W, (g + 1) * GROUP_W)
        y = jnp.dot(d_buf[rows, lanes].astype(BF16), poolw_ref[g], preferred_element_type=F32)
        mix_buf[rows, Q_WIDTH + g * GROUP_W:Q_WIDTH + (g + 1) * GROUP_W] = (
            y * pscale_ref[:, lanes]).astype(BF16)


def _rope_tables(seq):
    f32 = np.float32
    inv_freq = f32(ROPE_THETA) ** (-np.arange(0, ROTARY_DIM, 2, dtype=f32) / f32(ROTARY_DIM))
    ang = np.arange(seq, dtype=f32)[:, None] * inv_freq[None, :]
    cos, sin = np.cos(ang.astype(np.float64)).astype(f32), np.sin(ang.astype(np.float64)).astype(f32)
    half = ROTARY_DIM // 2
    rest = HEAD_DIM - ROTARY_DIM
    c = np.concatenate([cos, cos, np.ones((seq, rest), f32)], axis=1)
    sa = np.concatenate([-sin, np.zeros((seq, HEAD_DIM - half), f32)], axis=1)
    sb = np.concatenate([np.zeros((seq, half), f32), sin, np.zeros((seq, rest), f32)], axis=1)
    rep = LANES // HEAD_DIM
    return tuple(jnp.asarray(np.tile(t, (1, rep))) for t in (c, sa, sb))


def _band_bias(seq):
    r = np.arange(QBLK)[:, None]
    s = np.arange(KEYS)[None, :]
    band = (s - r >= 0) & (s - r <= 2 * WINDOW)
    first = band & (s >= QBLK)
    last = band & (s < 2 * QBLK)
    out = np.stack([np.where(m, 0.0, NEG) for m in (first, band, last)])
    return jnp.asarray(out, dtype=F32)


def _pool_band():
    out = np.zeros((len(POOL_WINDOWS), QBLK, 2 * QBLK), np.float32)
    r = np.arange(QBLK)
    for g, w in enumerate(POOL_WINDOWS):
        half = w // 2
        for k in range(-half, half + 1):
            out[g, r, r + POOL_MARGIN + k] = (0.5 if abs(k) == half else 1.0) / w
        out[g, r, r + POOL_MARGIN] -= 1.0
    return jnp.asarray(out).astype(BF16)


def _const_spec(shape):
    nd = len(shape)
    return pl.BlockSpec(shape, lambda *_: (0,) * nd, pipeline_mode=pl.Buffered(1))


def kernel(x, ffn1_norm, ffn1_w_gate, ffn1_w_up, ffn1_w_down, mix_norm, w_in, sink_logits,
           pool_w, pool_scale, w_out, ffn2_norm, ffn2_w_gate, ffn2_w_up, ffn2_w_down, final_norm):
    B, S, D = x.shape
    assert D == D_MODEL and S % TM == 0 and TM % QBLK == 0 and ffn1_norm.shape[0] == 1
    assert S % TM_IN == 0 and TM_IN % SUB_ROWS == 0
    tm = TM
    grid = (B, S // tm)
    in_width = w_in.shape[-1]
    params = pltpu.CompilerParams(dimension_semantics=("arbitrary", "arbitrary"),
                                  vmem_limit_bytes=VMEM_LIMIT)

    c_tab, sa_tab, sb_tab = _rope_tables(S)
    row = lambda g: g.reshape(1, -1).astype(F32)
    tile_spec = lambda width, rows=tm: pl.BlockSpec((1, rows, width), lambda b, i: (b, i, 0))
    tm_in = TM_IN
    tab_spec = pl.BlockSpec((tm_in, LANES), lambda b, i: (i, 0))
    hbm_spec = pl.BlockSpec(memory_space=pl.ANY)

    h1, q, kt, v, pc = pl.pallas_call(
        _ffn_in_kernel,
        grid=(B, S // tm_in),
        in_specs=[
            tile_spec(D, tm_in),
            _const_spec((1, D)),
            hbm_spec, hbm_spec, hbm_spec,
            _const_spec((1, D)),
            hbm_spec,
            tab_spec, tab_spec, tab_spec,
        ],
        out_specs=[
            tile_spec(D, tm_in),
            tile_spec(Q_WIDTH, tm_in),
            pl.BlockSpec((1, KV_WIDTH, tm_in), lambda b, i: (b, 0, i)),
            tile_spec(KV_WIDTH, tm_in),
            tile_spec(POOL_WIDTH, tm_in),
        ],
        out_shape=[
            jax.ShapeDtypeStruct((B, S, D), F32),
            jax.ShapeDtypeStruct((B, S, Q_WIDTH), BF16),
            jax.ShapeDtypeStruct((B, KV_WIDTH, S), BF16),
            jax.ShapeDtypeStruct((B, S, KV_WIDTH), BF16),
            jax.ShapeDtypeStruct((B, S, POOL_WIDTH), F32),
        ],
        scratch_shapes=[
            pltpu.VMEM((D, D_FF), BF16), pltpu.VMEM((D, D_FF), BF16), pltpu.VMEM((D_FF, D), BF16),
            pltpu.VMEM((D, in_width), BF16),
        ],
        compiler_params=params,
        name="ffn1_inproj",
    )(x, row(ffn1_norm[0]), ffn1_w_gate[0], ffn1_w_up[0], ffn1_w_down[0], row(mix_norm[0]),
      w_in[0], c_tab, sa_tab, sb_tab)

    qb = tm // QBLK
    pb = tm // POOL_HALO
    n_qb = S // QBLK
    n_pb = S // POOL_HALO
    mix_spec = pl.BlockSpec
    prev_q = lambda i: jnp.maximum(i * qb - 1, 0)
    next_q = lambda i: jnp.minimum((i + 1) * qb, n_qb - 1)
    prev_p = lambda i: jnp.maximum(i * pb - 1, 0)
    next_p = lambda i: jnp.minimum((i + 1) * pb, n_pb - 1)

    out = pl.pallas_call(
        functools.partial(_mix_ffn_kernel, tm=tm, seq=S),
        grid=grid,
        in_specs=[
            pl.BlockSpec(memory_space=pltpu.SMEM),
            tile_spec(D),
            mix_spec((1, tm, Q_WIDTH), lambda b, i: (b, i, 0)),
            mix_spec((1, KV_WIDTH, QBLK), lambda b, i: (b, 0, prev_q(i))),
            mix_spec((1, KV_WIDTH, tm), lambda b, i: (b, 0, i)),
            mix_spec((1, KV_WIDTH, QBLK), lambda b, i: (b, 0, next_q(i))),
            mix_spec((1, QBLK, KV_WIDTH), lambda b, i: (b, prev_q(i), 0)),
            mix_spec((1, tm, KV_WIDTH), lambda b, i: (b, i, 0)),
            mix_spec((1, QBLK, KV_WIDTH), lambda b, i: (b, next_q(i), 0)),
            mix_spec((1, POOL_HALO, POOL_WIDTH), lambda b, i: (b, prev_p(i), 0)),
            mix_spec((1, tm, POOL_WIDTH), lambda b, i: (b, i, 0)),
            mix_spec((1, POOL_HALO, POOL_WIDTH), lambda b, i: (b, next_p(i), 0)),
            _const_spec((3, QBLK, KEYS)),
            _const_spec((QBLK, QBLK)),
            _const_spec((len(POOL_WINDOWS), QBLK, 2 * QBLK)),
            _const_spec((len(POOL_WINDOWS), GROUP_W, GROUP_W)),
            _const_spec((1, POOL_WIDTH)),
            hbm_spec,
            _const_spec((1, D)),
            hbm_spec, hbm_spec, hbm_spec,
            _const_spec((1, D)),
        ],
        out_specs=tile_spec(D),
        out_shape=jax.ShapeDtypeStruct((B, S, D), x.dtype),
        scratch_shapes=[
            pltpu.VMEM((KV_WIDTH, tm + 2 * QBLK), BF16),
            pltpu.VMEM((N_KV, 2, tm + 2 * QBLK, 2 * LANES), BF16),
            pltpu.VMEM((tm + 2 * POOL_MARGIN, 2 * POOL_WIDTH), BF16),
            pltpu.VMEM((2, 3 * POOL_HALO, POOL_WIDTH), F32),
            pltpu.VMEM((tm, POOL_WIDTH), F32),
            pltpu.VMEM((tm, D), BF16),
            pltpu.VMEM((D, D), BF16),
            pltpu.VMEM((D, D_FF), BF16), pltpu.VMEM((D, D_FF), BF16), pltpu.VMEM((D_FF, D), BF16),
        ],
        compiler_params=params,
        name="mix_ffn2",
    )(sink_logits[0].astype(F32), h1, q, kt, kt, kt, v, v, v, pc, pc, pc,
      _band_bias(S), jnp.asarray(np.eye(QBLK, dtype=np.float32)).astype(BF16), _pool_band(),
      pool_w[0].astype(BF16),
      row(pool_scale[0]), w_out[0],
      row(ffn2_norm[0]), ffn2_w_gate[0], ffn2_w_up[0], ffn2_w_down[0], row(final_norm))
    return out
```

```python
import functools

import numpy as np
import jax
import jax.numpy as jnp
from jax import lax
from jax.experimental import pallas as pl
from jax.experimental.pallas import tpu as pltpu

D_MODEL = 1024
HEAD_DIM = 64
N_HEADS = 8
N_KV = 2
Q_WIDTH = N_HEADS * HEAD_DIM
KV_WIDTH = N_KV * HEAD_DIM
WINDOW = 128
QBLK = 128
KEYS = 3 * QBLK
ROPE_THETA = 500000.0
ROTARY_DIM = HEAD_DIM // 4
POOL_WINDOWS = (2, 4, 8, 16)
POOL_HALO = 8
POOL_MARGIN = 64
POOL_WIDTH = 512
GROUP_W = 128
D_FF = 2816
EPS = 1e-6
LANES = 128
NEG = -1e30
LOG2E = 1.4426950408889634

TM = 512
TM_IN = 1024
SUB_ROWS = 256
CAST_ROWS = 128
CAST_SLOTS = 3
VMEM_LIMIT = 58 * 1024 * 1024

F32 = jnp.float32
BF16 = jnp.bfloat16


def _rms(x, g):
    ms = jnp.mean(x * x, axis=-1, keepdims=True)
    return x * lax.rsqrt(ms + EPS) * g


def _swiglu(xn, wg_ref, wu_ref, wd_ref):
    gate = jnp.dot(xn, wg_ref[...], preferred_element_type=F32)
    up = jnp.dot(xn, wu_ref[...], preferred_element_type=F32)
    act = (gate * jax.nn.sigmoid(gate) * up).astype(BF16)
    return jnp.dot(act, wd_ref[...], preferred_element_type=F32)


def _load_weight_bf16(src, dst, stage, sem):
    n_rows, width = src.shape
    n_chunks = n_rows // CAST_ROWS
    ahead = CAST_SLOTS - 1

    def chunk_copy(k, slot):
        return pltpu.make_async_copy(src.at[pl.ds(k * CAST_ROWS, CAST_ROWS)],
                                     stage.at[slot, :, pl.ds(0, width)], sem.at[slot])

    for k in range(min(ahead, n_chunks)):
        chunk_copy(k, k).start()

    def body(k, carry):
        slot = lax.rem(k, CAST_SLOTS)

        @pl.when(k + ahead < n_chunks)
        def _():
            chunk_copy(k + ahead, lax.rem(k + ahead, CAST_SLOTS)).start()

        chunk_copy(k, slot).wait()
        r0 = pl.multiple_of(k * CAST_ROWS, CAST_ROWS)
        dst[pl.ds(r0, CAST_ROWS), :] = stage[slot, :, 0:width].astype(BF16)
        return carry

    lax.fori_loop(0, n_chunks, body, 0)


def _load_weights_bf16(pairs):
    def scoped(stage, sem):
        for src, dst in pairs:
            _load_weight_bf16(src, dst, stage, sem)

    pl.run_scoped(scoped, pltpu.VMEM((CAST_SLOTS, CAST_ROWS, D_FF), F32),
                  pltpu.SemaphoreType.DMA((CAST_SLOTS,)))


def _rope(t, c, sa, sb):
    t_plus = pltpu.roll(t, LANES - ROTARY_DIM // 2, 1)
    t_minus = pltpu.roll(t, ROTARY_DIM // 2, 1)
    return t * c + t_plus * sa + t_minus * sb


def _ffn_in_kernel(x_ref, g1_ref, wg_hbm, wu_hbm, wd_hbm, gm_ref, win_hbm,
                   c_ref, sa_ref, sb_ref,
                   h_ref, q_ref, kt_ref, v_ref, pc_ref,
                   wg_ref, wu_ref, wd_ref, win_ref):
    @pl.when((pl.program_id(0) == 0) & (pl.program_id(1) == 0))
    def _():
        _load_weights_bf16(((wg_hbm, wg_ref), (wu_hbm, wu_ref), (wd_hbm, wd_ref),
                            (win_hbm, win_ref)))

    tm = x_ref.shape[1]
    sub = SUB_ROWS
    scale = HEAD_DIM ** -0.5 * LOG2E
    def stage_norm(rows):
        return _rms(x_ref[0, rows], g1_ref[...]).astype(BF16)

    def stage_gate_up(xn):
        gate = jnp.dot(xn, wg_ref[...], preferred_element_type=F32)
        up = jnp.dot(xn, wu_ref[...], preferred_element_type=F32)
        return (gate * jax.nn.sigmoid(gate) * up).astype(BF16)

    def stage_down(rows, act):
        h = x_ref[0, rows] + 0.5 * jnp.dot(act, wd_ref[...], preferred_element_type=F32)
        h_ref[0, rows] = h
        return _rms(h, gm_ref[...]).astype(BF16)

    def stage_proj(hn):
        return jnp.dot(hn, win_ref[...], preferred_element_type=F32)

    def stage_out(rows, u):
        c, sa, sb = c_ref[rows], sa_ref[rows], sb_ref[rows]
        for t in range(Q_WIDTH // LANES):
            qt = _rope(u[:, t * LANES:(t + 1) * LANES], c, sa, sb)
            q_ref[0, rows, t * LANES:(t + 1) * LANES] = (qt * scale).astype(BF16)
        k = _rope(u[:, Q_WIDTH:Q_WIDTH + KV_WIDTH], c, sa, sb)
        kt_ref[0, :, rows] = k.T.astype(BF16)
        v_ref[0, rows] = u[:, Q_WIDTH + KV_WIDTH:Q_WIDTH + 2 * KV_WIDTH].astype(BF16)
        pc_ref[0, rows] = u[:, Q_WIDTH + 2 * KV_WIDTH:]

    for pair in range(tm // (2 * sub)):
        ra = slice((2 * pair) * sub, (2 * pair + 1) * sub)
        rb = slice((2 * pair + 1) * sub, (2 * pair + 2) * sub)
        xn_a, xn_b = stage_norm(ra), stage_norm(rb)
        act_a = stage_gate_up(xn_a)
        act_b = stage_gate_up(xn_b)
        hn_a = stage_down(ra, act_a)
        hn_b = stage_down(rb, act_b)
        u_a = stage_proj(hn_a)
        u_b = stage_proj(hn_b)
        stage_out(ra, u_a)
        stage_out(rb, u_b)


def _mix_ffn_kernel(sink_ref, h_ref, q_ref, ktp_ref, ktc_ref, ktn_ref,
                    vp_ref, vc_ref, vn_ref, pcp_ref, pcc_ref, pcn_ref,
                    bias_ref, eye_ref, band_ref, poolw_ref, pscale_ref, wout_hbm,
                    g2_ref, wg_hbm, wu_hbm, wd_hbm, gf_ref,
                    o_ref,
                    kt_buf, vab_buf, hl_buf, edge_buf, d_buf, mix_buf,
                    wout_ref, wg_ref, wu_ref, wd_ref, *, tm, seq):
    @pl.when((pl.program_id(0) == 0) & (pl.program_id(1) == 0))
    def _():
        _load_weights_bf16(((wout_hbm, wout_ref), (wg_hbm, wg_ref), (wu_hbm, wu_ref),
                            (wd_hbm, wd_ref)))

    i = pl.program_id(1)
    _prep_keys(ktp_ref, ktc_ref, ktn_ref, kt_buf, tm=tm)
    _prep_values(vp_ref, vc_ref, vn_ref, vab_buf, tm=tm)
    _prep_pool(pcp_ref, pcc_ref, pcn_ref, hl_buf, edge_buf, i, tm=tm, seq=seq)
    mix_rows = functools.partial(
        _mix_rows, sink_ref, q_ref, bias_ref, eye_ref, band_ref, poolw_ref, pscale_ref,
        kt_buf, vab_buf, hl_buf, edge_buf, d_buf, mix_buf, i, tm=tm, seq=seq)

    sub = tm // 2
    sub_blocks = sub // QBLK
    halves = (slice(0, sub), slice(sub, tm))

    def stage_out_proj(rows):
        h = h_ref[0, rows] + jnp.dot(mix_buf[rows], wout_ref[...], preferred_element_type=F32)
        return h, _rms(h, g2_ref[...]).astype(BF16)

    def stage_gate_up(hn):
        gate = jnp.dot(hn, wg_ref[...], preferred_element_type=F32)
        up = jnp.dot(hn, wu_ref[...], preferred_element_type=F32)
        return (gate * jax.nn.sigmoid(gate) * up).astype(BF16)

    def stage_down(rows, h, act):
        h = h + 0.5 * jnp.dot(act, wd_ref[...], preferred_element_type=F32)
        o_ref[0, rows] = _rms(h, gf_ref[...])

    mix_rows(0, 2 * sub_blocks)
    hs = [stage_out_proj(rows) for rows in halves]
    acts = [stage_gate_up(hn) for _, hn in hs]
    for rows, (h, _), act in zip(halves, hs, acts):
        stage_down(rows, h, act)


def _prep_keys(ktp_ref, ktc_ref, ktn_ref, kt_buf, *, tm):
    kt_buf[:, 0:QBLK] = ktp_ref[0]
    kt_buf[:, QBLK:QBLK + tm] = ktc_ref[0]
    kt_buf[:, QBLK + tm:] = ktn_ref[0]


def _prep_values(vp_ref, vc_ref, vn_ref, vab_buf, *, tm):
    for lo, ref, n in ((0, vp_ref, QBLK), (QBLK, vc_ref, tm), (QBLK + tm, vn_ref, QBLK)):
        v = ref[0].astype(F32)
        vr = pltpu.roll(v, HEAD_DIM, 1)
        low = lax.broadcasted_iota(jnp.int32, v.shape, 1) < HEAD_DIM
        zero = jnp.zeros_like(v)
        ones_low = jnp.where(low, 1.0, 0.0).astype(BF16)
        ones_high = jnp.where(low, 0.0, 1.0).astype(BF16)
        vab_buf[0, 0, lo:lo + n, 0:LANES] = jnp.where(low, v, zero).astype(BF16)
        vab_buf[0, 1, lo:lo + n, 0:LANES] = jnp.where(low, zero, vr).astype(BF16)
        vab_buf[1, 0, lo:lo + n, 0:LANES] = jnp.where(low, vr, zero).astype(BF16)
        vab_buf[1, 1, lo:lo + n, 0:LANES] = jnp.where(low, zero, v).astype(BF16)
        for c in range(N_KV):
            vab_buf[c, 0, lo:lo + n, LANES:2 * LANES] = ones_low
            vab_buf[c, 1, lo:lo + n, LANES:2 * LANES] = ones_high


def _prep_pool(pcp_ref, pcc_ref, pcn_ref, hl_buf, edge_buf, i, *, tm, seq):
    n_tiles = seq // tm
    prev_halo = jnp.where(i > 0, pcp_ref[0], 0.0)
    next_halo = jnp.where(i < n_tiles - 1, pcn_ref[0], 0.0)
    zpad = jnp.zeros((POOL_MARGIN - POOL_HALO, POOL_WIDTH), F32)
    for r0, nr, u in ((0, POOL_MARGIN, jnp.concatenate([zpad, prev_halo], axis=0)),
                      (POOL_MARGIN, tm, pcc_ref[0]),
                      (POOL_MARGIN + tm, POOL_MARGIN, jnp.concatenate([next_halo, zpad], axis=0))):
        hi = u.astype(BF16)
        lo = (u - hi.astype(F32)).astype(BF16)
        for g in range(len(POOL_WINDOWS)):
            lanes = slice(g * GROUP_W, (g + 1) * GROUP_W)
            hl_buf[r0:r0 + nr, 2 * g * GROUP_W:(2 * g + 1) * GROUP_W] = hi[:, lanes]
            hl_buf[r0:r0 + nr, (2 * g + 1) * GROUP_W:(2 * g + 2) * GROUP_W] = lo[:, lanes]

    edge_buf[0, 0:POOL_HALO] = prev_halo
    edge_buf[0, POOL_HALO:] = pcc_ref[0, 0:2 * POOL_HALO]
    edge_buf[1, 0:2 * POOL_HALO] = pcc_ref[0, tm - 2 * POOL_HALO:tm]
    edge_buf[1, 2 * POOL_HALO:] = next_halo


def _mix_rows(sink_ref, q_ref, bias_ref, eye_ref, band_ref, poolw_ref, pscale_ref,
              kt_buf, vab_buf, hl_buf, edge_buf, d_buf, mix_buf, i, j0, j1, *, tm, seq):
    n_tiles = seq // tm
    nq = tm // QBLK
    nb = seq // QBLK

    def band_block(n):
        for g in range(len(POOL_WINDOWS)):
            win = hl_buf[n * QBLK:n * QBLK + 2 * QBLK, 2 * g * GROUP_W:(2 * g + 2) * GROUP_W]
            dd = jnp.dot(band_ref[g], win, preferred_element_type=F32)
            d_buf[n * QBLK:(n + 1) * QBLK, g * GROUP_W:(g + 1) * GROUP_W] = (
                dd[:, :GROUP_W] + dd[:, GROUP_W:])

    low_out = lax.broadcasted_iota(jnp.int32, (QBLK, LANES), 1) < HEAD_DIM
    eye = eye_ref[...]

    def scores(j, c):
        blk = i * nq + j
        variant = jnp.where(blk == 0, 0, jnp.where(blk == nb - 1, 2, 1))
        bias_t = bias_ref[variant]
        rows = slice(j * QBLK, (j + 1) * QBLK)
        keys = slice(j * QBLK, j * QBLK + KEYS)
        lhs = jnp.concatenate(
            [jnp.concatenate([q_ref[0, rows, (2 * c) * LANES:(2 * c + 1) * LANES], eye], axis=1),
             jnp.concatenate([q_ref[0, rows, (2 * c + 1) * LANES:(2 * c + 2) * LANES], eye],
                             axis=1)], axis=0)
        kc = kt_buf[c * HEAD_DIM:(c + 1) * HEAD_DIM, keys]
        z = jnp.zeros_like(kc)
        rhs = jnp.concatenate(
            [jnp.concatenate([kc, z], axis=1), jnp.concatenate([z, kc], axis=1), bias_t],
            axis=0)
        return jnp.dot(lhs, rhs, preferred_element_type=F32)

    def softmax(c, s):
        p_rows, sink_terms = [], []
        for r in range(2):
            p_cols = []
            for par in range(2):
                sq = s[r * QBLK:(r + 1) * QBLK, par * KEYS:(par + 1) * KEYS]
                sink = sink_ref[4 * c + 2 * r + par] * LOG2E
                m = jnp.maximum(jnp.max(sq, axis=-1, keepdims=True), sink)
                p = jnp.exp2(sq - m)
                p_cols.append(p.astype(BF16))
                sink_terms.append(1.0 / (jnp.sum(p, axis=-1, keepdims=True) + jnp.exp2(sink - m)))
            p_rows.append(jnp.concatenate(p_cols, axis=1))
        return jnp.concatenate(p_rows, axis=0), sink_terms

    def weighted_values(j, c, pmat, sink_terms):
        rows = slice(j * QBLK, (j + 1) * QBLK)
        keys = slice(j * QBLK, j * QBLK + KEYS)
        vrhs = jnp.concatenate([vab_buf[c, 0, keys, 0:LANES], vab_buf[c, 1, keys, 0:LANES]], axis=0)
        o = jnp.dot(pmat, vrhs, preferred_element_type=F32)
        for r in range(2):
            orow = o[r * QBLK:(r + 1) * QBLK]
            inv = jnp.where(low_out, sink_terms[2 * r], sink_terms[2 * r + 1])
            t = 2 * c + r
            mix_buf[rows, t * LANES:(t + 1) * LANES] = (orow * inv).astype(BF16)

    units = [(j, c) for j in range(j0, j1) for c in range(N_KV)]
    for n in range(j0, j1):
        band_block(n)
    s_next = scores(*units[0])
    for n, (j, c) in enumerate(units):
        s_cur = s_next
        if n + 1 < len(units):
            s_next = scores(*units[n + 1])
        weighted_values(j, c, *softmax(c, s_cur))

    for e, (row0, clipped) in enumerate(((0, i == 0), (tm - POOL_HALO, i == n_tiles - 1))):
        if not j0 * QBLK <= row0 < j1 * QBLK:
            continue
        tpos = i * tm + row0 + lax.broadcasted_iota(jnp.int32, (POOL_HALO, GROUP_W), 0)
        base = POOL_HALO
        for g, w in enumerate(POOL_WINDOWS):
            half = w // 2
            lanes = slice(g * GROUP_W, (g + 1) * GROUP_W)

            def wsum(lo, hi):
                acc = edge_buf[e, base + lo:base + lo + POOL_HALO, lanes]
                for k in range(lo + 1, hi + 1):
                    acc = acc + edge_buf[e, base + k:base + k + POOL_HALO, lanes]
                return acc

            def count(lo, hi):
                a = jnp.clip(tpos + lo, 0, seq)
                b = jnp.clip(tpos + hi + 1, 0, seq)
                return (b - a).astype(F32)

            mean = 0.5 * (wsum(-half, half - 1) / count(-half, half - 1)
                          + wsum(-half + 1, half) / count(-half + 1, half))
            fixed = mean - edge_buf[e, base:base + POOL_HALO, lanes]
            d_buf[row0:row0 + POOL_HALO, lanes] = jnp.where(
                clipped, fixed, d_buf[row0:row0 + POOL_HALO, lanes])

    rows = slice(j0 * QBLK, j1 * QBLK)
    for g in range(len(POOL_WINDOWS)):
        lanes = slice(g * GROUP_W, (g + 1) * GROUP_W)
        y = jnp.dot(d_buf[rows, lanes].astype(BF16), poolw_ref[g], preferred_element_type=F32)
        mix_buf[rows, Q_WIDTH + g * GROUP_W:Q_WIDTH + (g + 1) * GROUP_W] = (
            y * pscale_ref[:, lanes]).astype(BF16)


def _rope_tables(seq):
    f32 = np.float32
    inv_freq = f32(ROPE_THETA) ** (-np.arange(0, ROTARY_DIM, 2, dtype=f32) / f32(ROTARY_DIM))
    ang = np.arange(seq, dtype=f32)[:, None] * inv_freq[None, :]
    cos, sin = np.cos(ang.astype(np.float64)).astype(f32), np.sin(ang.astype(np.float64)).astype(f32)
    half = ROTARY_DIM // 2
    rest = HEAD_DIM - ROTARY_DIM
    c = np.concatenate([cos, cos, np.ones((seq, rest), f32)], axis=1)
    sa = np.concatenate([-sin, np.zeros((seq, HEAD_DIM - half), f32)], axis=1)
    sb = np.concatenate([np.zeros((seq, half), f32), sin, np.zeros((seq, rest), f32)], axis=1)
    rep = LANES // HEAD_DIM
    return tuple(jnp.asarray(np.tile(t, (1, rep))) for t in (c, sa, sb))


def _band_bias(seq):
    r = np.arange(QBLK)[:, None]
    s = np.arange(KEYS)[None, :]
    band = (s - r >= 0) & (s - r <= 2 * WINDOW)
    first = band & (s >= QBLK)
    last = band & (s < 2 * QBLK)
    out = np.stack([np.tile(np.where(m, 0.0, NEG), (1, 2)) for m in (first, band, last)])
    return jnp.asarray(out, dtype=F32).astype(BF16)


def _pool_band():
    out = np.zeros((len(POOL_WINDOWS), QBLK, 2 * QBLK), np.float32)
    r = np.arange(QBLK)
    for g, w in enumerate(POOL_WINDOWS):
        half = w // 2
        for k in range(-half, half + 1):
            out[g, r, r + POOL_MARGIN + k] = (0.5 if abs(k) == half else 1.0) / w
        out[g, r, r + POOL_MARGIN] -= 1.0
    return jnp.asarray(out).astype(BF16)


def _const_spec(shape):
    nd = len(shape)
    return pl.BlockSpec(shape, lambda *_: (0,) * nd, pipeline_mode=pl.Buffered(1))


def kernel(x, ffn1_norm, ffn1_w_gate, ffn1_w_up, ffn1_w_down, mix_norm, w_in, sink_logits,
           pool_w, pool_scale, w_out, ffn2_norm, ffn2_w_gate, ffn2_w_up, ffn2_w_down, final_norm):
    B, S, D = x.shape
    assert D == D_MODEL and S % TM == 0 and TM % QBLK == 0 and ffn1_norm.shape[0] == 1
    assert S % TM_IN == 0 and TM_IN % SUB_ROWS == 0
    tm = TM
    grid = (B, S // tm)
    in_width = w_in.shape[-1]
    params = pltpu.CompilerParams(dimension_semantics=("arbitrary", "arbitrary"),
                                  vmem_limit_bytes=VMEM_LIMIT)

    c_tab, sa_tab, sb_tab = _rope_tables(S)
    row = lambda g: g.reshape(1, -1).astype(F32)
    tile_spec = lambda width, rows=tm: pl.BlockSpec((1, rows, width), lambda b, i: (b, i, 0))
    tm_in = TM_IN
    tab_spec = pl.BlockSpec((tm_in, LANES), lambda b, i: (i, 0))
    hbm_spec = pl.BlockSpec(memory_space=pl.ANY)

    h1, q, kt, v, pc = pl.pallas_call(
        _ffn_in_kernel,
        grid=(B, S // tm_in),
        in_specs=[
            tile_spec(D, tm_in),
            _const_spec((1, D)),
            hbm_spec, hbm_spec, hbm_spec,
            _const_spec((1, D)),
            hbm_spec,
            tab_spec, tab_spec, tab_spec,
        ],
        out_specs=[
            tile_spec(D, tm_in),
            tile_spec(Q_WIDTH, tm_in),
            pl.BlockSpec((1, KV_WIDTH, tm_in), lambda b, i: (b, 0, i)),
            tile_spec(KV_WIDTH, tm_in),
            tile_spec(POOL_WIDTH, tm_in),
        ],
        out_shape=[
            jax.ShapeDtypeStruct((B, S, D), F32),
            jax.ShapeDtypeStruct((B, S, Q_WIDTH), BF16),
            jax.ShapeDtypeStruct((B, KV_WIDTH, S), BF16),
            jax.ShapeDtypeStruct((B, S, KV_WIDTH), BF16),
            jax.ShapeDtypeStruct((B, S, POOL_WIDTH), F32),
        ],
        scratch_shapes=[
            pltpu.VMEM((D, D_FF), BF16), pltpu.VMEM((D, D_FF), BF16), pltpu.VMEM((D_FF, D), BF16),
            pltpu.VMEM((D, in_width), BF16),
        ],
        compiler_params=params,
        name="ffn1_inproj",
    )(x, row(ffn1_norm[0]), ffn1_w_gate[0], ffn1_w_up[0], ffn1_w_down[0], row(mix_norm[0]),
      w_in[0], c_tab, sa_tab, sb_tab)

    qb = tm // QBLK
    pb = tm // POOL_HALO
    n_qb = S // QBLK
    n_pb = S // POOL_HALO
    mix_spec = pl.BlockSpec
    prev_q = lambda i: jnp.maximum(i * qb - 1, 0)
    next_q = lambda i: jnp.minimum((i + 1) * qb, n_qb - 1)
    prev_p = lambda i: jnp.maximum(i * pb - 1, 0)
    next_p = lambda i: jnp.minimum((i + 1) * pb, n_pb - 1)

    out = pl.pallas_call(
        functools.partial(_mix_ffn_kernel, tm=tm, seq=S),
        grid=grid,
        in_specs=[
            pl.BlockSpec(memory_space=pltpu.SMEM),
            tile_spec(D),
            mix_spec((1, tm, Q_WIDTH), lambda b, i: (b, i, 0)),
            mix_spec((1, KV_WIDTH, QBLK), lambda b, i: (b, 0, prev_q(i))),
            mix_spec((1, KV_WIDTH, tm), lambda b, i: (b, 0, i)),
            mix_spec((1, KV_WIDTH, QBLK), lambda b, i: (b, 0, next_q(i))),
            mix_spec((1, QBLK, KV_WIDTH), lambda b, i: (b, prev_q(i), 0)),
            mix_spec((1, tm, KV_WIDTH), lambda b, i: (b, i, 0)),
            mix_spec((1, QBLK, KV_WIDTH), lambda b, i: (b, next_q(i), 0)),
            mix_spec((1, POOL_HALO, POOL_WIDTH), lambda b, i: (b, prev_p(i), 0)),
            mix_spec((1, tm, POOL_WIDTH), lambda b, i: (b, i, 0)),
            mix_spec((1, POOL_HALO, POOL_WIDTH), lambda b, i: (b, next_p(i), 0)),
            _const_spec((3, QBLK, 2 * KEYS)),
            _const_spec((QBLK, QBLK)),
            _const_spec((len(POOL_WINDOWS), QBLK, 2 * QBLK)),
            _const_spec((len(POOL_WINDOWS), GROUP_W, GROUP_W)),
            _const_spec((1, POOL_WIDTH)),
            hbm_spec,
            _const_spec((1, D)),
            hbm_spec, hbm_spec, hbm_spec,
            _const_spec((1, D)),
        ],
        out_specs=tile_spec(D),
        out_shape=jax.ShapeDtypeStruct((B, S, D), x.dtype),
        scratch_shapes=[
            pltpu.VMEM((KV_WIDTH, tm + 2 * QBLK), BF16),
            pltpu.VMEM((N_KV, 2, tm + 2 * QBLK, 2 * LANES), BF16),
            pltpu.VMEM((tm + 2 * POOL_MARGIN, 2 * POOL_WIDTH), BF16),
            pltpu.VMEM((2, 3 * POOL_HALO, POOL_WIDTH), F32),
            pltpu.VMEM((tm, POOL_WIDTH), F32),
            pltpu.VMEM((tm, D), BF16),
            pltpu.VMEM((D, D), BF16),
            pltpu.VMEM((D, D_FF), BF16), pltpu.VMEM((D, D_FF), BF16), pltpu.VMEM((D_FF, D), BF16),
        ],
        compiler_params=params,
        name="mix_ffn2",
    )(sink_logits[0].astype(F32), h1, q, kt, kt, kt, v, v, v, pc, pc, pc,
      _band_bias(S), jnp.asarray(np.eye(QBLK, dtype=np.float32)).astype(BF16), _pool_band(),
      pool_w[0].astype(BF16),
      row(pool_scale[0]), w_out[0],
      row(ffn2_norm[0]), ffn2_w_gate[0], ffn2_w_up[0], ffn2_w_down[0], row(final_norm))
    return out
```

```python
import functools

import numpy as np
import jax
import jax.numpy as jnp
from jax import lax
from jax.experimental import pallas as pl
from jax.experimental.pallas import tpu as pltpu

D_MODEL = 1024
HEAD_DIM = 64
N_HEADS = 8
N_KV = 2
Q_WIDTH = N_HEADS * HEAD_DIM
KV_WIDTH = N_KV * HEAD_DIM
WINDOW = 128
QBLK = 128
KEYS = 3 * QBLK
ROPE_THETA = 500000.0
ROTARY_DIM = HEAD_DIM // 4
POOL_WINDOWS = (2, 4, 8, 16)
POOL_HALO = 8
POOL_MARGIN = 64
POOL_WIDTH = 512
GROUP_W = 128
D_FF = 2816
EPS = 1e-6
LANES = 128
NEG = -1e30
LOG2E = 1.4426950408889634

TM = 512
TM_IN = 1024
SUB_ROWS = 256
CAST_ROWS_IN = 128
CAST_ROWS_MIX = 256
CAST_SLOTS = 3
VMEM_LIMIT = 58 * 1024 * 1024

F32 = jnp.float32
BF16 = jnp.bfloat16


def _rms(x, g):
    ms = jnp.mean(x * x, axis=-1, keepdims=True)
    return x * lax.rsqrt(ms + EPS) * g


def _gated(gate, up):
    return (gate * jax.nn.sigmoid(gate) * up).astype(BF16)


def _load_weight_bf16(src, dst, stage, sem):
    n_rows, width = src.shape
    chunk = stage.shape[1]
    n_chunks = n_rows // chunk
    ahead = CAST_SLOTS - 1

    def chunk_copy(k, slot):
        return pltpu.make_async_copy(src.at[pl.ds(k * chunk, chunk)],
                                     stage.at[slot, :, pl.ds(0, width)], sem.at[slot])

    for k in range(min(ahead, n_chunks)):
        chunk_copy(k, k).start()

    def body(k, carry):
        slot = lax.rem(k, CAST_SLOTS)

        @pl.when(k + ahead < n_chunks)
        def _():
            chunk_copy(k + ahead, lax.rem(k + ahead, CAST_SLOTS)).start()

        chunk_copy(k, slot).wait()
        r0 = pl.multiple_of(k * chunk, chunk)
        dst[pl.ds(r0, chunk), :] = stage[slot, :, 0:width].astype(BF16)
        return carry

    lax.fori_loop(0, n_chunks, body, 0)


def _load_weights_bf16(pairs, chunk_rows):
    def scoped(stage, sem):
        for src, dst in pairs:
            _load_weight_bf16(src, dst, stage, sem)

    pl.run_scoped(scoped, pltpu.VMEM((CAST_SLOTS, chunk_rows, D_FF), F32),
                  pltpu.SemaphoreType.DMA((CAST_SLOTS,)))


def _rope(t, c, sa, sb):
    t_plus = pltpu.roll(t, LANES - ROTARY_DIM // 2, 1)
    t_minus = pltpu.roll(t, ROTARY_DIM // 2, 1)
    return t * c + t_plus * sa + t_minus * sb


def _ffn_in_kernel(x_ref, g1_ref, wg_hbm, wu_hbm, wd_hbm, gm_ref, win_hbm,
                   c_ref, sa_ref, sb_ref,
                   h_ref, q_ref, kt_ref, v_ref, pc_ref,
                   wg_ref, wu_ref, wd_ref, win_ref):
    @pl.when((pl.program_id(0) == 0) & (pl.program_id(1) == 0))
    def _():
        _load_weights_bf16(((wg_hbm, wg_ref), (wu_hbm, wu_ref), (wd_hbm, wd_ref),
                            (win_hbm, win_ref)), CAST_ROWS_IN)

    tm = x_ref.shape[1]
    sub = SUB_ROWS
    scale = HEAD_DIM ** -0.5 * LOG2E
    def stage_norm(rows):
        return _rms(x_ref[0, rows], g1_ref[...]).astype(BF16)

    def stage_gate_up(xn):
        gate = jnp.dot(xn, wg_ref[...], preferred_element_type=F32)
        up = jnp.dot(xn, wu_ref[...], preferred_element_type=F32)
        return _gated(gate, up)

    def stage_down(rows, act):
        h = x_ref[0, rows] + 0.5 * jnp.dot(act, wd_ref[...], preferred_element_type=F32)
        h_ref[0, rows] = h
        return _rms(h, gm_ref[...]).astype(BF16)

    def stage_proj(hn):
        return jnp.dot(hn, win_ref[...], preferred_element_type=F32)

    def stage_out(rows, u):
        c, sa, sb = c_ref[rows], sa_ref[rows], sb_ref[rows]
        for t in range(Q_WIDTH // LANES):
            qt = _rope(u[:, t * LANES:(t + 1) * LANES], c, sa, sb)
            q_ref[0, rows, t * LANES:(t + 1) * LANES] = (qt * scale).astype(BF16)
        k = _rope(u[:, Q_WIDTH:Q_WIDTH + KV_WIDTH], c, sa, sb)
        kt_ref[0, :, rows] = k.T.astype(BF16)
        v_ref[0, rows] = u[:, Q_WIDTH + KV_WIDTH:Q_WIDTH + 2 * KV_WIDTH].astype(BF16)
        pc_ref[0, rows] = u[:, Q_WIDTH + 2 * KV_WIDTH:]

    for pair in range(tm // (2 * sub)):
        ra = slice((2 * pair) * sub, (2 * pair + 1) * sub)
        rb = slice((2 * pair + 1) * sub, (2 * pair + 2) * sub)
        xn_a, xn_b = stage_norm(ra), stage_norm(rb)
        act_a = stage_gate_up(xn_a)
        act_b = stage_gate_up(xn_b)
        hn_a = stage_down(ra, act_a)
        hn_b = stage_down(rb, act_b)
        u_a = stage_proj(hn_a)
        u_b = stage_proj(hn_b)
        stage_out(ra, u_a)
        stage_out(rb, u_b)


def _mix_ffn_kernel(sink_ref, h_ref, q_ref, ktp_ref, ktc_ref, ktn_ref,
                    vp_ref, vc_ref, vn_ref, pcp_ref, pcc_ref, pcn_ref,
                    bias_ref, eye_ref, band_ref, poolw_ref, pscale_ref, wout_hbm,
                    g2_ref, wg_hbm, wu_hbm, wd_hbm, gf_ref,
                    o_ref,
                    kt_buf, vab_buf, hl_buf, edge_buf, d_buf, mix_buf,
                    wout_ref, wg_ref, wu_ref, wd_ref, *, tm, seq):
    @pl.when((pl.program_id(0) == 0) & (pl.program_id(1) == 0))
    def _():
        _load_weights_bf16(((wout_hbm, wout_ref), (wg_hbm, wg_ref), (wu_hbm, wu_ref),
                            (wd_hbm, wd_ref)), CAST_ROWS_MIX)

    i = pl.program_id(1)
    _prep_keys(ktp_ref, ktc_ref, ktn_ref, kt_buf, tm=tm)
    _prep_values(vp_ref, vc_ref, vn_ref, vab_buf, tm=tm)
    _prep_pool(pcp_ref, pcc_ref, pcn_ref, hl_buf, edge_buf, i, tm=tm, seq=seq)
    mix_rows = functools.partial(
        _mix_rows, sink_ref, q_ref, bias_ref, eye_ref, band_ref, poolw_ref, pscale_ref,
        kt_buf, vab_buf, hl_buf, edge_buf, d_buf, mix_buf, i, tm=tm, seq=seq)

    sub = tm // 2
    sub_blocks = sub // QBLK
    halves = (slice(0, sub), slice(sub, tm))

    def stage_out_proj(rows):
        h = h_ref[0, rows] + jnp.dot(mix_buf[rows], wout_ref[...], preferred_element_type=F32)
        return h, _rms(h, g2_ref[...]).astype(BF16)

    def stage_gate_up(hn):
        gate = jnp.dot(hn, wg_ref[...], preferred_element_type=F32)
        up = jnp.dot(hn, wu_ref[...], preferred_element_type=F32)
        return _gated(gate, up)

    def stage_down(rows, h, act):
        h = h + 0.5 * jnp.dot(act, wd_ref[...], preferred_element_type=F32)
        o_ref[0, rows] = _rms(h, gf_ref[...])

    mix_rows(0, 2 * sub_blocks)
    hs = [stage_out_proj(rows) for rows in halves]
    acts = [stage_gate_up(hn) for _, hn in hs]
    for rows, (h, _), act in zip(halves, hs, acts):
        stage_down(rows, h, act)


def _prep_keys(ktp_ref, ktc_ref, ktn_ref, kt_buf, *, tm):
    kt_buf[:, 0:QBLK] = ktp_ref[0]
    kt_buf[:, QBLK:QBLK + tm] = ktc_ref[0]
    kt_buf[:, QBLK + tm:] = ktn_ref[0]


def _prep_values(vp_ref, vc_ref, vn_ref, vab_buf, *, tm):
    for lo, ref, n in ((0, vp_ref, QBLK), (QBLK, vc_ref, tm), (QBLK + tm, vn_ref, QBLK)):
        v = ref[0].astype(F32)
        vr = pltpu.roll(v, HEAD_DIM, 1)
        low = lax.broadcasted_iota(jnp.int32, v.shape, 1) < HEAD_DIM
        zero = jnp.zeros_like(v)
        vab_buf[0, 0, lo:lo + n] = jnp.where(low, v, zero).astype(BF16)
        vab_buf[0, 1, lo:lo + n] = jnp.where(low, zero, vr).astype(BF16)
        vab_buf[1, 0, lo:lo + n] = jnp.where(low, vr, zero).astype(BF16)
        vab_buf[1, 1, lo:lo + n] = jnp.where(low, zero, v).astype(BF16)


def _prep_pool(pcp_ref, pcc_ref, pcn_ref, hl_buf, edge_buf, i, *, tm, seq):
    n_tiles = seq // tm
    prev_halo = jnp.where(i > 0, pcp_ref[0], 0.0)
    next_halo = jnp.where(i < n_tiles - 1, pcn_ref[0], 0.0)
    zpad = jnp.zeros((POOL_MARGIN - POOL_HALO, POOL_WIDTH), F32)
    for r0, nr, u in ((0, POOL_MARGIN, jnp.concatenate([zpad, prev_halo], axis=0)),
                      (POOL_MARGIN, tm, pcc_ref[0]),
                      (POOL_MARGIN + tm, POOL_MARGIN, jnp.concatenate([next_halo, zpad], axis=0))):
        hi = u.astype(BF16)
        lo = (u - hi.astype(F32)).astype(BF16)
        for g in range(len(POOL_WINDOWS)):
            lanes = slice(g * GROUP_W, (g + 1) * GROUP_W)
            hl_buf[r0:r0 + nr, 2 * g * GROUP_W:(2 * g + 1) * GROUP_W] = hi[:, lanes]
            hl_buf[r0:r0 + nr, (2 * g + 1) * GROUP_W:(2 * g + 2) * GROUP_W] = lo[:, lanes]

    edge_buf[0, 0:POOL_HALO] = prev_halo
    edge_buf[0, POOL_HALO:] = pcc_ref[0, 0:2 * POOL_HALO]
    edge_buf[1, 0:2 * POOL_HALO] = pcc_ref[0, tm - 2 * POOL_HALO:tm]
    edge_buf[1, 2 * POOL_HALO:] = next_halo


def _mix_rows(sink_ref, q_ref, bias_ref, eye_ref, band_ref, poolw_ref, pscale_ref,
              kt_buf, vab_buf, hl_buf, edge_buf, d_buf, mix_buf, i, j0, j1, *, tm, seq):
    n_tiles = seq // tm
    nq = tm // QBLK
    nb = seq // QBLK

    def band_block(n):
        for g in range(len(POOL_WINDOWS)):
            win = hl_buf[n * QBLK:n * QBLK + 2 * QBLK, 2 * g * GROUP_W:(2 * g + 2) * GROUP_W]
            dd = jnp.dot(band_ref[g], win, preferred_element_type=F32)
            d_buf[n * QBLK:(n + 1) * QBLK, g * GROUP_W:(g + 1) * GROUP_W] = (
                dd[:, :GROUP_W] + dd[:, GROUP_W:])

    low_out = lax.broadcasted_iota(jnp.int32, (QBLK, LANES), 1) < HEAD_DIM
    eye = eye_ref[...]

    def scores(j, c):
        blk = i * nq + j
        variant = jnp.where(blk == 0, 0, jnp.where(blk == nb - 1, 2, 1))
        bias_t = bias_ref[variant]
        rows = slice(j * QBLK, (j + 1) * QBLK)
        keys = slice(j * QBLK, j * QBLK + KEYS)
        lhs = jnp.concatenate(
            [jnp.concatenate([q_ref[0, rows, (2 * c) * LANES:(2 * c + 1) * LANES], eye], axis=1),
             jnp.concatenate([q_ref[0, rows, (2 * c + 1) * LANES:(2 * c + 2) * LANES], eye],
                             axis=1)], axis=0)
        kc = kt_buf[c * HEAD_DIM:(c + 1) * HEAD_DIM, keys]
        z = jnp.zeros_like(kc)
        rhs = jnp.concatenate(
            [jnp.concatenate([kc, z], axis=1), jnp.concatenate([z, kc], axis=1), bias_t],
            axis=0)
        return jnp.dot(lhs, rhs, preferred_element_type=F32)

    def softmax(c, s):
        p_rows, inv_den = [], []
        for r in range(2):
            p_cols = []
            for par in range(2):
                sq = s[r * QBLK:(r + 1) * QBLK, par * KEYS:(par + 1) * KEYS]
                sink = sink_ref[4 * c + 2 * r + par] * LOG2E
                m = jnp.maximum(jnp.max(sq, axis=-1, keepdims=True), sink)
                p = jnp.exp2(sq - m)
                p_cols.append(p.astype(BF16))
                inv_den.append(1.0 / (jnp.sum(p, axis=-1, keepdims=True) + jnp.exp2(sink - m)))
            p_rows.append(jnp.concatenate(p_cols, axis=1))
        return jnp.concatenate(p_rows, axis=0), inv_den

    def weighted_values(j, c, pmat, inv_den):
        rows = slice(j * QBLK, (j + 1) * QBLK)
        keys = slice(j * QBLK, j * QBLK + KEYS)
        vrhs = jnp.concatenate([vab_buf[c, 0, keys], vab_buf[c, 1, keys]], axis=0)
        o = jnp.dot(pmat, vrhs, preferred_element_type=F32)
        for r in range(2):
            inv = jnp.where(low_out, inv_den[2 * r], inv_den[2 * r + 1])
            t = 2 * c + r
            mix_buf[rows, t * LANES:(t + 1) * LANES] = (
                o[r * QBLK:(r + 1) * QBLK] * inv).astype(BF16)

    units = [(j, c) for j in range(j0, j1) for c in range(N_KV)]
    for n in range(j0, j1):
        band_block(n)
    s_next = scores(*units[0])
    for n, (j, c) in enumerate(units):
        s_cur = s_next
        if n + 1 < len(units):
            s_next = scores(*units[n + 1])
        weighted_values(j, c, *softmax(c, s_cur))

    for e, (row0, clipped) in enumerate(((0, i == 0), (tm - POOL_HALO, i == n_tiles - 1))):
        if not j0 * QBLK <= row0 < j1 * QBLK:
            continue
        tpos = i * tm + row0 + lax.broadcasted_iota(jnp.int32, (POOL_HALO, GROUP_W), 0)
        base = POOL_HALO
        for g, w in enumerate(POOL_WINDOWS):
            half = w // 2
            lanes = slice(g * GROUP_W, (g + 1) * GROUP_W)

            def wsum(lo, hi):
                acc = edge_buf[e, base + lo:base + lo + POOL_HALO, lanes]
                for k in range(lo + 1, hi + 1):
                    acc = acc + edge_buf[e, base + k:base + k + POOL_HALO, lanes]
                return acc

            def count(lo, hi):
                a = jnp.clip(tpos + lo, 0, seq)
                b = jnp.clip(tpos + hi + 1, 0, seq)
                return (b - a).astype(F32)

            mean = 0.5 * (wsum(-half, half - 1) / count(-half, half - 1)
                          + wsum(-half + 1, half) / count(-half + 1, half))
            fixed = mean - edge_buf[e, base:base + POOL_HALO, lanes]
            d_buf[row0:row0 + POOL_HALO, lanes] = jnp.where(
                clipped, fixed, d_buf[row0:row0 + POOL_HALO, lanes])

    rows = slice(j0 * QBLK, j1 * QBLK)
    for g in range(len(POOL_WINDOWS)):
        lanes = slice(g * GROUP_W, (g + 1) * GROUP_W)
        y = jnp.dot(d_buf[rows, lanes].astype(BF16), poolw_ref[g], preferred_element_type=F32)
        mix_buf[rows, Q_WIDTH + g * GROUP_W:Q_WIDTH + (g + 1) * GROUP_W] = (
            y * pscale_ref[:, lanes]).astype(BF16)


def _rope_tables(seq):
    f32 = np.float32
    inv_freq = f32(ROPE_THETA) ** (-np.arange(0, ROTARY_DIM, 2, dtype=f32) / f32(ROTARY_DIM))
    ang = np.arange(seq, dtype=f32)[:, None] * inv_freq[None, :]
    cos, sin = np.cos(ang.astype(np.float64)).astype(f32), np.sin(ang.astype(np.float64)).astype(f32)
    half = ROTARY_DIM // 2
    rest = HEAD_DIM - ROTARY_DIM
    c = np.concatenate([cos, cos, np.ones((seq, rest), f32)], axis=1)
    sa = np.concatenate([-sin, np.zeros((seq, HEAD_DIM - half), f32)], axis=1)
    sb = np.concatenate([np.zeros((seq, half), f32), sin, np.zeros((seq, rest), f32)], axis=1)
    rep = LANES // HEAD_DIM
    return tuple(jnp.asarray(np.tile(t, (1, rep))) for t in (c, sa, sb))


def _band_bias(seq):
    r = np.arange(QBLK)[:, None]
    s = np.arange(KEYS)[None, :]
    band = (s - r >= 0) & (s - r <= 2 * WINDOW)
    first = band & (s >= QBLK)
    last = band & (s < 2 * QBLK)
    out = np.stack([np.tile(np.where(m, 0.0, NEG), (1, 2)) for m in (first, band, last)])
    return jnp.asarray(out, dtype=F32).astype(BF16)


def _pool_band():
    out = np.zeros((len(POOL_WINDOWS), QBLK, 2 * QBLK), np.float32)
    r = np.arange(QBLK)
    for g, w in enumerate(POOL_WINDOWS):
        half = w // 2
        for k in range(-half, half + 1):
            out[g, r, r + POOL_MARGIN + k] = (0.5 if abs(k) == half else 1.0) / w
        out[g, r, r + POOL_MARGIN] -= 1.0
    return jnp.asarray(out).astype(BF16)


def _const_spec(shape):
    nd = len(shape)
    return pl.BlockSpec(shape, lambda *_: (0,) * nd, pipeline_mode=pl.Buffered(1))


def kernel(x, ffn1_norm, ffn1_w_gate, ffn1_w_up, ffn1_w_down, mix_norm, w_in, sink_logits,
           pool_w, pool_scale, w_out, ffn2_norm, ffn2_w_gate, ffn2_w_up, ffn2_w_down, final_norm):
    B, S, D = x.shape
    assert D == D_MODEL and S % TM == 0 and TM % QBLK == 0 and ffn1_norm.shape[0] == 1
    assert S % TM_IN == 0 and TM_IN % SUB_ROWS == 0
    tm = TM
    grid = (B, S // tm)
    in_width = w_in.shape[-1]
    params = pltpu.CompilerParams(dimension_semantics=("arbitrary", "arbitrary"),
                                  vmem_limit_bytes=VMEM_LIMIT)

    c_tab, sa_tab, sb_tab = _rope_tables(S)
    row = lambda g: g.reshape(1, -1).astype(F32)
    tile_spec = lambda width, rows=tm: pl.BlockSpec((1, rows, width), lambda b, i: (b, i, 0))
    tm_in = TM_IN
    tab_spec = pl.BlockSpec((tm_in, LANES), lambda b, i: (i, 0))
    hbm_spec = pl.BlockSpec(memory_space=pl.ANY)

    h1, q, kt, v, pc = pl.pallas_call(
        _ffn_in_kernel,
        grid=(B, S // tm_in),
        in_specs=[
            tile_spec(D, tm_in),
            _const_spec((1, D)),
            hbm_spec, hbm_spec, hbm_spec,
            _const_spec((1, D)),
            hbm_spec,
            tab_spec, tab_spec, tab_spec,
        ],
        out_specs=[
            tile_spec(D, tm_in),
            tile_spec(Q_WIDTH, tm_in),
            pl.BlockSpec((1, KV_WIDTH, tm_in), lambda b, i: (b, 0, i)),
            tile_spec(KV_WIDTH, tm_in),
            tile_spec(POOL_WIDTH, tm_in),
        ],
        out_shape=[
            jax.ShapeDtypeStruct((B, S, D), F32),
            jax.ShapeDtypeStruct((B, S, Q_WIDTH), BF16),
            jax.ShapeDtypeStruct((B, KV_WIDTH, S), BF16),
            jax.ShapeDtypeStruct((B, S, KV_WIDTH), BF16),
            jax.ShapeDtypeStruct((B, S, POOL_WIDTH), F32),
        ],
        scratch_shapes=[
            pltpu.VMEM((D, D_FF), BF16), pltpu.VMEM((D, D_FF), BF16), pltpu.VMEM((D_FF, D), BF16),
            pltpu.VMEM((D, in_width), BF16),
        ],
        compiler_params=params,
        name="ffn1_inproj",
    )(x, row(ffn1_norm[0]), ffn1_w_gate[0], ffn1_w_up[0], ffn1_w_down[0], row(mix_norm[0]),
      w_in[0], c_tab, sa_tab, sb_tab)

    qb = tm // QBLK
    pb = tm // POOL_HALO
    n_qb = S // QBLK
    n_pb = S // POOL_HALO
    mix_spec = pl.BlockSpec
    prev_q = lambda i: jnp.maximum(i * qb - 1, 0)
    next_q = lambda i: jnp.minimum((i + 1) * qb, n_qb - 1)
    prev_p = lambda i: jnp.maximum(i * pb - 1, 0)
    next_p = lambda i: jnp.minimum((i + 1) * pb, n_pb - 1)

    out = pl.pallas_call(
        functools.partial(_mix_ffn_kernel, tm=tm, seq=S),
        grid=grid,
        in_specs=[
            pl.BlockSpec(memory_space=pltpu.SMEM),
            tile_spec(D),
            mix_spec((1, tm, Q_WIDTH), lambda b, i: (b, i, 0)),
            mix_spec((1, KV_WIDTH, QBLK), lambda b, i: (b, 0, prev_q(i))),
            mix_spec((1, KV_WIDTH, tm), lambda b, i: (b, 0, i)),
            mix_spec((1, KV_WIDTH, QBLK), lambda b, i: (b, 0, next_q(i))),
            mix_spec((1, QBLK, KV_WIDTH), lambda b, i: (b, prev_q(i), 0)),
            mix_spec((1, tm, KV_WIDTH), lambda b, i: (b, i, 0)),
            mix_spec((1, QBLK, KV_WIDTH), lambda b, i: (b, next_q(i), 0)),
            mix_spec((1, POOL_HALO, POOL_WIDTH), lambda b, i: (b, prev_p(i), 0)),
            mix_spec((1, tm, POOL_WIDTH), lambda b, i: (b, i, 0)),
            mix_spec((1, POOL_HALO, POOL_WIDTH), lambda b, i: (b, next_p(i), 0)),
            _const_spec((3, QBLK, 2 * KEYS)),
            _const_spec((QBLK, QBLK)),
            _const_spec((len(POOL_WINDOWS), QBLK, 2 * QBLK)),
            _const_spec((len(POOL_WINDOWS), GROUP_W, GROUP_W)),
            _const_spec((1, POOL_WIDTH)),
            hbm_spec,
            _const_spec((1, D)),
            hbm_spec, hbm_spec, hbm_spec,
            _const_spec((1, D)),
        ],
        out_specs=tile_spec(D),
        out_shape=jax.ShapeDtypeStruct((B, S, D), x.dtype),
        scratch_shapes=[
            pltpu.VMEM((KV_WIDTH, tm + 2 * QBLK), BF16),
            pltpu.VMEM((N_KV, 2, tm + 2 * QBLK, LANES), BF16),
            pltpu.VMEM((tm + 2 * POOL_MARGIN, 2 * POOL_WIDTH), BF16),
            pltpu.VMEM((2, 3 * POOL_HALO, POOL_WIDTH), F32),
            pltpu.VMEM((tm, POOL_WIDTH), F32),
            pltpu.VMEM((tm, D), BF16),
            pltpu.VMEM((D, D), BF16),
            pltpu.VMEM((D, D_FF), BF16), pltpu.VMEM((D, D_FF), BF16), pltpu.VMEM((D_FF, D), BF16),
        ],
        compiler_params=params,
        name="mix_ffn2",
    )(sink_logits[0].astype(F32), h1, q, kt, kt, kt, v, v, v, pc, pc, pc,
      _band_bias(S), jnp.asarray(np.eye(QBLK, dtype=np.float32)).astype(BF16), _pool_band(),
      pool_w[0].astype(BF16),
      row(pool_scale[0]), w_out[0],
      row(ffn2_norm[0]), ffn2_w_gate[0], ffn2_w_up[0], ffn2_w_down[0], row(final_norm))
    return out
```

```python
import functools

import numpy as np
import jax
import jax.numpy as jnp
from jax import lax
from jax.experimental import pallas as pl
from jax.experimental.pallas import tpu as pltpu

D_MODEL = 1024
HEAD_DIM = 64
N_HEADS = 8
N_KV = 2
Q_WIDTH = N_HEADS * HEAD_DIM
KV_WIDTH = N_KV * HEAD_DIM
WINDOW = 128
QBLK = 128
KEYS = 3 * QBLK
ROPE_THETA = 500000.0
ROTARY_DIM = HEAD_DIM // 4
POOL_WINDOWS = (2, 4, 8, 16)
POOL_HALO = 8
POOL_MARGIN = 64
POOL_WIDTH = 512
GROUP_W = 128
D_FF = 2816
EPS = 1e-6
LANES = 128
NEG = -1e30
LOG2E = 1.4426950408889634

TM = 512
TM_IN = 1024
SUB_ROWS = 256
CAST_ROWS_IN = 128
CAST_ROWS_MIX = 256
CAST_SLOTS = 3
VMEM_LIMIT = 58 * 1024 * 1024

F32 = jnp.float32
BF16 = jnp.bfloat16


def _rms(x, g):
    ms = jnp.mean(x * x, axis=-1, keepdims=True)
    return x * lax.rsqrt(ms + EPS) * g


def _gated(gate, up):
    return (gate * jax.nn.sigmoid(gate) * up).astype(BF16)


def _load_weight_bf16(src, dst, stage, sem):
    n_rows, width = src.shape
    chunk = stage.shape[1]
    n_chunks = n_rows // chunk
    ahead = CAST_SLOTS - 1

    def chunk_copy(k, slot):
        return pltpu.make_async_copy(src.at[pl.ds(k * chunk, chunk)],
                                     stage.at[slot, :, pl.ds(0, width)], sem.at[slot])

    for k in range(min(ahead, n_chunks)):
        chunk_copy(k, k).start()

    def body(k, carry):
        slot = lax.rem(k, CAST_SLOTS)

        @pl.when(k + ahead < n_chunks)
        def _():
            chunk_copy(k + ahead, lax.rem(k + ahead, CAST_SLOTS)).start()

        chunk_copy(k, slot).wait()
        r0 = pl.multiple_of(k * chunk, chunk)
        dst[pl.ds(r0, chunk), :] = stage[slot, :, 0:width].astype(BF16)
        return carry

    lax.fori_loop(0, n_chunks, body, 0)


def _load_weights_bf16(pairs, chunk_rows):
    def scoped(stage, sem):
        for src, dst in pairs:
            _load_weight_bf16(src, dst, stage, sem)

    pl.run_scoped(scoped, pltpu.VMEM((CAST_SLOTS, chunk_rows, D_FF), F32),
                  pltpu.SemaphoreType.DMA((CAST_SLOTS,)))


def _rope(t, c, sa, sb):
    t_plus = pltpu.roll(t, LANES - ROTARY_DIM // 2, 1)
    t_minus = pltpu.roll(t, ROTARY_DIM // 2, 1)
    return t * c + t_plus * sa + t_minus * sb


def _ffn_in_kernel(x_ref, g1_ref, wg_hbm, wu_hbm, wd_hbm, gm_ref, win_hbm,
                   c_ref, sa_ref, sb_ref,
                   h_ref, q_ref, kt_ref, v_ref, pc_ref,
                   wg_ref, wu_ref, wd_ref, win_ref):
    @pl.when((pl.program_id(0) == 0) & (pl.program_id(1) == 0))
    def _():
        _load_weights_bf16(((wg_hbm, wg_ref), (wu_hbm, wu_ref), (wd_hbm, wd_ref),
                            (win_hbm, win_ref)), CAST_ROWS_IN)

    tm = x_ref.shape[1]
    sub = SUB_ROWS
    scale = HEAD_DIM ** -0.5 * LOG2E
    def stage_norm(rows):
        return _rms(x_ref[0, rows], g1_ref[...]).astype(BF16)

    def stage_gate_up(xn):
        gate = jnp.dot(xn, wg_ref[...], preferred_element_type=F32)
        up = jnp.dot(xn, wu_ref[...], preferred_element_type=F32)
        return _gated(gate, up)

    def stage_down(rows, act):
        h = x_ref[0, rows] + 0.5 * jnp.dot(act, wd_ref[...], preferred_element_type=F32)
        h_ref[0, rows] = h
        return _rms(h, gm_ref[...]).astype(BF16)

    def stage_proj(hn):
        return jnp.dot(hn, win_ref[...], preferred_element_type=F32)

    def stage_out(rows, u):
        c, sa, sb = c_ref[rows], sa_ref[rows], sb_ref[rows]
        for t in range(Q_WIDTH // LANES):
            qt = _rope(u[:, t * LANES:(t + 1) * LANES], c, sa, sb)
            q_ref[0, rows, t * LANES:(t + 1) * LANES] = (qt * scale).astype(BF16)
        k = _rope(u[:, Q_WIDTH:Q_WIDTH + KV_WIDTH], c, sa, sb)
        kt_ref[0, :, rows] = k.T.astype(BF16)
        v_ref[0, rows] = u[:, Q_WIDTH + KV_WIDTH:Q_WIDTH + 2 * KV_WIDTH].astype(BF16)
        pc_ref[0, rows] = u[:, Q_WIDTH + 2 * KV_WIDTH:]

    for pair in range(tm // (2 * sub)):
        ra = slice((2 * pair) * sub, (2 * pair + 1) * sub)
        rb = slice((2 * pair + 1) * sub, (2 * pair + 2) * sub)
        xn_a, xn_b = stage_norm(ra), stage_norm(rb)
        act_a = stage_gate_up(xn_a)
        act_b = stage_gate_up(xn_b)
        hn_a = stage_down(ra, act_a)
        hn_b = stage_down(rb, act_b)
        u_a = stage_proj(hn_a)
        u_b = stage_proj(hn_b)
        stage_out(ra, u_a)
        stage_out(rb, u_b)


def _mix_ffn_kernel(sink_ref, h_ref, q_ref, ktp_ref, ktc_ref, ktn_ref,
                    vp_ref, vc_ref, vn_ref, pcp_ref, pcc_ref, pcn_ref,
                    bias_ref, eye_ref, band_ref, poolw_ref, pscale_ref, wout_hbm,
                    g2_ref, wg_hbm, wu_hbm, wd_hbm, gf_ref,
                    o_ref,
                    kt_buf, vab_buf, hl_buf, edge_buf, d_buf, mix_buf,
                    wout_ref, wg_ref, wu_ref, wd_ref, *, tm, seq, n_tiles_total):
    s = pl.program_id(0)
    n_tiles = seq // tm
    halves = (slice(0, tm // 2), slice(tm // 2, tm))

    def mix_pieces(slot, i):
        _prep_keys(ktp_ref, ktc_ref, ktn_ref, kt_buf, tm=tm)
        _prep_values(vp_ref, vc_ref, vn_ref, vab_buf, tm=tm)
        _prep_pool(pcp_ref, pcc_ref, pcn_ref, hl_buf, edge_buf, i, tm=tm, seq=seq)
        yield
        yield from _mix_rows(sink_ref, q_ref, bias_ref, eye_ref, band_ref, poolw_ref, pscale_ref,
                             kt_buf, vab_buf, hl_buf, edge_buf, d_buf, mix_buf.at[slot], i,
                             0, tm // QBLK, tm=tm, seq=seq)

    def ffn_pieces(slot):
        mix_in = mix_buf.at[slot]
        hs, acts = [], []
        for rows in halves:
            h = h_ref[0, rows] + jnp.dot(mix_in[rows], wout_ref[...], preferred_element_type=F32)
            hs.append((h, _rms(h, g2_ref[...]).astype(BF16)))
            yield
        for _, hn in hs:
            gate = jnp.dot(hn, wg_ref[...], preferred_element_type=F32)
            up = jnp.dot(hn, wu_ref[...], preferred_element_type=F32)
            acts.append(_gated(gate, up))
            yield
        for rows, (h, _), act in zip(halves, hs, acts):
            h = h + 0.5 * jnp.dot(act, wd_ref[...], preferred_element_type=F32)
            o_ref[0, rows] = _rms(h, gf_ref[...])
            yield

    def run(gen):
        for _ in gen:
            pass

    def alternate(ffn, mix, n_ffn, n_mix):
        done = 0
        for k in range(n_ffn):
            while done * n_ffn < (k + 1) * n_mix and next(mix, StopIteration) is not StopIteration:
                done += 1
            next(ffn, None)
        run(mix)
        run(ffn)

    n_mix_pieces = (tm // QBLK) * N_KV + 4
    n_ffn_pieces = 6

    @pl.when(s == 0)
    def _():
        _load_weights_bf16(((wout_hbm, wout_ref), (wg_hbm, wg_ref), (wu_hbm, wu_ref),
                            (wd_hbm, wd_ref)), CAST_ROWS_MIX)
        run(mix_pieces(0, 0))

    @pl.when((s > 0) & (s < n_tiles_total))
    def _():
        alternate(ffn_pieces(lax.rem(s - 1, 2)), mix_pieces(lax.rem(s, 2), lax.rem(s, n_tiles)),
                  n_ffn_pieces, n_mix_pieces)

    @pl.when(s == n_tiles_total)
    def _():
        run(ffn_pieces((n_tiles_total - 1) % 2))


def _prep_keys(ktp_ref, ktc_ref, ktn_ref, kt_buf, *, tm):
    kt_buf[:, 0:QBLK] = ktp_ref[0]
    kt_buf[:, QBLK:QBLK + tm] = ktc_ref[0]
    kt_buf[:, QBLK + tm:] = ktn_ref[0]


def _prep_values(vp_ref, vc_ref, vn_ref, vab_buf, *, tm):
    for lo, ref, n in ((0, vp_ref, QBLK), (QBLK, vc_ref, tm), (QBLK + tm, vn_ref, QBLK)):
        v = ref[0].astype(F32)
        vr = pltpu.roll(v, HEAD_DIM, 1)
        low = lax.broadcasted_iota(jnp.int32, v.shape, 1) < HEAD_DIM
        zero = jnp.zeros_like(v)
        vab_buf[0, 0, lo:lo + n] = jnp.where(low, v, zero).astype(BF16)
        vab_buf[0, 1, lo:lo + n] = jnp.where(low, zero, vr).astype(BF16)
        vab_buf[1, 0, lo:lo + n] = jnp.where(low, vr, zero).astype(BF16)
        vab_buf[1, 1, lo:lo + n] = jnp.where(low, zero, v).astype(BF16)


def _prep_pool(pcp_ref, pcc_ref, pcn_ref, hl_buf, edge_buf, i, *, tm, seq):
    n_tiles = seq // tm
    prev_halo = jnp.where(i > 0, pcp_ref[0], 0.0)
    next_halo = jnp.where(i < n_tiles - 1, pcn_ref[0], 0.0)
    zpad = jnp.zeros((POOL_MARGIN - POOL_HALO, POOL_WIDTH), F32)
    for r0, nr, u in ((0, POOL_MARGIN, jnp.concatenate([zpad, prev_halo], axis=0)),
                      (POOL_MARGIN, tm, pcc_ref[0]),
                      (POOL_MARGIN + tm, POOL_MARGIN, jnp.concatenate([next_halo, zpad], axis=0))):
        hi = u.astype(BF16)
        lo = (u - hi.astype(F32)).astype(BF16)
        for g in range(len(POOL_WINDOWS)):
            lanes = slice(g * GROUP_W, (g + 1) * GROUP_W)
            hl_buf[r0:r0 + nr, 2 * g * GROUP_W:(2 * g + 1) * GROUP_W] = hi[:, lanes]
            hl_buf[r0:r0 + nr, (2 * g + 1) * GROUP_W:(2 * g + 2) * GROUP_W] = lo[:, lanes]

    edge_buf[0, 0:POOL_HALO] = prev_halo
    edge_buf[0, POOL_HALO:] = pcc_ref[0, 0:2 * POOL_HALO]
    edge_buf[1, 0:2 * POOL_HALO] = pcc_ref[0, tm - 2 * POOL_HALO:tm]
    edge_buf[1, 2 * POOL_HALO:] = next_halo


def _mix_rows(sink_ref, q_ref, bias_ref, eye_ref, band_ref, poolw_ref, pscale_ref,
              kt_buf, vab_buf, hl_buf, edge_buf, d_buf, mix_buf, i, j0, j1, *, tm, seq):
    n_tiles = seq // tm
    nq = tm // QBLK
    nb = seq // QBLK

    def band_block(n):
        for g in range(len(POOL_WINDOWS)):
            win = hl_buf[n * QBLK:n * QBLK + 2 * QBLK, 2 * g * GROUP_W:(2 * g + 2) * GROUP_W]
            dd = jnp.dot(band_ref[g], win, preferred_element_type=F32)
            d_buf[n * QBLK:(n + 1) * QBLK, g * GROUP_W:(g + 1) * GROUP_W] = (
                dd[:, :GROUP_W] + dd[:, GROUP_W:])

    low_out = lax.broadcasted_iota(jnp.int32, (QBLK, LANES), 1) < HEAD_DIM
    eye = eye_ref[...]

    def scores(j, c):
        blk = i * nq + j
        variant = jnp.where(blk == 0, 0, jnp.where(blk == nb - 1, 2, 1))
        bias_t = bias_ref[variant]
        rows = slice(j * QBLK, (j + 1) * QBLK)
        keys = slice(j * QBLK, j * QBLK + KEYS)
        lhs = jnp.concatenate(
            [jnp.concatenate([q_ref[0, rows, (2 * c) * LANES:(2 * c + 1) * LANES], eye], axis=1),
             jnp.concatenate([q_ref[0, rows, (2 * c + 1) * LANES:(2 * c + 2) * LANES], eye],
                             axis=1)], axis=0)
        kc = kt_buf[c * HEAD_DIM:(c + 1) * HEAD_DIM, keys]
        z = jnp.zeros_like(kc)
        rhs = jnp.concatenate(
            [jnp.concatenate([kc, z], axis=1), jnp.concatenate([z, kc], axis=1), bias_t],
            axis=0)
        return jnp.dot(lhs, rhs, preferred_element_type=F32)

    def softmax(c, s):
        p_rows, inv_den = [], []
        for r in range(2):
            p_cols = []
            for par in range(2):
                sq = s[r * QBLK:(r + 1) * QBLK, par * KEYS:(par + 1) * KEYS]
                sink = sink_ref[4 * c + 2 * r + par] * LOG2E
                m = jnp.maximum(jnp.max(sq, axis=-1, keepdims=True), sink)
                p = jnp.exp2(sq - m)
                p_cols.append(p.astype(BF16))
                inv_den.append(1.0 / (jnp.sum(p, axis=-1, keepdims=True) + jnp.exp2(sink - m)))
            p_rows.append(jnp.concatenate(p_cols, axis=1))
        return jnp.concatenate(p_rows, axis=0), inv_den

    def weighted_values(j, c, pmat, inv_den):
        rows = slice(j * QBLK, (j + 1) * QBLK)
        keys = slice(j * QBLK, j * QBLK + KEYS)
        vrhs = jnp.concatenate([vab_buf[c, 0, keys], vab_buf[c, 1, keys]], axis=0)
        o = jnp.dot(pmat, vrhs, preferred_element_type=F32)
        for r in range(2):
            inv = jnp.where(low_out, inv_den[2 * r], inv_den[2 * r + 1])
            t = 2 * c + r
            mix_buf[rows, t * LANES:(t + 1) * LANES] = (
                o[r * QBLK:(r + 1) * QBLK] * inv).astype(BF16)

    units = [(j, c) for j in range(j0, j1) for c in range(N_KV)]
    for n in range(j0, j1):
        band_block(n)
    yield
    s_next = scores(*units[0])
    yield
    for n, (j, c) in enumerate(units):
        s_cur = s_next
        if n + 1 < len(units):
            s_next = scores(*units[n + 1])
        yield
        weighted_values(j, c, *softmax(c, s_cur))
    yield

    for e, (row0, clipped) in enumerate(((0, i == 0), (tm - POOL_HALO, i == n_tiles - 1))):
        if not j0 * QBLK <= row0 < j1 * QBLK:
            continue
        tpos = i * tm + row0 + lax.broadcasted_iota(jnp.int32, (POOL_HALO, GROUP_W), 0)
        base = POOL_HALO
        for g, w in enumerate(POOL_WINDOWS):
            half = w // 2
            lanes = slice(g * GROUP_W, (g + 1) * GROUP_W)

            def wsum(lo, hi):
                acc = edge_buf[e, base + lo:base + lo + POOL_HALO, lanes]
                for k in range(lo + 1, hi + 1):
                    acc = acc + edge_buf[e, base + k:base + k + POOL_HALO, lanes]
                return acc

            def count(lo, hi):
                a = jnp.clip(tpos + lo, 0, seq)
                b = jnp.clip(tpos + hi + 1, 0, seq)
                return (b - a).astype(F32)

            mean = 0.5 * (wsum(-half, half - 1) / count(-half, half - 1)
                          + wsum(-half + 1, half) / count(-half + 1, half))
            fixed = mean - edge_buf[e, base:base + POOL_HALO, lanes]
            d_buf[row0:row0 + POOL_HALO, lanes] = jnp.where(
                clipped, fixed, d_buf[row0:row0 + POOL_HALO, lanes])

    rows = slice(j0 * QBLK, j1 * QBLK)
    for g in range(len(POOL_WINDOWS)):
        lanes = slice(g * GROUP_W, (g + 1) * GROUP_W)
        y = jnp.dot(d_buf[rows, lanes].astype(BF16), poolw_ref[g], preferred_element_type=F32)
        mix_buf[rows, Q_WIDTH + g * GROUP_W:Q_WIDTH + (g + 1) * GROUP_W] = (
            y * pscale_ref[:, lanes]).astype(BF16)


def _rope_tables(seq):
    f32 = np.float32
    inv_freq = f32(ROPE_THETA) ** (-np.arange(0, ROTARY_DIM, 2, dtype=f32) / f32(ROTARY_DIM))
    ang = np.arange(seq, dtype=f32)[:, None] * inv_freq[None, :]
    cos, sin = np.cos(ang.astype(np.float64)).astype(f32), np.sin(ang.astype(np.float64)).astype(f32)
    half = ROTARY_DIM // 2
    rest = HEAD_DIM - ROTARY_DIM
    c = np.concatenate([cos, cos, np.ones((seq, rest), f32)], axis=1)
    sa = np.concatenate([-sin, np.zeros((seq, HEAD_DIM - half), f32)], axis=1)
    sb = np.concatenate([np.zeros((seq, half), f32), sin, np.zeros((seq, rest), f32)], axis=1)
    rep = LANES // HEAD_DIM
    return tuple(jnp.asarray(np.tile(t, (1, rep))) for t in (c, sa, sb))


def _band_bias(seq):
    r = np.arange(QBLK)[:, None]
    s = np.arange(KEYS)[None, :]
    band = (s - r >= 0) & (s - r <= 2 * WINDOW)
    first = band & (s >= QBLK)
    last = band & (s < 2 * QBLK)
    out = np.stack([np.tile(np.where(m, 0.0, NEG), (1, 2)) for m in (first, band, last)])
    return jnp.asarray(out, dtype=F32).astype(BF16)


def _pool_band():
    out = np.zeros((len(POOL_WINDOWS), QBLK, 2 * QBLK), np.float32)
    r = np.arange(QBLK)
    for g, w in enumerate(POOL_WINDOWS):
        half = w // 2
        for k in range(-half, half + 1):
            out[g, r, r + POOL_MARGIN + k] = (0.5 if abs(k) == half else 1.0) / w
        out[g, r, r + POOL_MARGIN] -= 1.0
    return jnp.asarray(out).astype(BF16)


def _const_spec(shape):
    nd = len(shape)
    return pl.BlockSpec(shape, lambda *_: (0,) * nd, pipeline_mode=pl.Buffered(1))


def kernel(x, ffn1_norm, ffn1_w_gate, ffn1_w_up, ffn1_w_down, mix_norm, w_in, sink_logits,
           pool_w, pool_scale, w_out, ffn2_norm, ffn2_w_gate, ffn2_w_up, ffn2_w_down, final_norm):
    B, S, D = x.shape
    assert D == D_MODEL and S % TM == 0 and TM % QBLK == 0 and ffn1_norm.shape[0] == 1
    assert S % TM_IN == 0 and TM_IN % SUB_ROWS == 0
    tm = TM
    grid = (B, S // tm)
    in_width = w_in.shape[-1]
    params = pltpu.CompilerParams(dimension_semantics=("arbitrary", "arbitrary"),
                                  vmem_limit_bytes=VMEM_LIMIT)

    c_tab, sa_tab, sb_tab = _rope_tables(S)
    row = lambda g: g.reshape(1, -1).astype(F32)
    tile_spec = lambda width, rows=tm: pl.BlockSpec((1, rows, width), lambda b, i: (b, i, 0))
    tm_in = TM_IN
    tab_spec = pl.BlockSpec((tm_in, LANES), lambda b, i: (i, 0))
    hbm_spec = pl.BlockSpec(memory_space=pl.ANY)

    h1, q, kt, v, pc = pl.pallas_call(
        _ffn_in_kernel,
        grid=(B, S // tm_in),
        in_specs=[
            tile_spec(D, tm_in),
            _const_spec((1, D)),
            hbm_spec, hbm_spec, hbm_spec,
            _const_spec((1, D)),
            hbm_spec,
            tab_spec, tab_spec, tab_spec,
        ],
        out_specs=[
            tile_spec(D, tm_in),
            tile_spec(Q_WIDTH, tm_in),
            pl.BlockSpec((1, KV_WIDTH, tm_in), lambda b, i: (b, 0, i)),
            tile_spec(KV_WIDTH, tm_in),
            tile_spec(POOL_WIDTH, tm_in),
        ],
        out_shape=[
            jax.ShapeDtypeStruct((B, S, D), F32),
            jax.ShapeDtypeStruct((B, S, Q_WIDTH), BF16),
            jax.ShapeDtypeStruct((B, KV_WIDTH, S), BF16),
            jax.ShapeDtypeStruct((B, S, KV_WIDTH), BF16),
            jax.ShapeDtypeStruct((B, S, POOL_WIDTH), F32),
        ],
        scratch_shapes=[
            pltpu.VMEM((D, D_FF), BF16), pltpu.VMEM((D, D_FF), BF16), pltpu.VMEM((D_FF, D), BF16),
            pltpu.VMEM((D, in_width), BF16),
        ],
        compiler_params=params,
        name="ffn1_inproj",
    )(x, row(ffn1_norm[0]), ffn1_w_gate[0], ffn1_w_up[0], ffn1_w_down[0], row(mix_norm[0]),
      w_in[0], c_tab, sa_tab, sb_tab)

    qb = tm // QBLK
    pb = tm // POOL_HALO
    n_qb = S // QBLK
    n_pb = S // POOL_HALO
    nt = S // tm
    n_tiles_total = B * nt

    def mix_spec(shape, index):
        def index_map(s):
            t = jnp.minimum(s, n_tiles_total - 1)
            return index(t // nt, t % nt)
        return pl.BlockSpec(shape, index_map)

    def ffn_index(s):
        t = jnp.maximum(s - 1, 0)
        return (t // nt, t % nt, 0)

    prev_q = lambda i: jnp.maximum(i * qb - 1, 0)
    next_q = lambda i: jnp.minimum((i + 1) * qb, n_qb - 1)
    prev_p = lambda i: jnp.maximum(i * pb - 1, 0)
    next_p = lambda i: jnp.minimum((i + 1) * pb, n_pb - 1)

    out = pl.pallas_call(
        functools.partial(_mix_ffn_kernel, tm=tm, seq=S, n_tiles_total=n_tiles_total),
        grid=(n_tiles_total + 1,),
        in_specs=[
            pl.BlockSpec(memory_space=pltpu.SMEM),
            pl.BlockSpec((1, tm, D), ffn_index),
            mix_spec((1, tm, Q_WIDTH), lambda b, i: (b, i, 0)),
            mix_spec((1, KV_WIDTH, QBLK), lambda b, i: (b, 0, prev_q(i))),
            mix_spec((1, KV_WIDTH, tm), lambda b, i: (b, 0, i)),
            mix_spec((1, KV_WIDTH, QBLK), lambda b, i: (b, 0, next_q(i))),
            mix_spec((1, QBLK, KV_WIDTH), lambda b, i: (b, prev_q(i), 0)),
            mix_spec((1, tm, KV_WIDTH), lambda b, i: (b, i, 0)),
            mix_spec((1, QBLK, KV_WIDTH), lambda b, i: (b, next_q(i), 0)),
            mix_spec((1, POOL_HALO, POOL_WIDTH), lambda b, i: (b, prev_p(i), 0)),
            mix_spec((1, tm, POOL_WIDTH), lambda b, i: (b, i, 0)),
            mix_spec((1, POOL_HALO, POOL_WIDTH), lambda b, i: (b, next_p(i), 0)),
            _const_spec((3, QBLK, 2 * KEYS)),
            _const_spec((QBLK, QBLK)),
            _const_spec((len(POOL_WINDOWS), QBLK, 2 * QBLK)),
            _const_spec((len(POOL_WINDOWS), GROUP_W, GROUP_W)),
            _const_spec((1, POOL_WIDTH)),
            hbm_spec,
            _const_spec((1, D)),
            hbm_spec, hbm_spec, hbm_spec,
            _const_spec((1, D)),
        ],
        out_specs=pl.BlockSpec((1, tm, D), ffn_index),
        out_shape=jax.ShapeDtypeStruct((B, S, D), x.dtype),
        scratch_shapes=[
            pltpu.VMEM((KV_WIDTH, tm + 2 * QBLK), BF16),
            pltpu.VMEM((N_KV, 2, tm + 2 * QBLK, LANES), BF16),
            pltpu.VMEM((tm + 2 * POOL_MARGIN, 2 * POOL_WIDTH), BF16),
            pltpu.VMEM((2, 3 * POOL_HALO, POOL_WIDTH), F32),
            pltpu.VMEM((tm, POOL_WIDTH), F32),
            pltpu.VMEM((2, tm, D), BF16),
            pltpu.VMEM((D, D), BF16),
            pltpu.VMEM((D, D_FF), BF16), pltpu.VMEM((D, D_FF), BF16), pltpu.VMEM((D_FF, D), BF16),
        ],
        compiler_params=pltpu.CompilerParams(dimension_semantics=("arbitrary",),
                                             vmem_limit_bytes=VMEM_LIMIT),
        name="mix_ffn2",
    )(sink_logits[0].astype(F32), h1, q, kt, kt, kt, v, v, v, pc, pc, pc,
      _band_bias(S), jnp.asarray(np.eye(QBLK, dtype=np.float32)).astype(BF16), _pool_band(),
      pool_w[0].astype(BF16),
      row(pool_scale[0]), w_out[0],
      row(ffn2_norm[0]), ffn2_w_gate[0], ffn2_w_up[0], ffn2_w_down[0], row(final_norm))
    return out
```

```python
import functools

import numpy as np
import jax
import jax.numpy as jnp
from jax import lax
from jax.experimental import pallas as pl
from jax.experimental.pallas import tpu as pltpu

D_MODEL = 1024
HEAD_DIM = 64
N_HEADS = 8
N_KV = 2
Q_WIDTH = N_HEADS * HEAD_DIM
KV_WIDTH = N_KV * HEAD_DIM
WINDOW = 128
QBLK = 128
KEYS = 3 * QBLK
ROPE_THETA = 500000.0
ROTARY_DIM = HEAD_DIM // 4
POOL_WINDOWS = (2, 4, 8, 16)
POOL_HALO = 8
POOL_MARGIN = 64
POOL_WIDTH = 512
GROUP_W = 128
D_FF = 2816
EPS = 1e-6
LANES = 128
NEG = -1e30
LOG2E = 1.4426950408889634

TM = 512
TM_IN = 1024
SUB_ROWS = 256
CAST_ROWS_IN = 128
CAST_ROWS_MIX = 256
CAST_SLOTS = 3
VMEM_LIMIT = 58 * 1024 * 1024

F32 = jnp.float32
BF16 = jnp.bfloat16


def _rms(x, g):
    ms = jnp.mean(x * x, axis=-1, keepdims=True)
    return x * lax.rsqrt(ms + EPS) * g


def _gated(gate, up):
    return (gate * jax.nn.sigmoid(gate) * up).astype(BF16)


def _load_weights_bf16(pairs, chunk_rows):
    ahead = CAST_SLOTS - 1
    counts = [src.shape[0] // chunk_rows for src, _ in pairs]
    bases = [sum(counts[:w]) for w in range(len(pairs))]
    assert all(n >= ahead for n in counts)

    def scoped(stage, sem):
        def chunk_copy(w, k, g):
            src = pairs[w][0]
            slot = lax.rem(g, CAST_SLOTS)
            return pltpu.make_async_copy(src.at[pl.ds(k * chunk_rows, chunk_rows)],
                                         stage.at[slot, :, pl.ds(0, src.shape[1])], sem.at[slot])

        for k in range(ahead):
            chunk_copy(0, k, k).start()

        for w, (src, dst) in enumerate(pairs):
            n, base, width = counts[w], bases[w], src.shape[1]

            def body(k, carry, w=w, n=n, base=base, width=width, dst=dst):
                @pl.when(k + ahead < n)
                def _():
                    chunk_copy(w, k + ahead, base + k + ahead).start()

                if w + 1 < len(pairs):
                    @pl.when(k + ahead >= n)
                    def _():
                        chunk_copy(w + 1, k + ahead - n, base + k + ahead).start()

                chunk_copy(w, k, base + k).wait()
                r0 = pl.multiple_of(k * chunk_rows, chunk_rows)
                slot = lax.rem(base + k, CAST_SLOTS)
                dst[pl.ds(r0, chunk_rows), :] = stage[slot, :, 0:width].astype(BF16)
                return carry

            lax.fori_loop(0, n, body, 0)

    pl.run_scoped(scoped, pltpu.VMEM((CAST_SLOTS, chunk_rows, D_FF), F32),
                  pltpu.SemaphoreType.DMA((CAST_SLOTS,)))


def _rope(t, c, sa, sb):
    t_plus = pltpu.roll(t, LANES - ROTARY_DIM // 2, 1)
    t_minus = pltpu.roll(t, ROTARY_DIM // 2, 1)
    return t * c + t_plus * sa + t_minus * sb


def _ffn_in_kernel(x_ref, g1_ref, wg_hbm, wu_hbm, wd_hbm, gm_ref, win_hbm,
                   c_ref, sa_ref, sb_ref,
                   h_ref, q_ref, kt_ref, v_ref, pc_ref,
                   wg_ref, wu_ref, wd_ref, win_ref):
    @pl.when((pl.program_id(0) == 0) & (pl.program_id(1) == 0))
    def _():
        _load_weights_bf16(((wg_hbm, wg_ref), (wu_hbm, wu_ref), (wd_hbm, wd_ref),
                            (win_hbm, win_ref)), CAST_ROWS_IN)

    tm = x_ref.shape[1]
    sub = SUB_ROWS
    scale = HEAD_DIM ** -0.5 * LOG2E
    def stage_norm(rows):
        return _rms(x_ref[0, rows], g1_ref[...]).astype(BF16)

    def stage_gate_up(xn):
        gate = jnp.dot(xn, wg_ref[...], preferred_element_type=F32)
        up = jnp.dot(xn, wu_ref[...], preferred_element_type=F32)
        return _gated(gate, up)

    def stage_down(rows, act):
        h = x_ref[0, rows] + 0.5 * jnp.dot(act, wd_ref[...], preferred_element_type=F32)
        h_ref[0, rows] = h
        return _rms(h, gm_ref[...]).astype(BF16)

    def stage_proj(hn):
        return jnp.dot(hn, win_ref[...], preferred_element_type=F32)

    def stage_out(rows, u):
        c, sa, sb = c_ref[rows], sa_ref[rows], sb_ref[rows]
        for t in range(Q_WIDTH // LANES):
            qt = _rope(u[:, t * LANES:(t + 1) * LANES], c, sa, sb)
            q_ref[0, rows, t * LANES:(t + 1) * LANES] = (qt * scale).astype(BF16)
        k = _rope(u[:, Q_WIDTH:Q_WIDTH + KV_WIDTH], c, sa, sb)
        kt_ref[0, :, rows] = k.T.astype(BF16)
        v_ref[0, rows] = u[:, Q_WIDTH + KV_WIDTH:Q_WIDTH + 2 * KV_WIDTH].astype(BF16)
        pc_ref[0, rows] = u[:, Q_WIDTH + 2 * KV_WIDTH:]

    for pair in range(tm // (2 * sub)):
        ra = slice((2 * pair) * sub, (2 * pair + 1) * sub)
        rb = slice((2 * pair + 1) * sub, (2 * pair + 2) * sub)
        xn_a, xn_b = stage_norm(ra), stage_norm(rb)
        act_a = stage_gate_up(xn_a)
        act_b = stage_gate_up(xn_b)
        hn_a = stage_down(ra, act_a)
        hn_b = stage_down(rb, act_b)
        u_a = stage_proj(hn_a)
        u_b = stage_proj(hn_b)
        stage_out(ra, u_a)
        stage_out(rb, u_b)


def _mix_ffn_kernel(sink_ref, h_ref, q_ref, ktp_ref, ktc_ref, ktn_ref,
                    vp_ref, vc_ref, vn_ref, pcp_ref, pcc_ref, pcn_ref,
                    bias_ref, eye_ref, band_ref, poolw_ref, pscale_ref, wout_hbm,
                    g2_ref, wg_hbm, wu_hbm, wd_hbm, gf_ref,
                    o_ref,
                    kt_buf, vab_buf, hl_buf, edge_buf, d_buf, mix_buf,
                    wout_ref, wg_ref, wu_ref, wd_ref, *, tm, seq):
    @pl.when((pl.program_id(0) == 0) & (pl.program_id(1) == 0))
    def _():
        _load_weights_bf16(((wout_hbm, wout_ref), (wg_hbm, wg_ref), (wu_hbm, wu_ref),
                            (wd_hbm, wd_ref)), CAST_ROWS_MIX)

    i = pl.program_id(1)
    _prep_keys(ktp_ref, ktc_ref, ktn_ref, kt_buf, tm=tm)
    _prep_values(vp_ref, vc_ref, vn_ref, vab_buf, tm=tm)
    _prep_pool(pcp_ref, pcc_ref, pcn_ref, hl_buf, edge_buf, i, tm=tm, seq=seq)
    mix_rows = functools.partial(
        _mix_rows, sink_ref, q_ref, bias_ref, eye_ref, band_ref, poolw_ref, pscale_ref,
        kt_buf, vab_buf, hl_buf, edge_buf, d_buf, mix_buf, i, tm=tm, seq=seq)

    sub = tm // 2
    sub_blocks = sub // QBLK
    halves = (slice(0, sub), slice(sub, tm))

    def stage_out_proj(rows):
        h = h_ref[0, rows] + jnp.dot(mix_buf[rows], wout_ref[...], preferred_element_type=F32)
        return h, _rms(h, g2_ref[...]).astype(BF16)

    def stage_gate_up(hn):
        gate = jnp.dot(hn, wg_ref[...], preferred_element_type=F32)
        up = jnp.dot(hn, wu_ref[...], preferred_element_type=F32)
        return _gated(gate, up)

    def stage_down(rows, h, act):
        h = h + 0.5 * jnp.dot(act, wd_ref[...], preferred_element_type=F32)
        o_ref[0, rows] = _rms(h, gf_ref[...])

    mix_rows(0, 2 * sub_blocks)
    hs = [stage_out_proj(rows) for rows in halves]
    acts = [stage_gate_up(hn) for _, hn in hs]
    for rows, (h, _), act in zip(halves, hs, acts):
        stage_down(rows, h, act)


def _prep_keys(ktp_ref, ktc_ref, ktn_ref, kt_buf, *, tm):
    kt_buf[:, 0:QBLK] = ktp_ref[0]
    kt_buf[:, QBLK:QBLK + tm] = ktc_ref[0]
    kt_buf[:, QBLK + tm:] = ktn_ref[0]


def _prep_values(vp_ref, vc_ref, vn_ref, vab_buf, *, tm):
    for lo, ref, n in ((0, vp_ref, QBLK), (QBLK, vc_ref, tm), (QBLK + tm, vn_ref, QBLK)):
        v = ref[0].astype(F32)
        vr = pltpu.roll(v, HEAD_DIM, 1)
        low = lax.broadcasted_iota(jnp.int32, v.shape, 1) < HEAD_DIM
        zero = jnp.zeros_like(v)
        vab_buf[0, 0, lo:lo + n] = jnp.where(low, v, zero).astype(BF16)
        vab_buf[0, 1, lo:lo + n] = jnp.where(low, zero, vr).astype(BF16)
        vab_buf[1, 0, lo:lo + n] = jnp.where(low, vr, zero).astype(BF16)
        vab_buf[1, 1, lo:lo + n] = jnp.where(low, zero, v).astype(BF16)


def _prep_pool(pcp_ref, pcc_ref, pcn_ref, hl_buf, edge_buf, i, *, tm, seq):
    n_tiles = seq // tm
    prev_halo = jnp.where(i > 0, pcp_ref[0], 0.0)
    next_halo = jnp.where(i < n_tiles - 1, pcn_ref[0], 0.0)
    zpad = jnp.zeros((POOL_MARGIN - POOL_HALO, POOL_WIDTH), F32)
    for r0, nr, u in ((0, POOL_MARGIN, jnp.concatenate([zpad, prev_halo], axis=0)),
                      (POOL_MARGIN, tm, pcc_ref[0]),
                      (POOL_MARGIN + tm, POOL_MARGIN, jnp.concatenate([next_halo, zpad], axis=0))):
        hi = u.astype(BF16)
        lo = (u - hi.astype(F32)).astype(BF16)
        for g in range(len(POOL_WINDOWS)):
            lanes = slice(g * GROUP_W, (g + 1) * GROUP_W)
            hl_buf[r0:r0 + nr, 2 * g * GROUP_W:(2 * g + 1) * GROUP_W] = hi[:, lanes]
            hl_buf[r0:r0 + nr, (2 * g + 1) * GROUP_W:(2 * g + 2) * GROUP_W] = lo[:, lanes]

    edge_buf[0, 0:POOL_HALO] = prev_halo
    edge_buf[0, POOL_HALO:] = pcc_ref[0, 0:2 * POOL_HALO]
    edge_buf[1, 0:2 * POOL_HALO] = pcc_ref[0, tm - 2 * POOL_HALO:tm]
    edge_buf[1, 2 * POOL_HALO:] = next_halo


def _mix_rows(sink_ref, q_ref, bias_ref, eye_ref, band_ref, poolw_ref, pscale_ref,
              kt_buf, vab_buf, hl_buf, edge_buf, d_buf, mix_buf, i, j0, j1, *, tm, seq):
    n_tiles = seq // tm
    nq = tm // QBLK
    nb = seq // QBLK

    def band_block(n):
        for g in range(len(POOL_WINDOWS)):
            win = hl_buf[n * QBLK:n * QBLK + 2 * QBLK, 2 * g * GROUP_W:(2 * g + 2) * GROUP_W]
            dd = jnp.dot(band_ref[g], win, preferred_element_type=F32)
            d_buf[n * QBLK:(n + 1) * QBLK, g * GROUP_W:(g + 1) * GROUP_W] = (
                dd[:, :GROUP_W] + dd[:, GROUP_W:])

    low_out = lax.broadcasted_iota(jnp.int32, (QBLK, LANES), 1) < HEAD_DIM
    eye = eye_ref[...]

    def scores(j, c):
        blk = i * nq + j
        variant = jnp.where(blk == 0, 0, jnp.where(blk == nb - 1, 2, 1))
        bias_t = bias_ref[variant]
        rows = slice(j * QBLK, (j + 1) * QBLK)
        keys = slice(j * QBLK, j * QBLK + KEYS)
        lhs = jnp.concatenate(
            [jnp.concatenate([q_ref[0, rows, (2 * c) * LANES:(2 * c + 1) * LANES], eye], axis=1),
             jnp.concatenate([q_ref[0, rows, (2 * c + 1) * LANES:(2 * c + 2) * LANES], eye],
                             axis=1)], axis=0)
        kc = kt_buf[c * HEAD_DIM:(c + 1) * HEAD_DIM, keys]
        z = jnp.zeros_like(kc)
        rhs = jnp.concatenate(
            [jnp.concatenate([kc, z], axis=1), jnp.concatenate([z, kc], axis=1), bias_t],
            axis=0)
        return jnp.dot(lhs, rhs, preferred_element_type=F32)

    def softmax(c, s):
        p_rows, inv_den = [], []
        for r in range(2):
            p_cols = []
            for par in range(2):
                sq = s[r * QBLK:(r + 1) * QBLK, par * KEYS:(par + 1) * KEYS]
                sink = sink_ref[4 * c + 2 * r + par] * LOG2E
                m = jnp.maximum(jnp.max(sq, axis=-1, keepdims=True), sink)
                p = jnp.exp2(sq - m)
                p_cols.append(p.astype(BF16))
                inv_den.append(1.0 / (jnp.sum(p, axis=-1, keepdims=True) + jnp.exp2(sink - m)))
            p_rows.append(jnp.concatenate(p_cols, axis=1))
        return jnp.concatenate(p_rows, axis=0), inv_den

    def weighted_values(j, c, pmat, inv_den):
        rows = slice(j * QBLK, (j + 1) * QBLK)
        keys = slice(j * QBLK, j * QBLK + KEYS)
        vrhs = jnp.concatenate([vab_buf[c, 0, keys], vab_buf[c, 1, keys]], axis=0)
        o = jnp.dot(pmat, vrhs, preferred_element_type=F32)
        for r in range(2):
            inv = jnp.where(low_out, inv_den[2 * r], inv_den[2 * r + 1])
            t = 2 * c + r
            mix_buf[rows, t * LANES:(t + 1) * LANES] = (
                o[r * QBLK:(r + 1) * QBLK] * inv).astype(BF16)

    units = [(j, c) for j in range(j0, j1) for c in range(N_KV)]
    for n in range(j0, j1):
        band_block(n)
    s_next = scores(*units[0])
    for n, (j, c) in enumerate(units):
        s_cur = s_next
        if n + 1 < len(units):
            s_next = scores(*units[n + 1])
        weighted_values(j, c, *softmax(c, s_cur))

    for e, (row0, clipped) in enumerate(((0, i == 0), (tm - POOL_HALO, i == n_tiles - 1))):
        if not j0 * QBLK <= row0 < j1 * QBLK:
            continue
        tpos = i * tm + row0 + lax.broadcasted_iota(jnp.int32, (POOL_HALO, GROUP_W), 0)
        base = POOL_HALO
        for g, w in enumerate(POOL_WINDOWS):
            half = w // 2
            lanes = slice(g * GROUP_W, (g + 1) * GROUP_W)

            def wsum(lo, hi):
                acc = edge_buf[e, base + lo:base + lo + POOL_HALO, lanes]
                for k in range(lo + 1, hi + 1):
                    acc = acc + edge_buf[e, base + k:base + k + POOL_HALO, lanes]
                return acc

            def count(lo, hi):
                a = jnp.clip(tpos + lo, 0, seq)
                b = jnp.clip(tpos + hi + 1, 0, seq)
                return (b - a).astype(F32)

            mean = 0.5 * (wsum(-half, half - 1) / count(-half, half - 1)
                          + wsum(-half + 1, half) / count(-half + 1, half))
            fixed = mean - edge_buf[e, base:base + POOL_HALO, lanes]
            d_buf[row0:row0 + POOL_HALO, lanes] = jnp.where(
                clipped, fixed, d_buf[row0:row0 + POOL_HALO, lanes])

    rows = slice(j0 * QBLK, j1 * QBLK)
    for g in range(len(POOL_WINDOWS)):
        lanes = slice(g * GROUP_W, (g + 1) * GROUP_W)
        y = jnp.dot(d_buf[rows, lanes].astype(BF16), poolw_ref[g], preferred_element_type=F32)
        mix_buf[rows, Q_WIDTH + g * GROUP_W:Q_WIDTH + (g + 1) * GROUP_W] = (
            y * pscale_ref[:, lanes]).astype(BF16)


def _rope_tables(seq):
    f32 = np.float32
    inv_freq = f32(ROPE_THETA) ** (-np.arange(0, ROTARY_DIM, 2, dtype=f32) / f32(ROTARY_DIM))
    ang = np.arange(seq, dtype=f32)[:, None] * inv_freq[None, :]
    cos, sin = np.cos(ang.astype(np.float64)).astype(f32), np.sin(ang.astype(np.float64)).astype(f32)
    half = ROTARY_DIM // 2
    rest = HEAD_DIM - ROTARY_DIM
    c = np.concatenate([cos, cos, np.ones((seq, rest), f32)], axis=1)
    sa = np.concatenate([-sin, np.zeros((seq, HEAD_DIM - half), f32)], axis=1)
    sb = np.concatenate([np.zeros((seq, half), f32), sin, np.zeros((seq, rest), f32)], axis=1)
    rep = LANES // HEAD_DIM
    return tuple(jnp.asarray(np.tile(t, (1, rep))) for t in (c, sa, sb))


def _band_bias(seq):
    r = np.arange(QBLK)[:, None]
    s = np.arange(KEYS)[None, :]
    band = (s - r >= 0) & (s - r <= 2 * WINDOW)
    first = band & (s >= QBLK)
    last = band & (s < 2 * QBLK)
    out = np.stack([np.tile(np.where(m, 0.0, NEG), (1, 2)) for m in (first, band, last)])
    return jnp.asarray(out, dtype=F32).astype(BF16)


def _pool_band():
    out = np.zeros((len(POOL_WINDOWS), QBLK, 2 * QBLK), np.float32)
    r = np.arange(QBLK)
    for g, w in enumerate(POOL_WINDOWS):
        half = w // 2
        for k in range(-half, half + 1):
            out[g, r, r + POOL_MARGIN + k] = (0.5 if abs(k) == half else 1.0) / w
        out[g, r, r + POOL_MARGIN] -= 1.0
    return jnp.asarray(out).astype(BF16)


def _const_spec(shape):
    nd = len(shape)
    return pl.BlockSpec(shape, lambda *_: (0,) * nd, pipeline_mode=pl.Buffered(1))


def kernel(x, ffn1_norm, ffn1_w_gate, ffn1_w_up, ffn1_w_down, mix_norm, w_in, sink_logits,
           pool_w, pool_scale, w_out, ffn2_norm, ffn2_w_gate, ffn2_w_up, ffn2_w_down, final_norm):
    B, S, D = x.shape
    assert D == D_MODEL and S % TM == 0 and TM % QBLK == 0 and ffn1_norm.shape[0] == 1
    assert S % TM_IN == 0 and TM_IN % SUB_ROWS == 0
    tm = TM
    grid = (B, S // tm)
    in_width = w_in.shape[-1]
    params = pltpu.CompilerParams(dimension_semantics=("arbitrary", "arbitrary"),
                                  vmem_limit_bytes=VMEM_LIMIT)

    c_tab, sa_tab, sb_tab = _rope_tables(S)
    row = lambda g: g.reshape(1, -1).astype(F32)
    tile_spec = lambda width, rows=tm: pl.BlockSpec((1, rows, width), lambda b, i: (b, i, 0))
    tm_in = TM_IN
    tab_spec = pl.BlockSpec((tm_in, LANES), lambda b, i: (i, 0))
    hbm_spec = pl.BlockSpec(memory_space=pl.ANY)

    h1, q, kt, v, pc = pl.pallas_call(
        _ffn_in_kernel,
        grid=(B, S // tm_in),
        in_specs=[
            tile_spec(D, tm_in),
            _const_spec((1, D)),
            hbm_spec, hbm_spec, hbm_spec,
            _const_spec((1, D)),
            hbm_spec,
            tab_spec, tab_spec, tab_spec,
        ],
        out_specs=[
            tile_spec(D, tm_in),
            tile_spec(Q_WIDTH, tm_in),
            pl.BlockSpec((1, KV_WIDTH, tm_in), lambda b, i: (b, 0, i)),
            tile_spec(KV_WIDTH, tm_in),
            tile_spec(POOL_WIDTH, tm_in),
        ],
        out_shape=[
            jax.ShapeDtypeStruct((B, S, D), F32),
            jax.ShapeDtypeStruct((B, S, Q_WIDTH), BF16),
            jax.ShapeDtypeStruct((B, KV_WIDTH, S), BF16),
            jax.ShapeDtypeStruct((B, S, KV_WIDTH), BF16),
            jax.ShapeDtypeStruct((B, S, POOL_WIDTH), F32),
        ],
        scratch_shapes=[
            pltpu.VMEM((D, D_FF), BF16), pltpu.VMEM((D, D_FF), BF16), pltpu.VMEM((D_FF, D), BF16),
            pltpu.VMEM((D, in_width), BF16),
        ],
        compiler_params=params,
        name="ffn1_inproj",
    )(x, row(ffn1_norm[0]), ffn1_w_gate[0], ffn1_w_up[0], ffn1_w_down[0], row(mix_norm[0]),
      w_in[0], c_tab, sa_tab, sb_tab)

    qb = tm // QBLK
    pb = tm // POOL_HALO
    n_qb = S // QBLK
    n_pb = S // POOL_HALO
    mix_spec = pl.BlockSpec
    prev_q = lambda i: jnp.maximum(i * qb - 1, 0)
    next_q = lambda i: jnp.minimum((i + 1) * qb, n_qb - 1)
    prev_p = lambda i: jnp.maximum(i * pb - 1, 0)
    next_p = lambda i: jnp.minimum((i + 1) * pb, n_pb - 1)

    out = pl.pallas_call(
        functools.partial(_mix_ffn_kernel, tm=tm, seq=S),
        grid=grid,
        in_specs=[
            pl.BlockSpec(memory_space=pltpu.SMEM),
            tile_spec(D),
            mix_spec((1, tm, Q_WIDTH), lambda b, i: (b, i, 0)),
            mix_spec((1, KV_WIDTH, QBLK), lambda b, i: (b, 0, prev_q(i))),
            mix_spec((1, KV_WIDTH, tm), lambda b, i: (b, 0, i)),
            mix_spec((1, KV_WIDTH, QBLK), lambda b, i: (b, 0, next_q(i))),
            mix_spec((1, QBLK, KV_WIDTH), lambda b, i: (b, prev_q(i), 0)),
            mix_spec((1, tm, KV_WIDTH), lambda b, i: (b, i, 0)),
            mix_spec((1, QBLK, KV_WIDTH), lambda b, i: (b, next_q(i), 0)),
            mix_spec((1, POOL_HALO, POOL_WIDTH), lambda b, i: (b, prev_p(i), 0)),
            mix_spec((1, tm, POOL_WIDTH), lambda b, i: (b, i, 0)),
            mix_spec((1, POOL_HALO, POOL_WIDTH), lambda b, i: (b, next_p(i), 0)),
            _const_spec((3, QBLK, 2 * KEYS)),
            _const_spec((QBLK, QBLK)),
            _const_spec((len(POOL_WINDOWS), QBLK, 2 * QBLK)),
            _const_spec((len(POOL_WINDOWS), GROUP_W, GROUP_W)),
            _const_spec((1, POOL_WIDTH)),
            hbm_spec,
            _const_spec((1, D)),
            hbm_spec, hbm_spec, hbm_spec,
            _const_spec((1, D)),
        ],
        out_specs=tile_spec(D),
        out_shape=jax.ShapeDtypeStruct((B, S, D), x.dtype),
        scratch_shapes=[
            pltpu.VMEM((KV_WIDTH, tm + 2 * QBLK), BF16),
            pltpu.VMEM((N_KV, 2, tm + 2 * QBLK, LANES), BF16),
            pltpu.VMEM((tm + 2 * POOL_MARGIN, 2 * POOL_WIDTH), BF16),
            pltpu.VMEM((2, 3 * POOL_HALO, POOL_WIDTH), F32),
            pltpu.VMEM((tm, POOL_WIDTH), F32),
            pltpu.VMEM((tm, D), BF16),
            pltpu.VMEM((D, D), BF16),
            pltpu.VMEM((D, D_FF), BF16), pltpu.VMEM((D, D_FF), BF16), pltpu.VMEM((D_FF, D), BF16),
        ],
        compiler_params=params,
        name="mix_ffn2",
    )(sink_logits[0].astype(F32), h1, q, kt, kt, kt, v, v, v, pc, pc, pc,
      _band_bias(S), jnp.asarray(np.eye(QBLK, dtype=np.float32)).astype(BF16), _pool_band(),
      pool_w[0].astype(BF16),
      row(pool_scale[0]), w_out[0],
      row(ffn2_norm[0]), ffn2_w_gate[0], ffn2_w_up[0], ffn2_w_down[0], row(final_norm))
    return out
```

```python
import functools

import numpy as np
import jax
import jax.numpy as jnp
from jax import lax
from jax.experimental import pallas as pl
from jax.experimental.pallas import tpu as pltpu

D_MODEL = 1024
HEAD_DIM = 64
N_HEADS = 8
N_KV = 2
Q_WIDTH = N_HEADS * HEAD_DIM
KV_WIDTH = N_KV * HEAD_DIM
WINDOW = 128
QBLK = 128
KEYS = 3 * QBLK
ROPE_THETA = 500000.0
ROTARY_DIM = HEAD_DIM // 4
POOL_WINDOWS = (2, 4, 8, 16)
POOL_HALO = 8
POOL_MARGIN = 64
POOL_WIDTH = 512
GROUP_W = 128
D_FF = 2816
EPS = 1e-6
LANES = 128
NEG = -1e30
LOG2E = 1.4426950408889634

TM = 512
TM_IN = 1024
SUB_ROWS = 256
CAST_ROWS_IN = 128
CAST_ROWS_MIX = 256
FF_CHUNK = 256
CAST_SLOTS = 3
VMEM_LIMIT = 58 * 1024 * 1024

F32 = jnp.float32
BF16 = jnp.bfloat16


def _rms(x, g):
    ms = jnp.mean(x * x, axis=-1, keepdims=True)
    return x * lax.rsqrt(ms + EPS) * g


def _gated(gate, up):
    return (gate * jax.nn.sigmoid(gate) * up).astype(BF16)


def _load_weights_bf16(pairs, chunk_rows):
    ahead = CAST_SLOTS - 1
    counts = [src.shape[0] // chunk_rows for src, _ in pairs]
    bases = [sum(counts[:w]) for w in range(len(pairs))]
    assert all(n >= ahead for n in counts)

    def scoped(stage, sem):
        def chunk_copy(w, k, g):
            src = pairs[w][0]
            slot = lax.rem(g, CAST_SLOTS)
            return pltpu.make_async_copy(src.at[pl.ds(k * chunk_rows, chunk_rows)],
                                         stage.at[slot, :, pl.ds(0, src.shape[1])], sem.at[slot])

        for k in range(ahead):
            chunk_copy(0, k, k).start()

        for w, (src, dst) in enumerate(pairs):
            n, base, width = counts[w], bases[w], src.shape[1]

            def body(k, carry, w=w, n=n, base=base, width=width, dst=dst):
                @pl.when(k + ahead < n)
                def _():
                    chunk_copy(w, k + ahead, base + k + ahead).start()

                if w + 1 < len(pairs):
                    @pl.when(k + ahead >= n)
                    def _():
                        chunk_copy(w + 1, k + ahead - n, base + k + ahead).start()

                chunk_copy(w, k, base + k).wait()
                r0 = pl.multiple_of(k * chunk_rows, chunk_rows)
                slot = lax.rem(base + k, CAST_SLOTS)
                dst[pl.ds(r0, chunk_rows), :] = stage[slot, :, 0:width].astype(BF16)
                return carry

            lax.fori_loop(0, n, body, 0)

    pl.run_scoped(scoped, pltpu.VMEM((CAST_SLOTS, chunk_rows, D_FF), F32),
                  pltpu.SemaphoreType.DMA((CAST_SLOTS,)))


def _rope(t, c, sa, sb):
    t_plus = pltpu.roll(t, LANES - ROTARY_DIM // 2, 1)
    t_minus = pltpu.roll(t, ROTARY_DIM // 2, 1)
    return t * c + t_plus * sa + t_minus * sb


def _ffn_in_kernel(x_ref, g1_ref, wg_hbm, wu_hbm, wd_hbm, gm_ref, win_hbm,
                   c_ref, sa_ref, sb_ref,
                   h_ref, q_ref, kt_ref, v_ref, pc_ref,
                   wg_ref, wu_ref, wd_ref, win_ref):
    @pl.when((pl.program_id(0) == 0) & (pl.program_id(1) == 0))
    def _():
        _load_weights_bf16(((wg_hbm, wg_ref), (wu_hbm, wu_ref), (wd_hbm, wd_ref),
                            (win_hbm, win_ref)), CAST_ROWS_IN)

    tm = x_ref.shape[1]
    sub = SUB_ROWS
    scale = HEAD_DIM ** -0.5 * LOG2E
    def stage_norm(rows):
        return _rms(x_ref[0, rows], g1_ref[...]).astype(BF16)

    def stage_gate_up(xn):
        acts = []
        for c0 in range(0, D_FF, FF_CHUNK):
            cols = slice(c0, c0 + FF_CHUNK)
            gate = jnp.dot(xn, wg_ref[:, cols], preferred_element_type=F32)
            up = jnp.dot(xn, wu_ref[:, cols], preferred_element_type=F32)
            acts.append(_gated(gate, up))
        return jnp.concatenate(acts, axis=1)

    def stage_down(rows, act):
        h = x_ref[0, rows] + 0.5 * jnp.dot(act, wd_ref[...], preferred_element_type=F32)
        h_ref[0, rows] = h
        return _rms(h, gm_ref[...]).astype(BF16)

    def stage_proj(hn):
        return jnp.dot(hn, win_ref[...], preferred_element_type=F32)

    def stage_out(rows, u):
        c, sa, sb = c_ref[rows], sa_ref[rows], sb_ref[rows]
        for t in range(Q_WIDTH // LANES):
            qt = _rope(u[:, t * LANES:(t + 1) * LANES], c, sa, sb)
            q_ref[0, rows, t * LANES:(t + 1) * LANES] = (qt * scale).astype(BF16)
        k = _rope(u[:, Q_WIDTH:Q_WIDTH + KV_WIDTH], c, sa, sb)
        kt_ref[0, :, rows] = k.T.astype(BF16)
        v_ref[0, rows] = u[:, Q_WIDTH + KV_WIDTH:Q_WIDTH + 2 * KV_WIDTH].astype(BF16)
        pc_ref[0, rows] = u[:, Q_WIDTH + 2 * KV_WIDTH:]

    for pair in range(tm // (2 * sub)):
        ra = slice((2 * pair) * sub, (2 * pair + 1) * sub)
        rb = slice((2 * pair + 1) * sub, (2 * pair + 2) * sub)
        xn_a, xn_b = stage_norm(ra), stage_norm(rb)
        act_a = stage_gate_up(xn_a)
        act_b = stage_gate_up(xn_b)
        hn_a = stage_down(ra, act_a)
        hn_b = stage_down(rb, act_b)
        u_a = stage_proj(hn_a)
        u_b = stage_proj(hn_b)
        stage_out(ra, u_a)
        stage_out(rb, u_b)


def _mix_ffn_kernel(sink_ref, h_ref, q_ref, ktp_ref, ktc_ref, ktn_ref,
                    vp_ref, vc_ref, vn_ref, pcp_ref, pcc_ref, pcn_ref,
                    bias_ref, eye_ref, band_ref, poolw_ref, pscale_ref, wout_hbm,
                    g2_ref, wg_hbm, wu_hbm, wd_hbm, gf_ref,
                    o_ref,
                    kt_buf, vab_buf, hl_buf, edge_buf, d_buf, mix_buf,
                    wout_ref, wg_ref, wu_ref, wd_ref, *, tm, seq):
    @pl.when((pl.program_id(0) == 0) & (pl.program_id(1) == 0))
    def _():
        _load_weights_bf16(((wout_hbm, wout_ref), (wg_hbm, wg_ref), (wu_hbm, wu_ref),
                            (wd_hbm, wd_ref)), CAST_ROWS_MIX)

    i = pl.program_id(1)
    _prep_keys(ktp_ref, ktc_ref, ktn_ref, kt_buf, tm=tm)
    _prep_values(vp_ref, vc_ref, vn_ref, vab_buf, tm=tm)
    _prep_pool(pcp_ref, pcc_ref, pcn_ref, hl_buf, edge_buf, i, tm=tm, seq=seq)
    mix_rows = functools.partial(
        _mix_rows, sink_ref, q_ref, bias_ref, eye_ref, band_ref, poolw_ref, pscale_ref,
        kt_buf, vab_buf, hl_buf, edge_buf, d_buf, mix_buf, i, tm=tm, seq=seq)

    sub = tm // 2
    sub_blocks = sub // QBLK
    halves = (slice(0, sub), slice(sub, tm))

    def stage_out_proj(rows):
        h = h_ref[0, rows] + jnp.dot(mix_buf[rows], wout_ref[...], preferred_element_type=F32)
        return h, _rms(h, g2_ref[...]).astype(BF16)

    def stage_gate_up(hn):
        acts = []
        for c0 in range(0, D_FF, FF_CHUNK):
            cols = slice(c0, c0 + FF_CHUNK)
            gate = jnp.dot(hn, wg_ref[:, cols], preferred_element_type=F32)
            up = jnp.dot(hn, wu_ref[:, cols], preferred_element_type=F32)
            acts.append(_gated(gate, up))
        return jnp.concatenate(acts, axis=1)

    def stage_down(rows, h, act):
        h = h + 0.5 * jnp.dot(act, wd_ref[...], preferred_element_type=F32)
        o_ref[0, rows] = _rms(h, gf_ref[...])

    mix_rows(0, 2 * sub_blocks)
    hs = [stage_out_proj(rows) for rows in halves]
    acts = [stage_gate_up(hn) for _, hn in hs]
    for rows, (h, _), act in zip(halves, hs, acts):
        stage_down(rows, h, act)


def _prep_keys(ktp_ref, ktc_ref, ktn_ref, kt_buf, *, tm):
    kt_buf[:, 0:QBLK] = ktp_ref[0]
    kt_buf[:, QBLK:QBLK + tm] = ktc_ref[0]
    kt_buf[:, QBLK + tm:] = ktn_ref[0]


def _prep_values(vp_ref, vc_ref, vn_ref, vab_buf, *, tm):
    for lo, ref, n in ((0, vp_ref, QBLK), (QBLK, vc_ref, tm), (QBLK + tm, vn_ref, QBLK)):
        v = ref[0].astype(F32)
        vr = pltpu.roll(v, HEAD_DIM, 1)
        low = lax.broadcasted_iota(jnp.int32, v.shape, 1) < HEAD_DIM
        zero = jnp.zeros_like(v)
        vab_buf[0, 0, lo:lo + n] = jnp.where(low, v, zero).astype(BF16)
        vab_buf[0, 1, lo:lo + n] = jnp.where(low, zero, vr).astype(BF16)
        vab_buf[1, 0, lo:lo + n] = jnp.where(low, vr, zero).astype(BF16)
        vab_buf[1, 1, lo:lo + n] = jnp.where(low, zero, v).astype(BF16)


def _prep_pool(pcp_ref, pcc_ref, pcn_ref, hl_buf, edge_buf, i, *, tm, seq):
    n_tiles = seq // tm
    prev_halo = jnp.where(i > 0, pcp_ref[0], 0.0)
    next_halo = jnp.where(i < n_tiles - 1, pcn_ref[0], 0.0)
    zpad = jnp.zeros((POOL_MARGIN - POOL_HALO, POOL_WIDTH), F32)
    for r0, nr, u in ((0, POOL_MARGIN, jnp.concatenate([zpad, prev_halo], axis=0)),
                      (POOL_MARGIN, tm, pcc_ref[0]),
                      (POOL_MARGIN + tm, POOL_MARGIN, jnp.concatenate([next_halo, zpad], axis=0))):
        hi = u.astype(BF16)
        lo = (u - hi.astype(F32)).astype(BF16)
        for g in range(len(POOL_WINDOWS)):
            lanes = slice(g * GROUP_W, (g + 1) * GROUP_W)
            hl_buf[r0:r0 + nr, 2 * g * GROUP_W:(2 * g + 1) * GROUP_W] = hi[:, lanes]
            hl_buf[r0:r0 + nr, (2 * g + 1) * GROUP_W:(2 * g + 2) * GROUP_W] = lo[:, lanes]

    edge_buf[0, 0:POOL_HALO] = prev_halo
    edge_buf[0, POOL_HALO:] = pcc_ref[0, 0:2 * POOL_HALO]
    edge_buf[1, 0:2 * POOL_HALO] = pcc_ref[0, tm - 2 * POOL_HALO:tm]
    edge_buf[1, 2 * POOL_HALO:] = next_halo


def _mix_rows(sink_ref, q_ref, bias_ref, eye_ref, band_ref, poolw_ref, pscale_ref,
              kt_buf, vab_buf, hl_buf, edge_buf, d_buf, mix_buf, i, j0, j1, *, tm, seq):
    n_tiles = seq // tm
    nq = tm // QBLK
    nb = seq // QBLK

    def band_block(n):
        for g in range(len(POOL_WINDOWS)):
            win = hl_buf[n * QBLK:n * QBLK + 2 * QBLK, 2 * g * GROUP_W:(2 * g + 2) * GROUP_W]
            dd = jnp.dot(band_ref[g], win, preferred_element_type=F32)
            d_buf[n * QBLK:(n + 1) * QBLK, g * GROUP_W:(g + 1) * GROUP_W] = (
                dd[:, :GROUP_W] + dd[:, GROUP_W:])

    low_out = lax.broadcasted_iota(jnp.int32, (QBLK, LANES), 1) < HEAD_DIM
    eye = eye_ref[...]

    def scores(j, c):
        blk = i * nq + j
        variant = jnp.where(blk == 0, 0, jnp.where(blk == nb - 1, 2, 1))
        bias_t = bias_ref[variant]
        rows = slice(j * QBLK, (j + 1) * QBLK)
        keys = slice(j * QBLK, j * QBLK + KEYS)
        lhs = jnp.concatenate(
            [jnp.concatenate([q_ref[0, rows, (2 * c) * LANES:(2 * c + 1) * LANES], eye], axis=1),
             jnp.concatenate([q_ref[0, rows, (2 * c + 1) * LANES:(2 * c + 2) * LANES], eye],
                             axis=1)], axis=0)
        kc = kt_buf[c * HEAD_DIM:(c + 1) * HEAD_DIM, keys]
        z = jnp.zeros_like(kc)
        rhs = jnp.concatenate(
            [jnp.concatenate([kc, z], axis=1), jnp.concatenate([z, kc], axis=1), bias_t],
            axis=0)
        return jnp.dot(lhs, rhs, preferred_element_type=F32)

    def softmax(c, s):
        p_rows, inv_den = [], []
        for r in range(2):
            p_cols = []
            for par in range(2):
                sq = s[r * QBLK:(r + 1) * QBLK, par * KEYS:(par + 1) * KEYS]
                sink = sink_ref[4 * c + 2 * r + par] * LOG2E
                m = jnp.maximum(jnp.max(sq, axis=-1, keepdims=True), sink)
                p = jnp.exp2(sq - m)
                p_cols.append(p.astype(BF16))
                inv_den.append(1.0 / (jnp.sum(p, axis=-1, keepdims=True) + jnp.exp2(sink - m)))
            p_rows.append(jnp.concatenate(p_cols, axis=1))
        return jnp.concatenate(p_rows, axis=0), inv_den

    def weighted_values(j, c, pmat, inv_den):
        rows = slice(j * QBLK, (j + 1) * QBLK)
        keys = slice(j * QBLK, j * QBLK + KEYS)
        vrhs = jnp.concatenate([vab_buf[c, 0, keys], vab_buf[c, 1, keys]], axis=0)
        o = jnp.dot(pmat, vrhs, preferred_element_type=F32)
        for r in range(2):
            inv = jnp.where(low_out, inv_den[2 * r], inv_den[2 * r + 1])
            t = 2 * c + r
            mix_buf[rows, t * LANES:(t + 1) * LANES] = (
                o[r * QBLK:(r + 1) * QBLK] * inv).astype(BF16)

    units = [(j, c) for j in range(j0, j1) for c in range(N_KV)]
    for n in range(j0, j1):
        band_block(n)
    s_next = scores(*units[0])
    for n, (j, c) in enumerate(units):
        s_cur = s_next
        if n + 1 < len(units):
            s_next = scores(*units[n + 1])
        weighted_values(j, c, *softmax(c, s_cur))

    for e, (row0, clipped) in enumerate(((0, i == 0), (tm - POOL_HALO, i == n_tiles - 1))):
        if not j0 * QBLK <= row0 < j1 * QBLK:
            continue
        tpos = i * tm + row0 + lax.broadcasted_iota(jnp.int32, (POOL_HALO, GROUP_W), 0)
        base = POOL_HALO
        for g, w in enumerate(POOL_WINDOWS):
            half = w // 2
            lanes = slice(g * GROUP_W, (g + 1) * GROUP_W)

            def wsum(lo, hi):
                acc = edge_buf[e, base + lo:base + lo + POOL_HALO, lanes]
                for k in range(lo + 1, hi + 1):
                    acc = acc + edge_buf[e, base + k:base + k + POOL_HALO, lanes]
                return acc

            def count(lo, hi):
                a = jnp.clip(tpos + lo, 0, seq)
                b = jnp.clip(tpos + hi + 1, 0, seq)
                return (b - a).astype(F32)

            mean = 0.5 * (wsum(-half, half - 1) / count(-half, half - 1)
                          + wsum(-half + 1, half) / count(-half + 1, half))
            fixed = mean - edge_buf[e, base:base + POOL_HALO, lanes]
            d_buf[row0:row0 + POOL_HALO, lanes] = jnp.where(
                clipped, fixed, d_buf[row0:row0 + POOL_HALO, lanes])

    rows = slice(j0 * QBLK, j1 * QBLK)
    for g in range(len(POOL_WINDOWS)):
        lanes = slice(g * GROUP_W, (g + 1) * GROUP_W)
        y = jnp.dot(d_buf[rows, lanes].astype(BF16), poolw_ref[g], preferred_element_type=F32)
        mix_buf[rows, Q_WIDTH + g * GROUP_W:Q_WIDTH + (g + 1) * GROUP_W] = (
            y * pscale_ref[:, lanes]).astype(BF16)


def _rope_tables(seq):
    f32 = np.float32
    inv_freq = f32(ROPE_THETA) ** (-np.arange(0, ROTARY_DIM, 2, dtype=f32) / f32(ROTARY_DIM))
    ang = np.arange(seq, dtype=f32)[:, None] * inv_freq[None, :]
    cos, sin = np.cos(ang.astype(np.float64)).astype(f32), np.sin(ang.astype(np.float64)).astype(f32)
    half = ROTARY_DIM // 2
    rest = HEAD_DIM - ROTARY_DIM
    c = np.concatenate([cos, cos, np.ones((seq, rest), f32)], axis=1)
    sa = np.concatenate([-sin, np.zeros((seq, HEAD_DIM - half), f32)], axis=1)
    sb = np.concatenate([np.zeros((seq, half), f32), sin, np.zeros((seq, rest), f32)], axis=1)
    rep = LANES // HEAD_DIM
    return tuple(jnp.asarray(np.tile(t, (1, rep))) for t in (c, sa, sb))


def _band_bias(seq):
    r = np.arange(QBLK)[:, None]
    s = np.arange(KEYS)[None, :]
    band = (s - r >= 0) & (s - r <= 2 * WINDOW)
    first = band & (s >= QBLK)
    last = band & (s < 2 * QBLK)
    out = np.stack([np.tile(np.where(m, 0.0, NEG), (1, 2)) for m in (first, band, last)])
    return jnp.asarray(out, dtype=F32).astype(BF16)


def _pool_band():
    out = np.zeros((len(POOL_WINDOWS), QBLK, 2 * QBLK), np.float32)
    r = np.arange(QBLK)
    for g, w in enumerate(POOL_WINDOWS):
        half = w // 2
        for k in range(-half, half + 1):
            out[g, r, r + POOL_MARGIN + k] = (0.5 if abs(k) == half else 1.0) / w
        out[g, r, r + POOL_MARGIN] -= 1.0
    return jnp.asarray(out).astype(BF16)


def _const_spec(shape):
    nd = len(shape)
    return pl.BlockSpec(shape, lambda *_: (0,) * nd, pipeline_mode=pl.Buffered(1))


def kernel(x, ffn1_norm, ffn1_w_gate, ffn1_w_up, ffn1_w_down, mix_norm, w_in, sink_logits,
           pool_w, pool_scale, w_out, ffn2_norm, ffn2_w_gate, ffn2_w_up, ffn2_w_down, final_norm):
    B, S, D = x.shape
    assert D == D_MODEL and S % TM == 0 and TM % QBLK == 0 and ffn1_norm.shape[0] == 1
    assert S % TM_IN == 0 and TM_IN % SUB_ROWS == 0
    tm = TM
    grid = (B, S // tm)
    in_width = w_in.shape[-1]
    params = pltpu.CompilerParams(dimension_semantics=("arbitrary", "arbitrary"),
                                  vmem_limit_bytes=VMEM_LIMIT)

    c_tab, sa_tab, sb_tab = _rope_tables(S)
    row = lambda g: g.reshape(1, -1).astype(F32)
    tile_spec = lambda width, rows=tm: pl.BlockSpec((1, rows, width), lambda b, i: (b, i, 0))
    tm_in = TM_IN
    tab_spec = pl.BlockSpec((tm_in, LANES), lambda b, i: (i, 0))
    hbm_spec = pl.BlockSpec(memory_space=pl.ANY)

    h1, q, kt, v, pc = pl.pallas_call(
        _ffn_in_kernel,
        grid=(B, S // tm_in),
        in_specs=[
            tile_spec(D, tm_in),
            _const_spec((1, D)),
            hbm_spec, hbm_spec, hbm_spec,
            _const_spec((1, D)),
            hbm_spec,
            tab_spec, tab_spec, tab_spec,
        ],
        out_specs=[
            tile_spec(D, tm_in),
            tile_spec(Q_WIDTH, tm_in),
            pl.BlockSpec((1, KV_WIDTH, tm_in), lambda b, i: (b, 0, i)),
            tile_spec(KV_WIDTH, tm_in),
            tile_spec(POOL_WIDTH, tm_in),
        ],
        out_shape=[
            jax.ShapeDtypeStruct((B, S, D), F32),
            jax.ShapeDtypeStruct((B, S, Q_WIDTH), BF16),
            jax.ShapeDtypeStruct((B, KV_WIDTH, S), BF16),
            jax.ShapeDtypeStruct((B, S, KV_WIDTH), BF16),
            jax.ShapeDtypeStruct((B, S, POOL_WIDTH), F32),
        ],
        scratch_shapes=[
            pltpu.VMEM((D, D_FF), BF16), pltpu.VMEM((D, D_FF), BF16), pltpu.VMEM((D_FF, D), BF16),
            pltpu.VMEM((D, in_width), BF16),
        ],
        compiler_params=params,
        name="ffn1_inproj",
    )(x, row(ffn1_norm[0]), ffn1_w_gate[0], ffn1_w_up[0], ffn1_w_down[0], row(mix_norm[0]),
      w_in[0], c_tab, sa_tab, sb_tab)

    qb = tm // QBLK
    pb = tm // POOL_HALO
    n_qb = S // QBLK
    n_pb = S // POOL_HALO
    mix_spec = pl.BlockSpec
    prev_q = lambda i: jnp.maximum(i * qb - 1, 0)
    next_q = lambda i: jnp.minimum((i + 1) * qb, n_qb - 1)
    prev_p = lambda i: jnp.maximum(i * pb - 1, 0)
    next_p = lambda i: jnp.minimum((i + 1) * pb, n_pb - 1)

    out = pl.pallas_call(
        functools.partial(_mix_ffn_kernel, tm=tm, seq=S),
        grid=grid,
        in_specs=[
            pl.BlockSpec(memory_space=pltpu.SMEM),
            tile_spec(D),
            mix_spec((1, tm, Q_WIDTH), lambda b, i: (b, i, 0)),
            mix_spec((1, KV_WIDTH, QBLK), lambda b, i: (b, 0, prev_q(i))),
            mix_spec((1, KV_WIDTH, tm), lambda b, i: (b, 0, i)),
            mix_spec((1, KV_WIDTH, QBLK), lambda b, i: (b, 0, next_q(i))),
            mix_spec((1, QBLK, KV_WIDTH), lambda b, i: (b, prev_q(i), 0)),
            mix_spec((1, tm, KV_WIDTH), lambda b, i: (b, i, 0)),
            mix_spec((1, QBLK, KV_WIDTH), lambda b, i: (b, next_q(i), 0)),
            mix_spec((1, POOL_HALO, POOL_WIDTH), lambda b, i: (b, prev_p(i), 0)),
            mix_spec((1, tm, POOL_WIDTH), lambda b, i: (b, i, 0)),
            mix_spec((1, POOL_HALO, POOL_WIDTH), lambda b, i: (b, next_p(i), 0)),
            _const_spec((3, QBLK, 2 * KEYS)),
            _const_spec((QBLK, QBLK)),
            _const_spec((len(POOL_WINDOWS), QBLK, 2 * QBLK)),
            _const_spec((len(POOL_WINDOWS), GROUP_W, GROUP_W)),
            _const_spec((1, POOL_WIDTH)),
            hbm_spec,
            _const_spec((1, D)),
            hbm_spec, hbm_spec, hbm_spec,
            _const_spec((1, D)),
        ],
        out_specs=tile_spec(D),
        out_shape=jax.ShapeDtypeStruct((B, S, D), x.dtype),
        scratch_shapes=[
            pltpu.VMEM((KV_WIDTH, tm + 2 * QBLK), BF16),
            pltpu.VMEM((N_KV, 2, tm + 2 * QBLK, LANES), BF16),
            pltpu.VMEM((tm + 2 * POOL_MARGIN, 2 * POOL_WIDTH), BF16),
            pltpu.VMEM((2, 3 * POOL_HALO, POOL_WIDTH), F32),
            pltpu.VMEM((tm, POOL_WIDTH), F32),
            pltpu.VMEM((tm, D), BF16),
            pltpu.VMEM((D, D), BF16),
            pltpu.VMEM((D, D_FF), BF16), pltpu.VMEM((D, D_FF), BF16), pltpu.VMEM((D_FF, D), BF16),
        ],
        compiler_params=params,
        name="mix_ffn2",
    )(sink_logits[0].astype(F32), h1, q, kt, kt, kt, v, v, v, pc, pc, pc,
      _band_bias(S), jnp.asarray(np.eye(QBLK, dtype=np.float32)).astype(BF16), _pool_band(),
      pool_w[0].astype(BF16),
      row(pool_scale[0]), w_out[0],
      row(ffn2_norm[0]), ffn2_w_gate[0], ffn2_w_up[0], ffn2_w_down[0], row(final_norm))
    return out
```

```python
import functools

import numpy as np
import jax
import jax.numpy as jnp
from jax import lax
from jax.experimental import pallas as pl
from jax.experimental.pallas import tpu as pltpu

D_MODEL = 1024
HEAD_DIM = 64
N_HEADS = 8
N_KV = 2
Q_WIDTH = N_HEADS * HEAD_DIM
KV_WIDTH = N_KV * HEAD_DIM
WINDOW = 128
QBLK = 128
KEYS = 3 * QBLK
ROPE_THETA = 500000.0
ROTARY_DIM = HEAD_DIM // 4
POOL_WINDOWS = (2, 4, 8, 16)
POOL_HALO = 8
POOL_MARGIN = 64
POOL_WIDTH = 512
GROUP_W = 128
D_FF = 2816
EPS = 1e-6
LANES = 128
NEG = -1e30
LOG2E = 1.4426950408889634

TM = 512
TM_IN = 1024
SUB_ROWS = 256
CAST_ROWS_IN = 128
CAST_ROWS_MIX = 256
FF_CHUNK = 256
CAST_SLOTS = 3
VMEM_LIMIT = 58 * 1024 * 1024

F32 = jnp.float32
BF16 = jnp.bfloat16


def _rms(x, g):
    ms = jnp.mean(x * x, axis=-1, keepdims=True)
    return x * lax.rsqrt(ms + EPS) * g


def _gated(gate, up):
    return (gate * jax.nn.sigmoid(gate) * up).astype(BF16)


def _gate_up(xn, wgu_ref):
    acts = []
    for c0 in range(0, 2 * D_FF, 2 * FF_CHUNK):
        gu = jnp.dot(xn, wgu_ref[:, c0:c0 + 2 * FF_CHUNK], preferred_element_type=F32)
        acts.append(_gated(gu[:, :FF_CHUNK], gu[:, FF_CHUNK:]))
    return jnp.concatenate(acts, axis=1)


def _load_weights_bf16(pairs, chunk_rows):
    ahead = CAST_SLOTS - 1
    counts = [src.shape[0] // chunk_rows for src, _, _ in pairs]
    bases = [sum(counts[:w]) for w in range(len(pairs))]
    assert all(n >= ahead for n in counts)

    def scoped(stage, sem):
        def chunk_copy(w, k, g):
            src = pairs[w][0]
            slot = lax.rem(g, CAST_SLOTS)
            return pltpu.make_async_copy(src.at[pl.ds(k * chunk_rows, chunk_rows)],
                                         stage.at[slot, :, pl.ds(0, src.shape[1])], sem.at[slot])

        for k in range(ahead):
            chunk_copy(0, k, k).start()

        for w, (src, dst, half) in enumerate(pairs):
            n, base, width = counts[w], bases[w], src.shape[1]

            def body(k, carry, w=w, n=n, base=base, width=width, dst=dst, half=half):
                @pl.when(k + ahead < n)
                def _():
                    chunk_copy(w, k + ahead, base + k + ahead).start()

                if w + 1 < len(pairs):
                    @pl.when(k + ahead >= n)
                    def _():
                        chunk_copy(w + 1, k + ahead - n, base + k + ahead).start()

                chunk_copy(w, k, base + k).wait()
                r0 = pl.multiple_of(k * chunk_rows, chunk_rows)
                slot = lax.rem(base + k, CAST_SLOTS)
                rows = pl.ds(r0, chunk_rows)
                if half is None:
                    dst[rows, :] = stage[slot, :, 0:width].astype(BF16)
                else:
                    for c in range(width // FF_CHUNK):
                        d0 = (2 * c + half) * FF_CHUNK
                        dst[rows, d0:d0 + FF_CHUNK] = (
                            stage[slot, :, c * FF_CHUNK:(c + 1) * FF_CHUNK].astype(BF16))
                return carry

            lax.fori_loop(0, n, body, 0)

    pl.run_scoped(scoped, pltpu.VMEM((CAST_SLOTS, chunk_rows, D_FF), F32),
                  pltpu.SemaphoreType.DMA((CAST_SLOTS,)))


def _rope(t, c, sa, sb):
    t_plus = pltpu.roll(t, LANES - ROTARY_DIM // 2, 1)
    t_minus = pltpu.roll(t, ROTARY_DIM // 2, 1)
    return t * c + t_plus * sa + t_minus * sb


def _ffn_in_kernel(x_ref, g1_ref, wg_hbm, wu_hbm, wd_hbm, gm_ref, win_hbm,
                   c_ref, sa_ref, sb_ref,
                   h_ref, q_ref, kt_ref, v_ref, pc_ref,
                   wgu_ref, wd_ref, win_ref):
    @pl.when((pl.program_id(0) == 0) & (pl.program_id(1) == 0))
    def _():
        _load_weights_bf16(((wg_hbm, wgu_ref, 0), (wu_hbm, wgu_ref, 1), (wd_hbm, wd_ref, None),
                            (win_hbm, win_ref, None)), CAST_ROWS_IN)

    tm = x_ref.shape[1]
    sub = SUB_ROWS
    scale = HEAD_DIM ** -0.5 * LOG2E
    def stage_norm(rows):
        return _rms(x_ref[0, rows], g1_ref[...]).astype(BF16)

    def stage_gate_up(xn):
        return _gate_up(xn, wgu_ref)

    def stage_down(rows, act):
        h = x_ref[0, rows] + 0.5 * jnp.dot(act, wd_ref[...], preferred_element_type=F32)
        h_ref[0, rows] = h
        return _rms(h, gm_ref[...]).astype(BF16)

    def stage_proj(hn):
        return jnp.dot(hn, win_ref[...], preferred_element_type=F32)

    def stage_out(rows, u):
        c, sa, sb = c_ref[rows], sa_ref[rows], sb_ref[rows]
        for t in range(Q_WIDTH // LANES):
            qt = _rope(u[:, t * LANES:(t + 1) * LANES], c, sa, sb)
            q_ref[0, rows, t * LANES:(t + 1) * LANES] = (qt * scale).astype(BF16)
        k = _rope(u[:, Q_WIDTH:Q_WIDTH + KV_WIDTH], c, sa, sb)
        kt_ref[0, :, rows] = k.T.astype(BF16)
        v_ref[0, rows] = u[:, Q_WIDTH + KV_WIDTH:Q_WIDTH + 2 * KV_WIDTH].astype(BF16)
        pc_ref[0, rows] = u[:, Q_WIDTH + 2 * KV_WIDTH:]

    for pair in range(tm // (2 * sub)):
        ra = slice((2 * pair) * sub, (2 * pair + 1) * sub)
        rb = slice((2 * pair + 1) * sub, (2 * pair + 2) * sub)
        xn_a, xn_b = stage_norm(ra), stage_norm(rb)
        act_a = stage_gate_up(xn_a)
        act_b = stage_gate_up(xn_b)
        hn_a = stage_down(ra, act_a)
        hn_b = stage_down(rb, act_b)
        u_a = stage_proj(hn_a)
        u_b = stage_proj(hn_b)
        stage_out(ra, u_a)
        stage_out(rb, u_b)


def _mix_ffn_kernel(sink_ref, h_ref, q_ref, ktp_ref, ktc_ref, ktn_ref,
                    vp_ref, vc_ref, vn_ref, pcp_ref, pcc_ref, pcn_ref,
                    bias_ref, eye_ref, band_ref, poolw_ref, pscale_ref, wout_hbm,
                    g2_ref, wg_hbm, wu_hbm, wd_hbm, gf_ref,
                    o_ref,
                    kt_buf, vab_buf, hl_buf, edge_buf, d_buf, mix_buf,
                    wout_ref, wgu_ref, wd_ref, *, tm, seq):
    @pl.when((pl.program_id(0) == 0) & (pl.program_id(1) == 0))
    def _():
        _load_weights_bf16(((wout_hbm, wout_ref, None), (wg_hbm, wgu_ref, 0), (wu_hbm, wgu_ref, 1),
                            (wd_hbm, wd_ref, None)), CAST_ROWS_MIX)

    i = pl.program_id(1)
    _prep_keys(ktp_ref, ktc_ref, ktn_ref, kt_buf, tm=tm)
    _prep_values(vp_ref, vc_ref, vn_ref, vab_buf, tm=tm)
    _prep_pool(pcp_ref, pcc_ref, pcn_ref, hl_buf, edge_buf, i, tm=tm, seq=seq)
    mix_rows = functools.partial(
        _mix_rows, sink_ref, q_ref, bias_ref, eye_ref, band_ref, poolw_ref, pscale_ref,
        kt_buf, vab_buf, hl_buf, edge_buf, d_buf, mix_buf, i, tm=tm, seq=seq)

    sub = tm // 2
    sub_blocks = sub // QBLK
    halves = (slice(0, sub), slice(sub, tm))

    def stage_out_proj(rows):
        h = h_ref[0, rows] + jnp.dot(mix_buf[rows], wout_ref[...], preferred_element_type=F32)
        return h, _rms(h, g2_ref[...]).astype(BF16)

    def stage_gate_up(hn):
        return _gate_up(hn, wgu_ref)

    def stage_down(rows, h, act):
        h = h + 0.5 * jnp.dot(act, wd_ref[...], preferred_element_type=F32)
        o_ref[0, rows] = _rms(h, gf_ref[...])

    mix_rows(0, 2 * sub_blocks)
    hs = [stage_out_proj(rows) for rows in halves]
    acts = [stage_gate_up(hn) for _, hn in hs]
    for rows, (h, _), act in zip(halves, hs, acts):
        stage_down(rows, h, act)


def _prep_keys(ktp_ref, ktc_ref, ktn_ref, kt_buf, *, tm):
    kt_buf[:, 0:QBLK] = ktp_ref[0]
    kt_buf[:, QBLK:QBLK + tm] = ktc_ref[0]
    kt_buf[:, QBLK + tm:] = ktn_ref[0]


def _prep_values(vp_ref, vc_ref, vn_ref, vab_buf, *, tm):
    for lo, ref, n in ((0, vp_ref, QBLK), (QBLK, vc_ref, tm), (QBLK + tm, vn_ref, QBLK)):
        v = ref[0].astype(F32)
        vr = pltpu.roll(v, HEAD_DIM, 1)
        low = lax.broadcasted_iota(jnp.int32, v.shape, 1) < HEAD_DIM
        zero = jnp.zeros_like(v)
        vab_buf[0, 0, lo:lo + n] = jnp.where(low, v, zero).astype(BF16)
        vab_buf[0, 1, lo:lo + n] = jnp.where(low, zero, vr).astype(BF16)
        vab_buf[1, 0, lo:lo + n] = jnp.where(low, vr, zero).astype(BF16)
        vab_buf[1, 1, lo:lo + n] = jnp.where(low, zero, v).astype(BF16)


def _prep_pool(pcp_ref, pcc_ref, pcn_ref, hl_buf, edge_buf, i, *, tm, seq):
    n_tiles = seq // tm
    prev_halo = jnp.where(i > 0, pcp_ref[0], 0.0)
    next_halo = jnp.where(i < n_tiles - 1, pcn_ref[0], 0.0)
    zpad = jnp.zeros((POOL_MARGIN - POOL_HALO, POOL_WIDTH), F32)
    for r0, nr, u in ((0, POOL_MARGIN, jnp.concatenate([zpad, prev_halo], axis=0)),
                      (POOL_MARGIN, tm, pcc_ref[0]),
                      (POOL_MARGIN + tm, POOL_MARGIN, jnp.concatenate([next_halo, zpad], axis=0))):
        hi = u.astype(BF16)
        lo = (u - hi.astype(F32)).astype(BF16)
        for g in range(len(POOL_WINDOWS)):
            lanes = slice(g * GROUP_W, (g + 1) * GROUP_W)
            hl_buf[r0:r0 + nr, 2 * g * GROUP_W:(2 * g + 1) * GROUP_W] = hi[:, lanes]
            hl_buf[r0:r0 + nr, (2 * g + 1) * GROUP_W:(2 * g + 2) * GROUP_W] = lo[:, lanes]

    edge_buf[0, 0:POOL_HALO] = prev_halo
    edge_buf[0, POOL_HALO:] = pcc_ref[0, 0:2 * POOL_HALO]
    edge_buf[1, 0:2 * POOL_HALO] = pcc_ref[0, tm - 2 * POOL_HALO:tm]
    edge_buf[1, 2 * POOL_HALO:] = next_halo


def _mix_rows(sink_ref, q_ref, bias_ref, eye_ref, band_ref, poolw_ref, pscale_ref,
              kt_buf, vab_buf, hl_buf, edge_buf, d_buf, mix_buf, i, j0, j1, *, tm, seq):
    n_tiles = seq // tm
    nq = tm // QBLK
    nb = seq // QBLK

    def band_block(n):
        for g in range(len(POOL_WINDOWS)):
            win = hl_buf[n * QBLK:n * QBLK + 2 * QBLK, 2 * g * GROUP_W:(2 * g + 2) * GROUP_W]
            dd = jnp.dot(band_ref[g], win, preferred_element_type=F32)
            d_buf[n * QBLK:(n + 1) * QBLK, g * GROUP_W:(g + 1) * GROUP_W] = (
                dd[:, :GROUP_W] + dd[:, GROUP_W:])

    low_out = lax.broadcasted_iota(jnp.int32, (QBLK, LANES), 1) < HEAD_DIM
    eye = eye_ref[...]

    def scores(j, c):
        blk = i * nq + j
        variant = jnp.where(blk == 0, 0, jnp.where(blk == nb - 1, 2, 1))
        bias_t = bias_ref[variant]
        rows = slice(j * QBLK, (j + 1) * QBLK)
        keys = slice(j * QBLK, j * QBLK + KEYS)
        lhs = jnp.concatenate(
            [jnp.concatenate([q_ref[0, rows, (2 * c) * LANES:(2 * c + 1) * LANES], eye], axis=1),
             jnp.concatenate([q_ref[0, rows, (2 * c + 1) * LANES:(2 * c + 2) * LANES], eye],
                             axis=1)], axis=0)
        kc = kt_buf[c * HEAD_DIM:(c + 1) * HEAD_DIM, keys]
        z = jnp.zeros_like(kc)
        rhs = jnp.concatenate(
            [jnp.concatenate([kc, z], axis=1), jnp.concatenate([z, kc], axis=1), bias_t],
            axis=0)
        return jnp.dot(lhs, rhs, preferred_element_type=F32)

    def softmax(c, s):
        p_rows, inv_den = [], []
        for r in range(2):
            p_cols = []
            for par in range(2):
                sq = s[r * QBLK:(r + 1) * QBLK, par * KEYS:(par + 1) * KEYS]
                sink = sink_ref[4 * c + 2 * r + par] * LOG2E
                m = jnp.maximum(jnp.max(sq, axis=-1, keepdims=True), sink)
                p = jnp.exp2(sq - m)
                p_cols.append(p.astype(BF16))
                inv_den.append(1.0 / (jnp.sum(p, axis=-1, keepdims=True) + jnp.exp2(sink - m)))
            p_rows.append(jnp.concatenate(p_cols, axis=1))
        return jnp.concatenate(p_rows, axis=0), inv_den

    def weighted_values(j, c, pmat, inv_den):
        rows = slice(j * QBLK, (j + 1) * QBLK)
        keys = slice(j * QBLK, j * QBLK + KEYS)
        vrhs = jnp.concatenate([vab_buf[c, 0, keys], vab_buf[c, 1, keys]], axis=0)
        o = jnp.dot(pmat, vrhs, preferred_element_type=F32)
        for r in range(2):
            inv = jnp.where(low_out, inv_den[2 * r], inv_den[2 * r + 1])
            t = 2 * c + r
            mix_buf[rows, t * LANES:(t + 1) * LANES] = (
                o[r * QBLK:(r + 1) * QBLK] * inv).astype(BF16)

    units = [(j, c) for j in range(j0, j1) for c in range(N_KV)]
    for n in range(j0, j1):
        band_block(n)
    s_next = scores(*units[0])
    for n, (j, c) in enumerate(units):
        s_cur = s_next
        if n + 1 < len(units):
            s_next = scores(*units[n + 1])
        weighted_values(j, c, *softmax(c, s_cur))

    for e, (row0, clipped) in enumerate(((0, i == 0), (tm - POOL_HALO, i == n_tiles - 1))):
        if not j0 * QBLK <= row0 < j1 * QBLK:
            continue
        tpos = i * tm + row0 + lax.broadcasted_iota(jnp.int32, (POOL_HALO, GROUP_W), 0)
        base = POOL_HALO
        for g, w in enumerate(POOL_WINDOWS):
            half = w // 2
            lanes = slice(g * GROUP_W, (g + 1) * GROUP_W)

            def wsum(lo, hi):
                acc = edge_buf[e, base + lo:base + lo + POOL_HALO, lanes]
                for k in range(lo + 1, hi + 1):
                    acc = acc + edge_buf[e, base + k:base + k + POOL_HALO, lanes]
                return acc

            def count(lo, hi):
                a = jnp.clip(tpos + lo, 0, seq)
                b = jnp.clip(tpos + hi + 1, 0, seq)
                return (b - a).astype(F32)

            mean = 0.5 * (wsum(-half, half - 1) / count(-half, half - 1)
                          + wsum(-half + 1, half) / count(-half + 1, half))
            fixed = mean - edge_buf[e, base:base + POOL_HALO, lanes]
            d_buf[row0:row0 + POOL_HALO, lanes] = jnp.where(
                clipped, fixed, d_buf[row0:row0 + POOL_HALO, lanes])

    rows = slice(j0 * QBLK, j1 * QBLK)
    for g in range(len(POOL_WINDOWS)):
        lanes = slice(g * GROUP_W, (g + 1) * GROUP_W)
        y = jnp.dot(d_buf[rows, lanes].astype(BF16), poolw_ref[g], preferred_element_type=F32)
        mix_buf[rows, Q_WIDTH + g * GROUP_W:Q_WIDTH + (g + 1) * GROUP_W] = (
            y * pscale_ref[:, lanes]).astype(BF16)


def _rope_tables(seq):
    f32 = np.float32
    inv_freq = f32(ROPE_THETA) ** (-np.arange(0, ROTARY_DIM, 2, dtype=f32) / f32(ROTARY_DIM))
    ang = np.arange(seq, dtype=f32)[:, None] * inv_freq[None, :]
    cos, sin = np.cos(ang.astype(np.float64)).astype(f32), np.sin(ang.astype(np.float64)).astype(f32)
    half = ROTARY_DIM // 2
    rest = HEAD_DIM - ROTARY_DIM
    c = np.concatenate([cos, cos, np.ones((seq, rest), f32)], axis=1)
    sa = np.concatenate([-sin, np.zeros((seq, HEAD_DIM - half), f32)], axis=1)
    sb = np.concatenate([np.zeros((seq, half), f32), sin, np.zeros((seq, rest), f32)], axis=1)
    rep = LANES // HEAD_DIM
    return tuple(jnp.asarray(np.tile(t, (1, rep))) for t in (c, sa, sb))


def _band_bias(seq):
    r = np.arange(QBLK)[:, None]
    s = np.arange(KEYS)[None, :]
    band = (s - r >= 0) & (s - r <= 2 * WINDOW)
    first = band & (s >= QBLK)
    last = band & (s < 2 * QBLK)
    out = np.stack([np.tile(np.where(m, 0.0, NEG), (1, 2)) for m in (first, band, last)])
    return jnp.asarray(out, dtype=F32).astype(BF16)


def _pool_band():
    out = np.zeros((len(POOL_WINDOWS), QBLK, 2 * QBLK), np.float32)
    r = np.arange(QBLK)
    for g, w in enumerate(POOL_WINDOWS):
        half = w // 2
        for k in range(-half, half + 1):
            out[g, r, r + POOL_MARGIN + k] = (0.5 if abs(k) == half else 1.0) / w
        out[g, r, r + POOL_MARGIN] -= 1.0
    return jnp.asarray(out).astype(BF16)


def _const_spec(shape):
    nd = len(shape)
    return pl.BlockSpec(shape, lambda *_: (0,) * nd, pipeline_mode=pl.Buffered(1))


def kernel(x, ffn1_norm, ffn1_w_gate, ffn1_w_up, ffn1_w_down, mix_norm, w_in, sink_logits,
           pool_w, pool_scale, w_out, ffn2_norm, ffn2_w_gate, ffn2_w_up, ffn2_w_down, final_norm):
    B, S, D = x.shape
    assert D == D_MODEL and S % TM == 0 and TM % QBLK == 0 and ffn1_norm.shape[0] == 1
    assert S % TM_IN == 0 and TM_IN % SUB_ROWS == 0
    tm = TM
    grid = (B, S // tm)
    in_width = w_in.shape[-1]
    params = pltpu.CompilerParams(dimension_semantics=("arbitrary", "arbitrary"),
                                  vmem_limit_bytes=VMEM_LIMIT)

    c_tab, sa_tab, sb_tab = _rope_tables(S)
    row = lambda g: g.reshape(1, -1).astype(F32)
    tile_spec = lambda width, rows=tm: pl.BlockSpec((1, rows, width), lambda b, i: (b, i, 0))
    tm_in = TM_IN
    tab_spec = pl.BlockSpec((tm_in, LANES), lambda b, i: (i, 0))
    hbm_spec = pl.BlockSpec(memory_space=pl.ANY)

    h1, q, kt, v, pc = pl.pallas_call(
        _ffn_in_kernel,
        grid=(B, S // tm_in),
        in_specs=[
            tile_spec(D, tm_in),
            _const_spec((1, D)),
            hbm_spec, hbm_spec, hbm_spec,
            _const_spec((1, D)),
            hbm_spec,
            tab_spec, tab_spec, tab_spec,
        ],
        out_specs=[
            tile_spec(D, tm_in),
            tile_spec(Q_WIDTH, tm_in),
            pl.BlockSpec((1, KV_WIDTH, tm_in), lambda b, i: (b, 0, i)),
            tile_spec(KV_WIDTH, tm_in),
            tile_spec(POOL_WIDTH, tm_in),
        ],
        out_shape=[
            jax.ShapeDtypeStruct((B, S, D), F32),
            jax.ShapeDtypeStruct((B, S, Q_WIDTH), BF16),
            jax.ShapeDtypeStruct((B, KV_WIDTH, S), BF16),
            jax.ShapeDtypeStruct((B, S, KV_WIDTH), BF16),
            jax.ShapeDtypeStruct((B, S, POOL_WIDTH), F32),
        ],
        scratch_shapes=[
            pltpu.VMEM((D, 2 * D_FF), BF16), pltpu.VMEM((D_FF, D), BF16),
            pltpu.VMEM((D, in_width), BF16),
        ],
        compiler_params=params,
        name="ffn1_inproj",
    )(x, row(ffn1_norm[0]), ffn1_w_gate[0], ffn1_w_up[0], ffn1_w_down[0], row(mix_norm[0]),
      w_in[0], c_tab, sa_tab, sb_tab)

    qb = tm // QBLK
    pb = tm // POOL_HALO
    n_qb = S // QBLK
    n_pb = S // POOL_HALO
    mix_spec = pl.BlockSpec
    prev_q = lambda i: jnp.maximum(i * qb - 1, 0)
    next_q = lambda i: jnp.minimum((i + 1) * qb, n_qb - 1)
    prev_p = lambda i: jnp.maximum(i * pb - 1, 0)
    next_p = lambda i: jnp.minimum((i + 1) * pb, n_pb - 1)

    out = pl.pallas_call(
        functools.partial(_mix_ffn_kernel, tm=tm, seq=S),
        grid=grid,
        in_specs=[
            pl.BlockSpec(memory_space=pltpu.SMEM),
            tile_spec(D),
            mix_spec((1, tm, Q_WIDTH), lambda b, i: (b, i, 0)),
            mix_spec((1, KV_WIDTH, QBLK), lambda b, i: (b, 0, prev_q(i))),
            mix_spec((1, KV_WIDTH, tm), lambda b, i: (b, 0, i)),
            mix_spec((1, KV_WIDTH, QBLK), lambda b, i: (b, 0, next_q(i))),
            mix_spec((1, QBLK, KV_WIDTH), lambda b, i: (b, prev_q(i), 0)),
            mix_spec((1, tm, KV_WIDTH), lambda b, i: (b, i, 0)),
            mix_spec((1, QBLK, KV_WIDTH), lambda b, i: (b, next_q(i), 0)),
            mix_spec((1, POOL_HALO, POOL_WIDTH), lambda b, i: (b, prev_p(i), 0)),
            mix_spec((1, tm, POOL_WIDTH), lambda b, i: (b, i, 0)),
            mix_spec((1, POOL_HALO, POOL_WIDTH), lambda b, i: (b, next_p(i), 0)),
            _const_spec((3, QBLK, 2 * KEYS)),
            _const_spec((QBLK, QBLK)),
            _const_spec((len(POOL_WINDOWS), QBLK, 2 * QBLK)),
            _const_spec((len(POOL_WINDOWS), GROUP_W, GROUP_W)),
            _const_spec((1, POOL_WIDTH)),
            hbm_spec,
            _const_spec((1, D)),
            hbm_spec, hbm_spec, hbm_spec,
            _const_spec((1, D)),
        ],
        out_specs=tile_spec(D),
        out_shape=jax.ShapeDtypeStruct((B, S, D), x.dtype),
        scratch_shapes=[
            pltpu.VMEM((KV_WIDTH, tm + 2 * QBLK), BF16),
            pltpu.VMEM((N_KV, 2, tm + 2 * QBLK, LANES), BF16),
            pltpu.VMEM((tm + 2 * POOL_MARGIN, 2 * POOL_WIDTH), BF16),
            pltpu.VMEM((2, 3 * POOL_HALO, POOL_WIDTH), F32),
            pltpu.VMEM((tm, POOL_WIDTH), F32),
            pltpu.VMEM((tm, D), BF16),
            pltpu.VMEM((D, D), BF16),
            pltpu.VMEM((D, 2 * D_FF), BF16), pltpu.VMEM((D_FF, D), BF16),
        ],
        compiler_params=params,
        name="mix_ffn2",
    )(sink_logits[0].astype(F32), h1, q, kt, kt, kt, v, v, v, pc, pc, pc,
      _band_bias(S), jnp.asarray(np.eye(QBLK, dtype=np.float32)).astype(BF16), _pool_band(),
      pool_w[0].astype(BF16),
      row(pool_scale[0]), w_out[0],
      row(ffn2_norm[0]), ffn2_w_gate[0], ffn2_w_up[0], ffn2_w_down[0], row(final_norm))
    return out
```

```python
import functools

import numpy as np
import jax
import jax.numpy as jnp
from jax import lax
from jax.experimental import pallas as pl
from jax.experimental.pallas import tpu as pltpu

D_MODEL = 1024
HEAD_DIM = 64
N_HEADS = 8
N_KV = 2
Q_WIDTH = N_HEADS * HEAD_DIM
KV_WIDTH = N_KV * HEAD_DIM
WINDOW = 128
QBLK = 128
KEYS = 3 * QBLK
ROPE_THETA = 500000.0
ROTARY_DIM = HEAD_DIM // 4
POOL_WINDOWS = (2, 4, 8, 16)
POOL_HALO = 8
POOL_MARGIN = 64
POOL_WIDTH = 512
GROUP_W = 128
D_FF = 2816
EPS = 1e-6
LANES = 128
NEG = -1e30
LOG2E = 1.4426950408889634

TM = 512
TM_IN = 1024
SUB_ROWS = 256
CAST_ROWS_IN = 128
CAST_ROWS_MIX = 256
FF_CHUNK = 256
CAST_SLOTS = 3
VMEM_LIMIT = 58 * 1024 * 1024

F32 = jnp.float32
BF16 = jnp.bfloat16


def _rms(x, g):
    ms = jnp.mean(x * x, axis=-1, keepdims=True)
    return x * lax.rsqrt(ms + EPS) * g


def _gated(gate, up):
    return (gate * jax.nn.sigmoid(gate) * up).astype(BF16)


def _gate_up(xn, wg_ref, wu_ref):
    acts = []
    for c0 in range(0, D_FF, FF_CHUNK):
        cols = slice(c0, c0 + FF_CHUNK)
        gate = jnp.dot(xn, wg_ref[:, cols], preferred_element_type=F32)
        up = jnp.dot(xn, wu_ref[:, cols], preferred_element_type=F32)
        acts.append(_gated(gate, up))
    return jnp.concatenate(acts, axis=1)


def _load_weights_bf16(pairs, chunk_rows):
    ahead = CAST_SLOTS - 1
    counts = [src.shape[0] // chunk_rows for src, _ in pairs]
    bases = [sum(counts[:w]) for w in range(len(pairs))]
    assert all(n >= ahead for n in counts)

    def scoped(stage, sem):
        def chunk_copy(w, k, g):
            src = pairs[w][0]
            slot = lax.rem(g, CAST_SLOTS)
            return pltpu.make_async_copy(src.at[pl.ds(k * chunk_rows, chunk_rows)],
                                         stage.at[slot, :, pl.ds(0, src.shape[1])], sem.at[slot])

        for k in range(ahead):
            chunk_copy(0, k, k).start()

        for w, (src, dst) in enumerate(pairs):
            n, base, width = counts[w], bases[w], src.shape[1]

            def body(k, carry, w=w, n=n, base=base, width=width, dst=dst):
                @pl.when(k + ahead < n)
                def _():
                    chunk_copy(w, k + ahead, base + k + ahead).start()

                if w + 1 < len(pairs):
                    @pl.when(k + ahead >= n)
                    def _():
                        chunk_copy(w + 1, k + ahead - n, base + k + ahead).start()

                chunk_copy(w, k, base + k).wait()
                r0 = pl.multiple_of(k * chunk_rows, chunk_rows)
                slot = lax.rem(base + k, CAST_SLOTS)
                dst[pl.ds(r0, chunk_rows), :] = stage[slot, :, 0:width].astype(BF16)
                return carry

            lax.fori_loop(0, n, body, 0)

    pl.run_scoped(scoped, pltpu.VMEM((CAST_SLOTS, chunk_rows, D_FF), F32),
                  pltpu.SemaphoreType.DMA((CAST_SLOTS,)))


def _rope(t, c, sa, sb):
    t_plus = pltpu.roll(t, LANES - ROTARY_DIM // 2, 1)
    t_minus = pltpu.roll(t, ROTARY_DIM // 2, 1)
    return t * c + t_plus * sa + t_minus * sb


def _ffn_in_kernel(x_ref, g1_ref, wg_hbm, wu_hbm, wd_hbm, gm_ref, win_hbm,
                   c_ref, sa_ref, sb_ref,
                   h_ref, q_ref, kt_ref, v_ref, pc_ref,
                   wg_ref, wu_ref, wd_ref, win_ref):
    @pl.when((pl.program_id(0) == 0) & (pl.program_id(1) == 0))
    def _():
        _load_weights_bf16(((wg_hbm, wg_ref), (wu_hbm, wu_ref), (wd_hbm, wd_ref),
                            (win_hbm, win_ref)), CAST_ROWS_IN)

    tm = x_ref.shape[1]
    sub = SUB_ROWS
    scale = HEAD_DIM ** -0.5 * LOG2E
    def stage_norm(rows):
        return _rms(x_ref[0, rows], g1_ref[...]).astype(BF16)

    def stage_gate_up(xn):
        return _gate_up(xn, wg_ref, wu_ref)

    def stage_down(rows, act):
        h = x_ref[0, rows] + 0.5 * jnp.dot(act, wd_ref[...], preferred_element_type=F32)
        h_ref[0, rows] = h
        return _rms(h, gm_ref[...]).astype(BF16)

    def stage_proj(hn):
        return jnp.dot(hn, win_ref[...], preferred_element_type=F32)

    def stage_out(rows, u):
        c, sa, sb = c_ref[rows], sa_ref[rows], sb_ref[rows]
        for t in range(Q_WIDTH // LANES):
            qt = _rope(u[:, t * LANES:(t + 1) * LANES], c, sa, sb)
            q_ref[0, rows, t * LANES:(t + 1) * LANES] = (qt * scale).astype(BF16)
        k = _rope(u[:, Q_WIDTH:Q_WIDTH + KV_WIDTH], c, sa, sb)
        kt_ref[0, :, rows] = k.T.astype(BF16)
        v_ref[0, rows] = u[:, Q_WIDTH + KV_WIDTH:Q_WIDTH + 2 * KV_WIDTH].astype(BF16)
        pc_ref[0, rows] = u[:, Q_WIDTH + 2 * KV_WIDTH:]

    for pair in range(tm // (2 * sub)):
        ra = slice((2 * pair) * sub, (2 * pair + 1) * sub)
        rb = slice((2 * pair + 1) * sub, (2 * pair + 2) * sub)
        xn_a, xn_b = stage_norm(ra), stage_norm(rb)
        act_a = stage_gate_up(xn_a)
        act_b = stage_gate_up(xn_b)
        hn_a = stage_down(ra, act_a)
        hn_b = stage_down(rb, act_b)
        u_a = stage_proj(hn_a)
        u_b = stage_proj(hn_b)
        stage_out(ra, u_a)
        stage_out(rb, u_b)


def _mix_ffn_kernel(sink_ref, h_ref, q_ref, ktp_ref, ktc_ref, ktn_ref,
                    vp_ref, vc_ref, vn_ref, pcp_ref, pcc_ref, pcn_ref,
                    bias_ref, eye_ref, band_ref, poolw_ref, pscale_ref, wout_hbm,
                    g2_ref, wg_hbm, wu_hbm, wd_hbm, gf_ref,
                    o_ref,
                    kt_buf, vab_buf, hl_buf, edge_buf, d_buf, mix_buf,
                    wout_ref, wg_ref, wu_ref, wd_ref, *, tm, seq):
    @pl.when((pl.program_id(0) == 0) & (pl.program_id(1) == 0))
    def _():
        _load_weights_bf16(((wout_hbm, wout_ref), (wg_hbm, wg_ref), (wu_hbm, wu_ref),
                            (wd_hbm, wd_ref)), CAST_ROWS_MIX)

    i = pl.program_id(1)
    _prep_keys(ktp_ref, ktc_ref, ktn_ref, kt_buf, tm=tm)
    _prep_values(vp_ref, vc_ref, vn_ref, vab_buf, tm=tm)
    _prep_pool(pcp_ref, pcc_ref, pcn_ref, hl_buf, edge_buf, i, tm=tm, seq=seq)
    mix_rows = functools.partial(
        _mix_rows, sink_ref, q_ref, bias_ref, eye_ref, band_ref, poolw_ref, pscale_ref,
        kt_buf, vab_buf, hl_buf, edge_buf, d_buf, mix_buf, i, tm=tm, seq=seq)

    sub = tm // 2
    sub_blocks = sub // QBLK
    halves = (slice(0, sub), slice(sub, tm))

    def stage_out_proj(rows):
        h = h_ref[0, rows] + jnp.dot(mix_buf[rows], wout_ref[...], preferred_element_type=F32)
        return h, _rms(h, g2_ref[...]).astype(BF16)

    def stage_gate_up(hn):
        return _gate_up(hn, wg_ref, wu_ref)

    def stage_down(rows, h, act):
        h = h + 0.5 * jnp.dot(act, wd_ref[...], preferred_element_type=F32)
        o_ref[0, rows] = _rms(h, gf_ref[...])

    mix_rows(0, 2 * sub_blocks)
    hs = [stage_out_proj(rows) for rows in halves]
    acts = [stage_gate_up(hn) for _, hn in hs]
    for rows, (h, _), act in zip(halves, hs, acts):
        stage_down(rows, h, act)


def _prep_keys(ktp_ref, ktc_ref, ktn_ref, kt_buf, *, tm):
    kt_buf[:, 0:QBLK] = ktp_ref[0]
    kt_buf[:, QBLK:QBLK + tm] = ktc_ref[0]
    kt_buf[:, QBLK + tm:] = ktn_ref[0]


def _prep_values(vp_ref, vc_ref, vn_ref, vab_buf, *, tm):
    for lo, ref, n in ((0, vp_ref, QBLK), (QBLK, vc_ref, tm), (QBLK + tm, vn_ref, QBLK)):
        v = ref[0].astype(F32)
        vr = pltpu.roll(v, HEAD_DIM, 1)
        low = lax.broadcasted_iota(jnp.int32, v.shape, 1) < HEAD_DIM
        zero = jnp.zeros_like(v)
        vab_buf[0, 0, lo:lo + n] = jnp.where(low, v, zero).astype(BF16)
        vab_buf[0, 1, lo:lo + n] = jnp.where(low, zero, vr).astype(BF16)
        vab_buf[1, 0, lo:lo + n] = jnp.where(low, vr, zero).astype(BF16)
        vab_buf[1, 1, lo:lo + n] = jnp.where(low, zero, v).astype(BF16)


def _prep_pool(pcp_ref, pcc_ref, pcn_ref, hl_buf, edge_buf, i, *, tm, seq):
    n_tiles = seq // tm
    prev_halo = jnp.where(i > 0, pcp_ref[0], 0.0)
    next_halo = jnp.where(i < n_tiles - 1, pcn_ref[0], 0.0)
    zpad = jnp.zeros((POOL_MARGIN - POOL_HALO, POOL_WIDTH), F32)
    for r0, nr, u in ((0, POOL_MARGIN, jnp.concatenate([zpad, prev_halo], axis=0)),
                      (POOL_MARGIN, tm, pcc_ref[0]),
                      (POOL_MARGIN + tm, POOL_MARGIN, jnp.concatenate([next_halo, zpad], axis=0))):
        hi = u.astype(BF16)
        lo = (u - hi.astype(F32)).astype(BF16)
        for g in range(len(POOL_WINDOWS)):
            lanes = slice(g * GROUP_W, (g + 1) * GROUP_W)
            hl_buf[r0:r0 + nr, 2 * g * GROUP_W:(2 * g + 1) * GROUP_W] = hi[:, lanes]
            hl_buf[r0:r0 + nr, (2 * g + 1) * GROUP_W:(2 * g + 2) * GROUP_W] = lo[:, lanes]

    edge_buf[0, 0:POOL_HALO] = prev_halo
    edge_buf[0, POOL_HALO:] = pcc_ref[0, 0:2 * POOL_HALO]
    edge_buf[1, 0:2 * POOL_HALO] = pcc_ref[0, tm - 2 * POOL_HALO:tm]
    edge_buf[1, 2 * POOL_HALO:] = next_halo


def _mix_rows(sink_ref, q_ref, bias_ref, eye_ref, band_ref, poolw_ref, pscale_ref,
              kt_buf, vab_buf, hl_buf, edge_buf, d_buf, mix_buf, i, j0, j1, *, tm, seq):
    n_tiles = seq // tm
    nq = tm // QBLK
    nb = seq // QBLK

    def band_block(n):
        for g in range(len(POOL_WINDOWS)):
            win = hl_buf[n * QBLK:n * QBLK + 2 * QBLK, 2 * g * GROUP_W:(2 * g + 2) * GROUP_W]
            dd = jnp.dot(band_ref[g], win, preferred_element_type=F32)
            d_buf[n * QBLK:(n + 1) * QBLK, g * GROUP_W:(g + 1) * GROUP_W] = (
                dd[:, :GROUP_W] + dd[:, GROUP_W:])

    low_out = lax.broadcasted_iota(jnp.int32, (QBLK, LANES), 1) < HEAD_DIM
    eye = eye_ref[...]

    def scores(j, c):
        blk = i * nq + j
        variant = jnp.where(blk == 0, 0, jnp.where(blk == nb - 1, 2, 1))
        bias_t = bias_ref[variant]
        rows = slice(j * QBLK, (j + 1) * QBLK)
        keys = slice(j * QBLK, j * QBLK + KEYS)
        lhs = jnp.concatenate(
            [jnp.concatenate([q_ref[0, rows, (2 * c) * LANES:(2 * c + 1) * LANES], eye], axis=1),
             jnp.concatenate([q_ref[0, rows, (2 * c + 1) * LANES:(2 * c + 2) * LANES], eye],
                             axis=1)], axis=0)
        kc = kt_buf[c * HEAD_DIM:(c + 1) * HEAD_DIM, keys]
        z = jnp.zeros_like(kc)
        rhs = jnp.concatenate(
            [jnp.concatenate([kc, z], axis=1), jnp.concatenate([z, kc], axis=1), bias_t],
            axis=0)
        return jnp.dot(lhs, rhs, preferred_element_type=F32)

    def softmax(c, s):
        p_rows, inv_den = [], []
        for r in range(2):
            p_cols = []
            for par in range(2):
                sq = s[r * QBLK:(r + 1) * QBLK, par * KEYS:(par + 1) * KEYS]
                sink = sink_ref[4 * c + 2 * r + par] * LOG2E
                m = jnp.maximum(jnp.max(sq, axis=-1, keepdims=True), sink)
                p = jnp.exp2(sq - m)
                p_cols.append(p.astype(BF16))
                inv_den.append(1.0 / (jnp.sum(p, axis=-1, keepdims=True) + jnp.exp2(sink - m)))
            p_rows.append(jnp.concatenate(p_cols, axis=1))
        return jnp.concatenate(p_rows, axis=0), inv_den

    def weighted_values(j, c, pmat, inv_den):
        rows = slice(j * QBLK, (j + 1) * QBLK)
        keys = slice(j * QBLK, j * QBLK + KEYS)
        vrhs = jnp.concatenate([vab_buf[c, 0, keys], vab_buf[c, 1, keys]], axis=0)
        o = jnp.dot(pmat, vrhs, preferred_element_type=F32)
        for r in range(2):
            inv = jnp.where(low_out, inv_den[2 * r], inv_den[2 * r + 1])
            t = 2 * c + r
            mix_buf[rows, t * LANES:(t + 1) * LANES] = (
                o[r * QBLK:(r + 1) * QBLK] * inv).astype(BF16)

    units = [(j, c) for j in range(j0, j1) for c in range(N_KV)]
    for n in range(j0, j1):
        band_block(n)
    s_next = scores(*units[0])
    for n, (j, c) in enumerate(units):
        s_cur = s_next
        if n + 1 < len(units):
            s_next = scores(*units[n + 1])
        weighted_values(j, c, *softmax(c, s_cur))

    for e, (row0, clipped) in enumerate(((0, i == 0), (tm - POOL_HALO, i == n_tiles - 1))):
        if not j0 * QBLK <= row0 < j1 * QBLK:
            continue
        tpos = i * tm + row0 + lax.broadcasted_iota(jnp.int32, (POOL_HALO, GROUP_W), 0)
        base = POOL_HALO
        for g, w in enumerate(POOL_WINDOWS):
            half = w // 2
            lanes = slice(g * GROUP_W, (g + 1) * GROUP_W)

            def wsum(lo, hi):
                acc = edge_buf[e, base + lo:base + lo + POOL_HALO, lanes]
                for k in range(lo + 1, hi + 1):
                    acc = acc + edge_buf[e, base + k:base + k + POOL_HALO, lanes]
                return acc

            def count(lo, hi):
                a = jnp.clip(tpos + lo, 0, seq)
                b = jnp.clip(tpos + hi + 1, 0, seq)
                return (b - a).astype(F32)

            mean = 0.5 * (wsum(-half, half - 1) / count(-half, half - 1)
                          + wsum(-half + 1, half) / count(-half + 1, half))
            fixed = mean - edge_buf[e, base:base + POOL_HALO, lanes]
            d_buf[row0:row0 + POOL_HALO, lanes] = jnp.where(
                clipped, fixed, d_buf[row0:row0 + POOL_HALO, lanes])

    rows = slice(j0 * QBLK, j1 * QBLK)
    for g in range(len(POOL_WINDOWS)):
        lanes = slice(g * GROUP_W, (g + 1) * GROUP_W)
        y = jnp.dot(d_buf[rows, lanes].astype(BF16), poolw_ref[g], preferred_element_type=F32)
        mix_buf[rows, Q_WIDTH + g * GROUP_W:Q_WIDTH + (g + 1) * GROUP_W] = (
            y * pscale_ref[:, lanes]).astype(BF16)


def _rope_tables(seq):
    f32 = np.float32
    inv_freq = f32(ROPE_THETA) ** (-np.arange(0, ROTARY_DIM, 2, dtype=f32) / f32(ROTARY_DIM))
    ang = np.arange(seq, dtype=f32)[:, None] * inv_freq[None, :]
    cos, sin = np.cos(ang.astype(np.float64)).astype(f32), np.sin(ang.astype(np.float64)).astype(f32)
    half = ROTARY_DIM // 2
    rest = HEAD_DIM - ROTARY_DIM
    c = np.concatenate([cos, cos, np.ones((seq, rest), f32)], axis=1)
    sa = np.concatenate([-sin, np.zeros((seq, HEAD_DIM - half), f32)], axis=1)
    sb = np.concatenate([np.zeros((seq, half), f32), sin, np.zeros((seq, rest), f32)], axis=1)
    rep = LANES // HEAD_DIM
    return tuple(jnp.asarray(np.tile(t, (1, rep))) for t in (c, sa, sb))


def _band_bias(seq):
    r = np.arange(QBLK)[:, None]
    s = np.arange(KEYS)[None, :]
    band = (s - r >= 0) & (s - r <= 2 * WINDOW)
    first = band & (s >= QBLK)
    last = band & (s < 2 * QBLK)
    out = np.stack([np.tile(np.where(m, 0.0, NEG), (1, 2)) for m in (first, band, last)])
    return jnp.asarray(out, dtype=F32).astype(BF16)


def _pool_band():
    out = np.zeros((len(POOL_WINDOWS), QBLK, 2 * QBLK), np.float32)
    r = np.arange(QBLK)
    for g, w in enumerate(POOL_WINDOWS):
        half = w // 2
        for k in range(-half, half + 1):
            out[g, r, r + POOL_MARGIN + k] = (0.5 if abs(k) == half else 1.0) / w
        out[g, r, r + POOL_MARGIN] -= 1.0
    return jnp.asarray(out).astype(BF16)


def _const_spec(shape):
    nd = len(shape)
    return pl.BlockSpec(shape, lambda *_: (0,) * nd, pipeline_mode=pl.Buffered(1))


def kernel(x, ffn1_norm, ffn1_w_gate, ffn1_w_up, ffn1_w_down, mix_norm, w_in, sink_logits,
           pool_w, pool_scale, w_out, ffn2_norm, ffn2_w_gate, ffn2_w_up, ffn2_w_down, final_norm):
    B, S, D = x.shape
    assert D == D_MODEL and S % TM == 0 and TM % QBLK == 0 and ffn1_norm.shape[0] == 1
    assert S % TM_IN == 0 and TM_IN % SUB_ROWS == 0
    tm = TM
    grid = (B, S // tm)
    in_width = w_in.shape[-1]
    params = pltpu.CompilerParams(dimension_semantics=("arbitrary", "arbitrary"),
                                  vmem_limit_bytes=VMEM_LIMIT)

    c_tab, sa_tab, sb_tab = _rope_tables(S)
    row = lambda g: g.reshape(1, -1).astype(F32)
    tile_spec = lambda width, rows=tm: pl.BlockSpec((1, rows, width), lambda b, i: (b, i, 0))
    tm_in = TM_IN
    tab_spec = pl.BlockSpec((tm_in, LANES), lambda b, i: (i, 0))
    hbm_spec = pl.BlockSpec(memory_space=pl.ANY)

    h1, q, kt, v, pc = pl.pallas_call(
        _ffn_in_kernel,
        grid=(B, S // tm_in),
        in_specs=[
            tile_spec(D, tm_in),
            _const_spec((1, D)),
            hbm_spec, hbm_spec, hbm_spec,
            _const_spec((1, D)),
            hbm_spec,
            tab_spec, tab_spec, tab_spec,
        ],
        out_specs=[
            tile_spec(D, tm_in),
            tile_spec(Q_WIDTH, tm_in),
            pl.BlockSpec((1, KV_WIDTH, tm_in), lambda b, i: (b, 0, i)),
            tile_spec(KV_WIDTH, tm_in),
            tile_spec(POOL_WIDTH, tm_in),
        ],
        out_shape=[
            jax.ShapeDtypeStruct((B, S, D), F32),
            jax.ShapeDtypeStruct((B, S, Q_WIDTH), BF16),
            jax.ShapeDtypeStruct((B, KV_WIDTH, S), BF16),
            jax.ShapeDtypeStruct((B, S, KV_WIDTH), BF16),
            jax.ShapeDtypeStruct((B, S, POOL_WIDTH), F32),
        ],
        scratch_shapes=[
            pltpu.VMEM((D, D_FF), BF16), pltpu.VMEM((D, D_FF), BF16), pltpu.VMEM((D_FF, D), BF16),
            pltpu.VMEM((D, in_width), BF16),
        ],
        compiler_params=params,
        name="ffn1_inproj",
    )(x, row(ffn1_norm[0]), ffn1_w_gate[0], ffn1_w_up[0], ffn1_w_down[0], row(mix_norm[0]),
      w_in[0], c_tab, sa_tab, sb_tab)

    qb = tm // QBLK
    pb = tm // POOL_HALO
    n_qb = S // QBLK
    n_pb = S // POOL_HALO
    mix_spec = pl.BlockSpec
    prev_q = lambda i: jnp.maximum(i * qb - 1, 0)
    next_q = lambda i: jnp.minimum((i + 1) * qb, n_qb - 1)
    prev_p = lambda i: jnp.maximum(i * pb - 1, 0)
    next_p = lambda i: jnp.minimum((i + 1) * pb, n_pb - 1)

    out = pl.pallas_call(
        functools.partial(_mix_ffn_kernel, tm=tm, seq=S),
        grid=grid,
        in_specs=[
            pl.BlockSpec(memory_space=pltpu.SMEM),
            tile_spec(D),
            mix_spec((1, tm, Q_WIDTH), lambda b, i: (b, i, 0)),
            mix_spec((1, KV_WIDTH, QBLK), lambda b, i: (b, 0, prev_q(i))),
            mix_spec((1, KV_WIDTH, tm), lambda b, i: (b, 0, i)),
            mix_spec((1, KV_WIDTH, QBLK), lambda b, i: (b, 0, next_q(i))),
            mix_spec((1, QBLK, KV_WIDTH), lambda b, i: (b, prev_q(i), 0)),
            mix_spec((1, tm, KV_WIDTH), lambda b, i: (b, i, 0)),
            mix_spec((1, QBLK, KV_WIDTH), lambda b, i: (b, next_q(i), 0)),
            mix_spec((1, POOL_HALO, POOL_WIDTH), lambda b, i: (b, prev_p(i), 0)),
            mix_spec((1, tm, POOL_WIDTH), lambda b, i: (b, i, 0)),
            mix_spec((1, POOL_HALO, POOL_WIDTH), lambda b, i: (b, next_p(i), 0)),
            _const_spec((3, QBLK, 2 * KEYS)),
            _const_spec((QBLK, QBLK)),
            _const_spec((len(POOL_WINDOWS), QBLK, 2 * QBLK)),
            _const_spec((len(POOL_WINDOWS), GROUP_W, GROUP_W)),
            _const_spec((1, POOL_WIDTH)),
            hbm_spec,
            _const_spec((1, D)),
            hbm_spec, hbm_spec, hbm_spec,
            _const_spec((1, D)),
        ],
        out_specs=tile_spec(D),
        out_shape=jax.ShapeDtypeStruct((B, S, D), x.dtype),
        scratch_shapes=[
            pltpu.VMEM((KV_WIDTH, tm + 2 * QBLK), BF16),
            pltpu.VMEM((N_KV, 2, tm + 2 * QBLK, LANES), BF16),
            pltpu.VMEM((tm + 2 * POOL_MARGIN, 2 * POOL_WIDTH), BF16),
            pltpu.VMEM((2, 3 * POOL_HALO, POOL_WIDTH), F32),
            pltpu.VMEM((tm, POOL_WIDTH), F32),
            pltpu.VMEM((tm, D), BF16),
            pltpu.VMEM((D, D), BF16),
            pltpu.VMEM((D, D_FF), BF16), pltpu.VMEM((D, D_FF), BF16), pltpu.VMEM((D_FF, D), BF16),
        ],
        compiler_params=params,
        name="mix_ffn2",
    )(sink_logits[0].astype(F32), h1, q, kt, kt, kt, v, v, v, pc, pc, pc,
      _band_bias(S), jnp.asarray(np.eye(QBLK, dtype=np.float32)).astype(BF16), _pool_band(),
      pool_w[0].astype(BF16),
      row(pool_scale[0]), w_out[0],
      row(ffn2_norm[0]), ffn2_w_gate[0], ffn2_w_up[0], ffn2_w_down[0], row(final_norm))
    return out
```

```python
import functools

import numpy as np
import jax
import jax.numpy as jnp
from jax import lax
from jax.experimental import pallas as pl
from jax.experimental.pallas import tpu as pltpu

D_MODEL = 1024
HEAD_DIM = 64
N_HEADS = 8
N_KV = 2
Q_WIDTH = N_HEADS * HEAD_DIM
KV_WIDTH = N_KV * HEAD_DIM
WINDOW = 128
QBLK = 128
KEYS = 3 * QBLK
ROPE_THETA = 500000.0
ROTARY_DIM = HEAD_DIM // 4
POOL_WINDOWS = (2, 4, 8, 16)
POOL_HALO = 8
POOL_MARGIN = 64
POOL_WIDTH = 512
GROUP_W = 128
D_FF = 2816
EPS = 1e-6
LANES = 128
NEG = -1e30
LOG2E = 1.4426950408889634

TM = 512
TM_IN = 1024
SUB_ROWS = 512
CAST_ROWS_IN = 128
CAST_ROWS_MIX = 256
FF_CHUNK = 256
CAST_SLOTS = 3
VMEM_LIMIT = 58 * 1024 * 1024

F32 = jnp.float32
BF16 = jnp.bfloat16


def _rms(x, g):
    ms = jnp.mean(x * x, axis=-1, keepdims=True)
    return x * lax.rsqrt(ms + EPS) * g


def _gated(gate, up):
    return (gate * jax.nn.sigmoid(gate) * up).astype(BF16)


def _gate_up(xn, wg_ref, wu_ref):
    acts = []
    for c0 in range(0, D_FF, FF_CHUNK):
        cols = slice(c0, c0 + FF_CHUNK)
        gate = jnp.dot(xn, wg_ref[:, cols], preferred_element_type=F32)
        up = jnp.dot(xn, wu_ref[:, cols], preferred_element_type=F32)
        acts.append(_gated(gate, up))
    return jnp.concatenate(acts, axis=1)


def _load_weights_bf16(pairs, chunk_rows):
    ahead = CAST_SLOTS - 1
    counts = [src.shape[0] // chunk_rows for src, _ in pairs]
    bases = [sum(counts[:w]) for w in range(len(pairs))]
    assert all(n >= ahead for n in counts)

    def scoped(stage, sem):
        def chunk_copy(w, k, g):
            src = pairs[w][0]
            slot = lax.rem(g, CAST_SLOTS)
            return pltpu.make_async_copy(src.at[pl.ds(k * chunk_rows, chunk_rows)],
                                         stage.at[slot, :, pl.ds(0, src.shape[1])], sem.at[slot])

        for k in range(ahead):
            chunk_copy(0, k, k).start()

        for w, (src, dst) in enumerate(pairs):
            n, base, width = counts[w], bases[w], src.shape[1]

            def body(k, carry, w=w, n=n, base=base, width=width, dst=dst):
                @pl.when(k + ahead < n)
                def _():
                    chunk_copy(w, k + ahead, base + k + ahead).start()

                if w + 1 < len(pairs):
                    @pl.when(k + ahead >= n)
                    def _():
                        chunk_copy(w + 1, k + ahead - n, base + k + ahead).start()

                chunk_copy(w, k, base + k).wait()
                r0 = pl.multiple_of(k * chunk_rows, chunk_rows)
                slot = lax.rem(base + k, CAST_SLOTS)
                dst[pl.ds(r0, chunk_rows), :] = stage[slot, :, 0:width].astype(BF16)
                return carry

            lax.fori_loop(0, n, body, 0)

    pl.run_scoped(scoped, pltpu.VMEM((CAST_SLOTS, chunk_rows, D_FF), F32),
                  pltpu.SemaphoreType.DMA((CAST_SLOTS,)))


def _rope(t, c, sa, sb):
    t_plus = pltpu.roll(t, LANES - ROTARY_DIM // 2, 1)
    t_minus = pltpu.roll(t, ROTARY_DIM // 2, 1)
    return t * c + t_plus * sa + t_minus * sb


def _ffn_in_kernel(x_ref, g1_ref, wg_hbm, wu_hbm, wd_hbm, gm_ref, win_hbm,
                   c_ref, sa_ref, sb_ref,
                   h_ref, q_ref, kt_ref, v_ref, pc_ref,
                   wg_ref, wu_ref, wd_ref, win_ref):
    @pl.when((pl.program_id(0) == 0) & (pl.program_id(1) == 0))
    def _():
        _load_weights_bf16(((wg_hbm, wg_ref), (wu_hbm, wu_ref), (wd_hbm, wd_ref),
                            (win_hbm, win_ref)), CAST_ROWS_IN)

    tm = x_ref.shape[1]
    sub = SUB_ROWS
    scale = HEAD_DIM ** -0.5 * LOG2E
    def stage_norm(rows):
        return _rms(x_ref[0, rows], g1_ref[...]).astype(BF16)

    def stage_gate_up(xn):
        return _gate_up(xn, wg_ref, wu_ref)

    def stage_down(rows, act):
        h = x_ref[0, rows] + 0.5 * jnp.dot(act, wd_ref[...], preferred_element_type=F32)
        h_ref[0, rows] = h
        return _rms(h, gm_ref[...]).astype(BF16)

    def stage_proj(hn):
        return jnp.dot(hn, win_ref[...], preferred_element_type=F32)

    def stage_out(rows, u):
        c, sa, sb = c_ref[rows], sa_ref[rows], sb_ref[rows]
        for t in range(Q_WIDTH // LANES):
            qt = _rope(u[:, t * LANES:(t + 1) * LANES], c, sa, sb)
            q_ref[0, rows, t * LANES:(t + 1) * LANES] = (qt * scale).astype(BF16)
        k = _rope(u[:, Q_WIDTH:Q_WIDTH + KV_WIDTH], c, sa, sb)
        kt_ref[0, :, rows] = k.T.astype(BF16)
        v_ref[0, rows] = u[:, Q_WIDTH + KV_WIDTH:Q_WIDTH + 2 * KV_WIDTH].astype(BF16)
        pc_ref[0, rows] = u[:, Q_WIDTH + 2 * KV_WIDTH:]

    for pair in range(tm // (2 * sub)):
        ra = slice((2 * pair) * sub, (2 * pair + 1) * sub)
        rb = slice((2 * pair + 1) * sub, (2 * pair + 2) * sub)
        xn_a, xn_b = stage_norm(ra), stage_norm(rb)
        act_a = stage_gate_up(xn_a)
        act_b = stage_gate_up(xn_b)
        hn_a = stage_down(ra, act_a)
        hn_b = stage_down(rb, act_b)
        u_a = stage_proj(hn_a)
        u_b = stage_proj(hn_b)
        stage_out(ra, u_a)
        stage_out(rb, u_b)


def _mix_ffn_kernel(sink_ref, h_ref, q_ref, ktp_ref, ktc_ref, ktn_ref,
                    vp_ref, vc_ref, vn_ref, pcp_ref, pcc_ref, pcn_ref,
                    bias_ref, eye_ref, band_ref, poolw_ref, pscale_ref, wout_hbm,
                    g2_ref, wg_hbm, wu_hbm, wd_hbm, gf_ref,
                    o_ref,
                    kt_buf, vab_buf, hl_buf, edge_buf, d_buf, mix_buf,
                    wout_ref, wg_ref, wu_ref, wd_ref, *, tm, seq):
    @pl.when((pl.program_id(0) == 0) & (pl.program_id(1) == 0))
    def _():
        _load_weights_bf16(((wout_hbm, wout_ref), (wg_hbm, wg_ref), (wu_hbm, wu_ref),
                            (wd_hbm, wd_ref)), CAST_ROWS_MIX)

    i = pl.program_id(1)
    _prep_keys(ktp_ref, ktc_ref, ktn_ref, kt_buf, tm=tm)
    _prep_values(vp_ref, vc_ref, vn_ref, vab_buf, tm=tm)
    _prep_pool(pcp_ref, pcc_ref, pcn_ref, hl_buf, edge_buf, i, tm=tm, seq=seq)
    mix_rows = functools.partial(
        _mix_rows, sink_ref, q_ref, bias_ref, eye_ref, band_ref, poolw_ref, pscale_ref,
        kt_buf, vab_buf, hl_buf, edge_buf, d_buf, mix_buf, i, tm=tm, seq=seq)

    sub = tm // 2
    sub_blocks = sub // QBLK
    halves = (slice(0, sub), slice(sub, tm))

    def stage_out_proj(rows):
        h = h_ref[0, rows] + jnp.dot(mix_buf[rows], wout_ref[...], preferred_element_type=F32)
        return h, _rms(h, g2_ref[...]).astype(BF16)

    def stage_gate_up(hn):
        return _gate_up(hn, wg_ref, wu_ref)

    def stage_down(rows, h, act):
        h = h + 0.5 * jnp.dot(act, wd_ref[...], preferred_element_type=F32)
        o_ref[0, rows] = _rms(h, gf_ref[...])

    mix_rows(0, 2 * sub_blocks)
    hs = [stage_out_proj(rows) for rows in halves]
    acts = [stage_gate_up(hn) for _, hn in hs]
    for rows, (h, _), act in zip(halves, hs, acts):
        stage_down(rows, h, act)


def _prep_keys(ktp_ref, ktc_ref, ktn_ref, kt_buf, *, tm):
    kt_buf[:, 0:QBLK] = ktp_ref[0]
    kt_buf[:, QBLK:QBLK + tm] = ktc_ref[0]
    kt_buf[:, QBLK + tm:] = ktn_ref[0]


def _prep_values(vp_ref, vc_ref, vn_ref, vab_buf, *, tm):
    for lo, ref, n in ((0, vp_ref, QBLK), (QBLK, vc_ref, tm), (QBLK + tm, vn_ref, QBLK)):
        v = ref[0].astype(F32)
        vr = pltpu.roll(v, HEAD_DIM, 1)
        low = lax.broadcasted_iota(jnp.int32, v.shape, 1) < HEAD_DIM
        zero = jnp.zeros_like(v)
        vab_buf[0, 0, lo:lo + n] = jnp.where(low, v, zero).astype(BF16)
        vab_buf[0, 1, lo:lo + n] = jnp.where(low, zero, vr).astype(BF16)
        vab_buf[1, 0, lo:lo + n] = jnp.where(low, vr, zero).astype(BF16)
        vab_buf[1, 1, lo:lo + n] = jnp.where(low, zero, v).astype(BF16)


def _prep_pool(pcp_ref, pcc_ref, pcn_ref, hl_buf, edge_buf, i, *, tm, seq):
    n_tiles = seq // tm
    prev_halo = jnp.where(i > 0, pcp_ref[0], 0.0)
    next_halo = jnp.where(i < n_tiles - 1, pcn_ref[0], 0.0)
    zpad = jnp.zeros((POOL_MARGIN - POOL_HALO, POOL_WIDTH), F32)
    for r0, nr, u in ((0, POOL_MARGIN, jnp.concatenate([zpad, prev_halo], axis=0)),
                      (POOL_MARGIN, tm, pcc_ref[0]),
                      (POOL_MARGIN + tm, POOL_MARGIN, jnp.concatenate([next_halo, zpad], axis=0))):
        hi = u.astype(BF16)
        lo = (u - hi.astype(F32)).astype(BF16)
        for g in range(len(POOL_WINDOWS)):
            lanes = slice(g * GROUP_W, (g + 1) * GROUP_W)
            hl_buf[r0:r0 + nr, 2 * g * GROUP_W:(2 * g + 1) * GROUP_W] = hi[:, lanes]
            hl_buf[r0:r0 + nr, (2 * g + 1) * GROUP_W:(2 * g + 2) * GROUP_W] = lo[:, lanes]

    edge_buf[0, 0:POOL_HALO] = prev_halo
    edge_buf[0, POOL_HALO:] = pcc_ref[0, 0:2 * POOL_HALO]
    edge_buf[1, 0:2 * POOL_HALO] = pcc_ref[0, tm - 2 * POOL_HALO:tm]
    edge_buf[1, 2 * POOL_HALO:] = next_halo


def _mix_rows(sink_ref, q_ref, bias_ref, eye_ref, band_ref, poolw_ref, pscale_ref,
              kt_buf, vab_buf, hl_buf, edge_buf, d_buf, mix_buf, i, j0, j1, *, tm, seq):
    n_tiles = seq // tm
    nq = tm // QBLK
    nb = seq // QBLK

    def band_block(n):
        for g in range(len(POOL_WINDOWS)):
            win = hl_buf[n * QBLK:n * QBLK + 2 * QBLK, 2 * g * GROUP_W:(2 * g + 2) * GROUP_W]
            dd = jnp.dot(band_ref[g], win, preferred_element_type=F32)
            d_buf[n * QBLK:(n + 1) * QBLK, g * GROUP_W:(g + 1) * GROUP_W] = (
                dd[:, :GROUP_W] + dd[:, GROUP_W:])

    low_out = lax.broadcasted_iota(jnp.int32, (QBLK, LANES), 1) < HEAD_DIM
    eye = eye_ref[...]

    def scores(j, c):
        blk = i * nq + j
        variant = jnp.where(blk == 0, 0, jnp.where(blk == nb - 1, 2, 1))
        bias_t = bias_ref[variant]
        rows = slice(j * QBLK, (j + 1) * QBLK)
        keys = slice(j * QBLK, j * QBLK + KEYS)
        lhs = jnp.concatenate(
            [jnp.concatenate([q_ref[0, rows, (2 * c) * LANES:(2 * c + 1) * LANES], eye], axis=1),
             jnp.concatenate([q_ref[0, rows, (2 * c + 1) * LANES:(2 * c + 2) * LANES], eye],
                             axis=1)], axis=0)
        kc = kt_buf[c * HEAD_DIM:(c + 1) * HEAD_DIM, keys]
        z = jnp.zeros_like(kc)
        rhs = jnp.concatenate(
            [jnp.concatenate([kc, z], axis=1), jnp.concatenate([z, kc], axis=1), bias_t],
            axis=0)
        return jnp.dot(lhs, rhs, preferred_element_type=F32)

    def softmax(c, s):
        p_rows, inv_den = [], []
        for r in range(2):
            p_cols = []
            for par in range(2):
                sq = s[r * QBLK:(r + 1) * QBLK, par * KEYS:(par + 1) * KEYS]
                sink = sink_ref[4 * c + 2 * r + par] * LOG2E
                m = jnp.maximum(jnp.max(sq, axis=-1, keepdims=True), sink)
                p = jnp.exp2(sq - m)
                p_cols.append(p.astype(BF16))
                inv_den.append(1.0 / (jnp.sum(p, axis=-1, keepdims=True) + jnp.exp2(sink - m)))
            p_rows.append(jnp.concatenate(p_cols, axis=1))
        return jnp.concatenate(p_rows, axis=0), inv_den

    def weighted_values(j, c, pmat, inv_den):
        rows = slice(j * QBLK, (j + 1) * QBLK)
        keys = slice(j * QBLK, j * QBLK + KEYS)
        vrhs = jnp.concatenate([vab_buf[c, 0, keys], vab_buf[c, 1, keys]], axis=0)
        o = jnp.dot(pmat, vrhs, preferred_element_type=F32)
        for r in range(2):
            inv = jnp.where(low_out, inv_den[2 * r], inv_den[2 * r + 1])
            t = 2 * c + r
            mix_buf[rows, t * LANES:(t + 1) * LANES] = (
                o[r * QBLK:(r + 1) * QBLK] * inv).astype(BF16)

    units = [(j, c) for j in range(j0, j1) for c in range(N_KV)]
    for n in range(j0, j1):
        band_block(n)
    s_next = scores(*units[0])
    for n, (j, c) in enumerate(units):
        s_cur = s_next
        if n + 1 < len(units):
            s_next = scores(*units[n + 1])
        weighted_values(j, c, *softmax(c, s_cur))

    for e, (row0, clipped) in enumerate(((0, i == 0), (tm - POOL_HALO, i == n_tiles - 1))):
        if not j0 * QBLK <= row0 < j1 * QBLK:
            continue
        tpos = i * tm + row0 + lax.broadcasted_iota(jnp.int32, (POOL_HALO, GROUP_W), 0)
        base = POOL_HALO
        for g, w in enumerate(POOL_WINDOWS):
            half = w // 2
            lanes = slice(g * GROUP_W, (g + 1) * GROUP_W)

            def wsum(lo, hi):
                acc = edge_buf[e, base + lo:base + lo + POOL_HALO, lanes]
                for k in range(lo + 1, hi + 1):
                    acc = acc + edge_buf[e, base + k:base + k + POOL_HALO, lanes]
                return acc

            def count(lo, hi):
                a = jnp.clip(tpos + lo, 0, seq)
                b = jnp.clip(tpos + hi + 1, 0, seq)
                return (b - a).astype(F32)

            mean = 0.5 * (wsum(-half, half - 1) / count(-half, half - 1)
                          + wsum(-half + 1, half) / count(-half + 1, half))
            fixed = mean - edge_buf[e, base:base + POOL_HALO, lanes]
            d_buf[row0:row0 + POOL_HALO, lanes] = jnp.where(
                clipped, fixed, d_buf[row0:row0 + POOL_HALO, lanes])

    rows = slice(j0 * QBLK, j1 * QBLK)
    for g in range(len(POOL_WINDOWS)):
        lanes = slice(g * GROUP_W, (g + 1) * GROUP_W)
        y = jnp.dot(d_buf[rows, lanes].astype(BF16), poolw_ref[g], preferred_element_type=F32)
        mix_buf[rows, Q_WIDTH + g * GROUP_W:Q_WIDTH + (g + 1) * GROUP_W] = (
            y * pscale_ref[:, lanes]).astype(BF16)


def _rope_tables(seq):
    f32 = np.float32
    inv_freq = f32(ROPE_THETA) ** (-np.arange(0, ROTARY_DIM, 2, dtype=f32) / f32(ROTARY_DIM))
    ang = np.arange(seq, dtype=f32)[:, None] * inv_freq[None, :]
    cos, sin = np.cos(ang.astype(np.float64)).astype(f32), np.sin(ang.astype(np.float64)).astype(f32)
    half = ROTARY_DIM // 2
    rest = HEAD_DIM - ROTARY_DIM
    c = np.concatenate([cos, cos, np.ones((seq, rest), f32)], axis=1)
    sa = np.concatenate([-sin, np.zeros((seq, HEAD_DIM - half), f32)], axis=1)
    sb = np.concatenate([np.zeros((seq, half), f32), sin, np.zeros((seq, rest), f32)], axis=1)
    rep = LANES // HEAD_DIM
    return tuple(jnp.asarray(np.tile(t, (1, rep))) for t in (c, sa, sb))


def _band_bias(seq):
    r = np.arange(QBLK)[:, None]
    s = np.arange(KEYS)[None, :]
    band = (s - r >= 0) & (s - r <= 2 * WINDOW)
    first = band & (s >= QBLK)
    last = band & (s < 2 * QBLK)
    out = np.stack([np.tile(np.where(m, 0.0, NEG), (1, 2)) for m in (first, band, last)])
    return jnp.asarray(out, dtype=F32).astype(BF16)


def _pool_band():
    out = np.zeros((len(POOL_WINDOWS), QBLK, 2 * QBLK), np.float32)
    r = np.arange(QBLK)
    for g, w in enumerate(POOL_WINDOWS):
        half = w // 2
        for k in range(-half, half + 1):
            out[g, r, r + POOL_MARGIN + k] = (0.5 if abs(k) == half else 1.0) / w
        out[g, r, r + POOL_MARGIN] -= 1.0
    return jnp.asarray(out).astype(BF16)


def _const_spec(shape):
    nd = len(shape)
    return pl.BlockSpec(shape, lambda *_: (0,) * nd, pipeline_mode=pl.Buffered(1))


def kernel(x, ffn1_norm, ffn1_w_gate, ffn1_w_up, ffn1_w_down, mix_norm, w_in, sink_logits,
           pool_w, pool_scale, w_out, ffn2_norm, ffn2_w_gate, ffn2_w_up, ffn2_w_down, final_norm):
    B, S, D = x.shape
    assert D == D_MODEL and S % TM == 0 and TM % QBLK == 0 and ffn1_norm.shape[0] == 1
    assert S % TM_IN == 0 and TM_IN % SUB_ROWS == 0
    tm = TM
    grid = (B, S // tm)
    in_width = w_in.shape[-1]
    params = pltpu.CompilerParams(dimension_semantics=("arbitrary", "arbitrary"),
                                  vmem_limit_bytes=VMEM_LIMIT)

    c_tab, sa_tab, sb_tab = _rope_tables(S)
    row = lambda g: g.reshape(1, -1).astype(F32)
    tile_spec = lambda width, rows=tm: pl.BlockSpec((1, rows, width), lambda b, i: (b, i, 0))
    tm_in = TM_IN
    tab_spec = pl.BlockSpec((tm_in, LANES), lambda b, i: (i, 0))
    hbm_spec = pl.BlockSpec(memory_space=pl.ANY)

    h1, q, kt, v, pc = pl.pallas_call(
        _ffn_in_kernel,
        grid=(B, S // tm_in),
        in_specs=[
            tile_spec(D, tm_in),
            _const_spec((1, D)),
            hbm_spec, hbm_spec, hbm_spec,
            _const_spec((1, D)),
            hbm_spec,
            tab_spec, tab_spec, tab_spec,
        ],
        out_specs=[
            tile_spec(D, tm_in),
            tile_spec(Q_WIDTH, tm_in),
            pl.BlockSpec((1, KV_WIDTH, tm_in), lambda b, i: (b, 0, i)),
            tile_spec(KV_WIDTH, tm_in),
            tile_spec(POOL_WIDTH, tm_in),
        ],
        out_shape=[
            jax.ShapeDtypeStruct((B, S, D), F32),
            jax.ShapeDtypeStruct((B, S, Q_WIDTH), BF16),
            jax.ShapeDtypeStruct((B, KV_WIDTH, S), BF16),
            jax.ShapeDtypeStruct((B, S, KV_WIDTH), BF16),
            jax.ShapeDtypeStruct((B, S, POOL_WIDTH), F32),
        ],
        scratch_shapes=[
            pltpu.VMEM((D, D_FF), BF16), pltpu.VMEM((D, D_FF), BF16), pltpu.VMEM((D_FF, D), BF16),
            pltpu.VMEM((D, in_width), BF16),
        ],
        compiler_params=params,
        name="ffn1_inproj",
    )(x, row(ffn1_norm[0]), ffn1_w_gate[0], ffn1_w_up[0], ffn1_w_down[0], row(mix_norm[0]),
      w_in[0], c_tab, sa_tab, sb_tab)

    qb = tm // QBLK
    pb = tm // POOL_HALO
    n_qb = S // QBLK
    n_pb = S // POOL_HALO
    mix_spec = pl.BlockSpec
    prev_q = lambda i: jnp.maximum(i * qb - 1, 0)
    next_q = lambda i: jnp.minimum((i + 1) * qb, n_qb - 1)
    prev_p = lambda i: jnp.maximum(i * pb - 1, 0)
    next_p = lambda i: jnp.minimum((i + 1) * pb, n_pb - 1)

    out = pl.pallas_call(
        functools.partial(_mix_ffn_kernel, tm=tm, seq=S),
        grid=grid,
        in_specs=[
            pl.BlockSpec(memory_space=pltpu.SMEM),
            tile_spec(D),
            mix_spec((1, tm, Q_WIDTH), lambda b, i: (b, i, 0)),
            mix_spec((1, KV_WIDTH, QBLK), lambda b, i: (b, 0, prev_q(i))),
            mix_spec((1, KV_WIDTH, tm), lambda b, i: (b, 0, i)),
            mix_spec((1, KV_WIDTH, QBLK), lambda b, i: (b, 0, next_q(i))),
            mix_spec((1, QBLK, KV_WIDTH), lambda b, i: (b, prev_q(i), 0)),
            mix_spec((1, tm, KV_WIDTH), lambda b, i: (b, i, 0)),
            mix_spec((1, QBLK, KV_WIDTH), lambda b, i: (b, next_q(i), 0)),
            mix_spec((1, POOL_HALO, POOL_WIDTH), lambda b, i: (b, prev_p(i), 0)),
            mix_spec((1, tm, POOL_WIDTH), lambda b, i: (b, i, 0)),
            mix_spec((1, POOL_HALO, POOL_WIDTH), lambda b, i: (b, next_p(i), 0)),
            _const_spec((3, QBLK, 2 * KEYS)),
            _const_spec((QBLK, QBLK)),
            _const_spec((len(POOL_WINDOWS), QBLK, 2 * QBLK)),
            _const_spec((len(POOL_WINDOWS), GROUP_W, GROUP_W)),
            _const_spec((1, POOL_WIDTH)),
            hbm_spec,
            _const_spec((1, D)),
            hbm_spec, hbm_spec, hbm_spec,
            _const_spec((1, D)),
        ],
        out_specs=tile_spec(D),
        out_shape=jax.ShapeDtypeStruct((B, S, D), x.dtype),
        scratch_shapes=[
            pltpu.VMEM((KV_WIDTH, tm + 2 * QBLK), BF16),
            pltpu.VMEM((N_KV, 2, tm + 2 * QBLK, LANES), BF16),
            pltpu.VMEM((tm + 2 * POOL_MARGIN, 2 * POOL_WIDTH), BF16),
            pltpu.VMEM((2, 3 * POOL_HALO, POOL_WIDTH), F32),
            pltpu.VMEM((tm, POOL_WIDTH), F32),
            pltpu.VMEM((tm, D), BF16),
            pltpu.VMEM((D, D), BF16),
            pltpu.VMEM((D, D_FF), BF16), pltpu.VMEM((D, D_FF), BF16), pltpu.VMEM((D_FF, D), BF16),
        ],
        compiler_params=params,
        name="mix_ffn2",
    )(sink_logits[0].astype(F32), h1, q, kt, kt, kt, v, v, v, pc, pc, pc,
      _band_bias(S), jnp.asarray(np.eye(QBLK, dtype=np.float32)).astype(BF16), _pool_band(),
      pool_w[0].astype(BF16),
      row(pool_scale[0]), w_out[0],
      row(ffn2_norm[0]), ffn2_w_gate[0], ffn2_w_up[0], ffn2_w_down[0], row(final_norm))
    return out
```

```python
import functools

import numpy as np
import jax
import jax.numpy as jnp
from jax import lax
from jax.experimental import pallas as pl
from jax.experimental.pallas import tpu as pltpu

D_MODEL = 1024
HEAD_DIM = 64
N_HEADS = 8
N_KV = 2
Q_WIDTH = N_HEADS * HEAD_DIM
KV_WIDTH = N_KV * HEAD_DIM
WINDOW = 128
QBLK = 128
KEYS = 3 * QBLK
ROPE_THETA = 500000.0
ROTARY_DIM = HEAD_DIM // 4
POOL_WINDOWS = (2, 4, 8, 16)
POOL_HALO = 8
POOL_MARGIN = 64
POOL_WIDTH = 512
GROUP_W = 128
D_FF = 2816
EPS = 1e-6
LANES = 128
NEG = -1e30
LOG2E = 1.4426950408889634

TM = 512
TM_IN = 1024
SUB_ROWS = 256
CAST_ROWS_IN = 128
CAST_ROWS_MIX = 256
FF_CHUNK = 256
CAST_SLOTS = 3
VMEM_LIMIT = 58 * 1024 * 1024

F32 = jnp.float32
BF16 = jnp.bfloat16


def _rms(x, g):
    ms = jnp.mean(x * x, axis=-1, keepdims=True)
    return x * lax.rsqrt(ms + EPS) * g


def _gated(gate, up):
    return (gate * jax.nn.sigmoid(gate) * up).astype(BF16)


def _gate_up(xn, wg_ref, wu_ref):
    acts = []
    for c0 in range(0, D_FF, FF_CHUNK):
        cols = slice(c0, c0 + FF_CHUNK)
        gate = jnp.dot(xn, wg_ref[:, cols], preferred_element_type=F32)
        up = jnp.dot(xn, wu_ref[:, cols], preferred_element_type=F32)
        acts.append(_gated(gate, up))
    return jnp.concatenate(acts, axis=1)


def _load_weights_bf16(pairs, chunk_rows):
    ahead = CAST_SLOTS - 1
    counts = [src.shape[0] // chunk_rows for src, _ in pairs]
    bases = [sum(counts[:w]) for w in range(len(pairs))]
    assert all(n >= ahead for n in counts)

    def scoped(stage, sem):
        def chunk_copy(w, k, g):
            src = pairs[w][0]
            slot = lax.rem(g, CAST_SLOTS)
            return pltpu.make_async_copy(src.at[pl.ds(k * chunk_rows, chunk_rows)],
                                         stage.at[slot, :, pl.ds(0, src.shape[1])], sem.at[slot])

        for k in range(ahead):
            chunk_copy(0, k, k).start()

        for w, (src, dst) in enumerate(pairs):
            n, base, width = counts[w], bases[w], src.shape[1]

            def body(k, carry, w=w, n=n, base=base, width=width, dst=dst):
                @pl.when(k + ahead < n)
                def _():
                    chunk_copy(w, k + ahead, base + k + ahead).start()

                if w + 1 < len(pairs):
                    @pl.when(k + ahead >= n)
                    def _():
                        chunk_copy(w + 1, k + ahead - n, base + k + ahead).start()

                chunk_copy(w, k, base + k).wait()
                r0 = pl.multiple_of(k * chunk_rows, chunk_rows)
                slot = lax.rem(base + k, CAST_SLOTS)
                dst[pl.ds(r0, chunk_rows), :] = stage[slot, :, 0:width].astype(BF16)
                return carry

            lax.fori_loop(0, n, body, 0)

    pl.run_scoped(scoped, pltpu.VMEM((CAST_SLOTS, chunk_rows, D_FF), F32),
                  pltpu.SemaphoreType.DMA((CAST_SLOTS,)))


def _rope(t, c, sa, sb):
    t_plus = pltpu.roll(t, LANES - ROTARY_DIM // 2, 1)
    t_minus = pltpu.roll(t, ROTARY_DIM // 2, 1)
    return t * c + t_plus * sa + t_minus * sb


def _ffn_in_kernel(x_ref, g1_ref, wg_hbm, wu_hbm, wd_hbm, gm_ref, win_hbm,
                   c_ref, sa_ref, sb_ref,
                   h_ref, q_ref, kt_ref, v_ref, pc_ref,
                   wg_ref, wu_ref, wd_ref, win_ref):
    @pl.when((pl.program_id(0) == 0) & (pl.program_id(1) == 0))
    def _():
        _load_weights_bf16(((wg_hbm, wg_ref), (wu_hbm, wu_ref), (wd_hbm, wd_ref),
                            (win_hbm, win_ref)), CAST_ROWS_IN)

    tm = x_ref.shape[1]
    sub = SUB_ROWS
    scale = HEAD_DIM ** -0.5 * LOG2E
    def stage_norm(rows):
        return _rms(x_ref[0, rows], g1_ref[...]).astype(BF16)

    def stage_gate_up(xn):
        return _gate_up(xn, wg_ref, wu_ref)

    def stage_down(rows, act):
        h = x_ref[0, rows] + 0.5 * jnp.dot(act, wd_ref[...], preferred_element_type=F32)
        h_ref[0, rows] = h
        return _rms(h, gm_ref[...]).astype(BF16)

    def stage_proj(hn):
        return jnp.dot(hn, win_ref[...], preferred_element_type=F32)

    def stage_out(rows, u):
        c, sa, sb = c_ref[rows], sa_ref[rows], sb_ref[rows]
        for t in range(Q_WIDTH // LANES):
            qt = _rope(u[:, t * LANES:(t + 1) * LANES], c, sa, sb)
            q_ref[0, rows, t * LANES:(t + 1) * LANES] = (qt * scale).astype(BF16)
        k = _rope(u[:, Q_WIDTH:Q_WIDTH + KV_WIDTH], c, sa, sb)
        kt_ref[0, :, rows] = k.T.astype(BF16)
        v_ref[0, rows] = u[:, Q_WIDTH + KV_WIDTH:Q_WIDTH + 2 * KV_WIDTH].astype(BF16)
        pc_ref[0, rows] = u[:, Q_WIDTH + 2 * KV_WIDTH:]

    for pair in range(tm // (2 * sub)):
        ra = slice((2 * pair) * sub, (2 * pair + 1) * sub)
        rb = slice((2 * pair + 1) * sub, (2 * pair + 2) * sub)
        xn_a, xn_b = stage_norm(ra), stage_norm(rb)
        act_a = stage_gate_up(xn_a)
        act_b = stage_gate_up(xn_b)
        hn_a = stage_down(ra, act_a)
        hn_b = stage_down(rb, act_b)
        u_a = stage_proj(hn_a)
        u_b = stage_proj(hn_b)
        stage_out(ra, u_a)
        stage_out(rb, u_b)


def _mix_ffn_kernel(sink_ref, h_ref, q_ref, ktp_ref, ktc_ref, ktn_ref,
                    vp_ref, vc_ref, vn_ref, pcp_ref, pcc_ref, pcn_ref,
                    bias_ref, eye_ref, band_ref, poolw_ref, pscale_ref, wout_hbm,
                    g2_ref, wg_hbm, wu_hbm, wd_hbm, gf_ref,
                    o_ref,
                    kt_buf, vab_buf, hl_buf, edge_buf, d_buf, mix_buf,
                    wout_ref, wg_ref, wu_ref, wd_ref, *, tm, seq):
    @pl.when((pl.program_id(0) == 0) & (pl.program_id(1) == 0))
    def _():
        _load_weights_bf16(((wout_hbm, wout_ref), (wg_hbm, wg_ref), (wu_hbm, wu_ref),
                            (wd_hbm, wd_ref)), CAST_ROWS_MIX)

    i = pl.program_id(1)
    _prep_keys(ktp_ref, ktc_ref, ktn_ref, kt_buf, tm=tm)
    _prep_values(vp_ref, vc_ref, vn_ref, vab_buf, tm=tm)
    _prep_pool(pcp_ref, pcc_ref, pcn_ref, hl_buf, edge_buf, i, tm=tm, seq=seq)
    mix_rows = functools.partial(
        _mix_rows, sink_ref, q_ref, bias_ref, eye_ref, band_ref, poolw_ref, pscale_ref,
        kt_buf, vab_buf, hl_buf, edge_buf, d_buf, mix_buf, i, tm=tm, seq=seq)

    sub = tm // 2
    sub_blocks = sub // QBLK
    halves = (slice(0, sub), slice(sub, tm))

    def stage_out_proj(rows):
        h = h_ref[0, rows] + jnp.dot(mix_buf[rows], wout_ref[...], preferred_element_type=F32)
        return h, _rms(h, g2_ref[...]).astype(BF16)

    def stage_gate_up(hn):
        return _gate_up(hn, wg_ref, wu_ref)

    def stage_down(rows, h, act):
        h = h + 0.5 * jnp.dot(act, wd_ref[...], preferred_element_type=F32)
        o_ref[0, rows] = _rms(h, gf_ref[...])

    mix_rows(0, 2 * sub_blocks)
    hs = [stage_out_proj(rows) for rows in halves]
    acts = [stage_gate_up(hn) for _, hn in hs]
    for rows, (h, _), act in zip(halves, hs, acts):
        stage_down(rows, h, act)


def _prep_keys(ktp_ref, ktc_ref, ktn_ref, kt_buf, *, tm):
    kt_buf[:, 0:QBLK] = ktp_ref[0]
    kt_buf[:, QBLK:QBLK + tm] = ktc_ref[0]
    kt_buf[:, QBLK + tm:] = ktn_ref[0]


def _prep_values(vp_ref, vc_ref, vn_ref, vab_buf, *, tm):
    for lo, ref, n in ((0, vp_ref, QBLK), (QBLK, vc_ref, tm), (QBLK + tm, vn_ref, QBLK)):
        v = ref[0].astype(F32)
        vr = pltpu.roll(v, HEAD_DIM, 1)
        low = lax.broadcasted_iota(jnp.int32, v.shape, 1) < HEAD_DIM
        zero = jnp.zeros_like(v)
        vab_buf[0, 0, lo:lo + n] = jnp.where(low, v, zero).astype(BF16)
        vab_buf[0, 1, lo:lo + n] = jnp.where(low, zero, vr).astype(BF16)
        vab_buf[1, 0, lo:lo + n] = jnp.where(low, vr, zero).astype(BF16)
        vab_buf[1, 1, lo:lo + n] = jnp.where(low, zero, v).astype(BF16)


def _prep_pool(pcp_ref, pcc_ref, pcn_ref, hl_buf, edge_buf, i, *, tm, seq):
    n_tiles = seq // tm
    prev_halo = jnp.where(i > 0, pcp_ref[0], 0.0)
    next_halo = jnp.where(i < n_tiles - 1, pcn_ref[0], 0.0)
    zpad = jnp.zeros((POOL_MARGIN - POOL_HALO, POOL_WIDTH), F32)
    for r0, nr, u in ((0, POOL_MARGIN, jnp.concatenate([zpad, prev_halo], axis=0)),
                      (POOL_MARGIN, tm, pcc_ref[0]),
                      (POOL_MARGIN + tm, POOL_MARGIN, jnp.concatenate([next_halo, zpad], axis=0))):
        hi = u.astype(BF16)
        lo = (u - hi.astype(F32)).astype(BF16)
        for g in range(len(POOL_WINDOWS)):
            lanes = slice(g * GROUP_W, (g + 1) * GROUP_W)
            hl_buf[r0:r0 + nr, 2 * g * GROUP_W:(2 * g + 1) * GROUP_W] = hi[:, lanes]
            hl_buf[r0:r0 + nr, (2 * g + 1) * GROUP_W:(2 * g + 2) * GROUP_W] = lo[:, lanes]

    edge_buf[0, 0:POOL_HALO] = prev_halo
    edge_buf[0, POOL_HALO:] = pcc_ref[0, 0:2 * POOL_HALO]
    edge_buf[1, 0:2 * POOL_HALO] = pcc_ref[0, tm - 2 * POOL_HALO:tm]
    edge_buf[1, 2 * POOL_HALO:] = next_halo


def _mix_rows(sink_ref, q_ref, bias_ref, eye_ref, band_ref, poolw_ref, pscale_ref,
              kt_buf, vab_buf, hl_buf, edge_buf, d_buf, mix_buf, i, j0, j1, *, tm, seq):
    n_tiles = seq // tm
    nq = tm // QBLK
    nb = seq // QBLK

    def band_block(n):
        for g in range(len(POOL_WINDOWS)):
            win = hl_buf[n * QBLK:n * QBLK + 2 * QBLK, 2 * g * GROUP_W:(2 * g + 2) * GROUP_W]
            dd = jnp.dot(band_ref[g], win, preferred_element_type=F32)
            d_buf[n * QBLK:(n + 1) * QBLK, g * GROUP_W:(g + 1) * GROUP_W] = (
                dd[:, :GROUP_W] + dd[:, GROUP_W:])

    low_out = lax.broadcasted_iota(jnp.int32, (QBLK, LANES), 1) < HEAD_DIM
    eye = eye_ref[...]

    def scores(j, c):
        blk = i * nq + j
        variant = jnp.where(blk == 0, 0, jnp.where(blk == nb - 1, 2, 1))
        bias_t = bias_ref[variant]
        rows = slice(j * QBLK, (j + 1) * QBLK)
        keys = slice(j * QBLK, j * QBLK + KEYS)
        lhs = jnp.concatenate(
            [jnp.concatenate([q_ref[0, rows, (2 * c) * LANES:(2 * c + 1) * LANES], eye], axis=1),
             jnp.concatenate([q_ref[0, rows, (2 * c + 1) * LANES:(2 * c + 2) * LANES], eye],
                             axis=1)], axis=0)
        kc = kt_buf[c * HEAD_DIM:(c + 1) * HEAD_DIM, keys]
        z = jnp.zeros_like(kc)
        rhs = jnp.concatenate(
            [jnp.concatenate([kc, z], axis=1), jnp.concatenate([z, kc], axis=1), bias_t],
            axis=0)
        return jnp.dot(lhs, rhs, preferred_element_type=F32)

    def softmax(c, s):
        p_rows, inv_den = [], []
        for r in range(2):
            p_cols = []
            for par in range(2):
                sq = s[r * QBLK:(r + 1) * QBLK, par * KEYS:(par + 1) * KEYS]
                sink = sink_ref[4 * c + 2 * r + par] * LOG2E
                m = jnp.maximum(jnp.max(sq, axis=-1, keepdims=True), sink)
                p = jnp.exp2(sq - m)
                p_cols.append(p.astype(BF16))
                inv_den.append(1.0 / (jnp.sum(p, axis=-1, keepdims=True) + jnp.exp2(sink - m)))
            p_rows.append(jnp.concatenate(p_cols, axis=1))
        return jnp.concatenate(p_rows, axis=0), inv_den

    def weighted_values(j, c, pmat, inv_den):
        rows = slice(j * QBLK, (j + 1) * QBLK)
        keys = slice(j * QBLK, j * QBLK + KEYS)
        vrhs = jnp.concatenate([vab_buf[c, 0, keys], vab_buf[c, 1, keys]], axis=0)
        o = jnp.dot(pmat, vrhs, preferred_element_type=F32)
        for r in range(2):
            inv = jnp.where(low_out, inv_den[2 * r], inv_den[2 * r + 1])
            t = 2 * c + r
            mix_buf[rows, t * LANES:(t + 1) * LANES] = (
                o[r * QBLK:(r + 1) * QBLK] * inv).astype(BF16)

    units = [(j, c) for j in range(j0, j1) for c in range(N_KV)]
    for n in range(j0, j1):
        band_block(n)
    s_next = scores(*units[0])
    for n, (j, c) in enumerate(units):
        s_cur = s_next
        if n + 1 < len(units):
            s_next = scores(*units[n + 1])
        weighted_values(j, c, *softmax(c, s_cur))

    for e, (row0, clipped) in enumerate(((0, i == 0), (tm - POOL_HALO, i == n_tiles - 1))):
        if not j0 * QBLK <= row0 < j1 * QBLK:
            continue
        tpos = i * tm + row0 + lax.broadcasted_iota(jnp.int32, (POOL_HALO, GROUP_W), 0)
        base = POOL_HALO
        for g, w in enumerate(POOL_WINDOWS):
            half = w // 2
            lanes = slice(g * GROUP_W, (g + 1) * GROUP_W)

            def wsum(lo, hi):
                acc = edge_buf[e, base + lo:base + lo + POOL_HALO, lanes]
                for k in range(lo + 1, hi + 1):
                    acc = acc + edge_buf[e, base + k:base + k + POOL_HALO, lanes]
                return acc

            def count(lo, hi):
                a = jnp.clip(tpos + lo, 0, seq)
                b = jnp.clip(tpos + hi + 1, 0, seq)
                return (b - a).astype(F32)

            mean = 0.5 * (wsum(-half, half - 1) / count(-half, half - 1)
                          + wsum(-half + 1, half) / count(-half + 1, half))
            fixed = mean - edge_buf[e, base:base + POOL_HALO, lanes]
            d_buf[row0:row0 + POOL_HALO, lanes] = jnp.where(
                clipped, fixed, d_buf[row0:row0 + POOL_HALO, lanes])

    rows = slice(j0 * QBLK, j1 * QBLK)
    for g in range(len(POOL_WINDOWS)):
        lanes = slice(g * GROUP_W, (g + 1) * GROUP_W)
        y = jnp.dot(d_buf[rows, lanes].astype(BF16), poolw_ref[g], preferred_element_type=F32)
        mix_buf[rows, Q_WIDTH + g * GROUP_W:Q_WIDTH + (g + 1) * GROUP_W] = (
            y * pscale_ref[:, lanes]).astype(BF16)


def _rope_tables(seq):
    f32 = np.float32
    inv_freq = f32(ROPE_THETA) ** (-np.arange(0, ROTARY_DIM, 2, dtype=f32) / f32(ROTARY_DIM))
    ang = np.arange(seq, dtype=f32)[:, None] * inv_freq[None, :]
    cos, sin = np.cos(ang.astype(np.float64)).astype(f32), np.sin(ang.astype(np.float64)).astype(f32)
    half = ROTARY_DIM // 2
    rest = HEAD_DIM - ROTARY_DIM
    c = np.concatenate([cos, cos, np.ones((seq, rest), f32)], axis=1)
    sa = np.concatenate([-sin, np.zeros((seq, HEAD_DIM - half), f32)], axis=1)
    sb = np.concatenate([np.zeros((seq, half), f32), sin, np.zeros((seq, rest), f32)], axis=1)
    rep = LANES // HEAD_DIM
    return tuple(jnp.asarray(np.tile(t, (1, rep))) for t in (c, sa, sb))


def _band_bias(seq):
    r = np.arange(QBLK)[:, None]
    s = np.arange(KEYS)[None, :]
    band = (s - r >= 0) & (s - r <= 2 * WINDOW)
    first = band & (s >= QBLK)
    last = band & (s < 2 * QBLK)
    out = np.stack([np.tile(np.where(m, 0.0, NEG), (1, 2)) for m in (first, band, last)])
    return jnp.asarray(out, dtype=F32).astype(BF16)


def _pool_band():
    out = np.zeros((len(POOL_WINDOWS), QBLK, 2 * QBLK), np.float32)
    r = np.arange(QBLK)
    for g, w in enumerate(POOL_WINDOWS):
        half = w // 2
        for k in range(-half, half + 1):
            out[g, r, r + POOL_MARGIN + k] = (0.5 if abs(k) == half else 1.0) / w
        out[g, r, r + POOL_MARGIN] -= 1.0
    return jnp.asarray(out).astype(BF16)


def _const_spec(shape):
    nd = len(shape)
    return pl.BlockSpec(shape, lambda *_: (0,) * nd, pipeline_mode=pl.Buffered(1))


def kernel(x, ffn1_norm, ffn1_w_gate, ffn1_w_up, ffn1_w_down, mix_norm, w_in, sink_logits,
           pool_w, pool_scale, w_out, ffn2_norm, ffn2_w_gate, ffn2_w_up, ffn2_w_down, final_norm):
    B, S, D = x.shape
    assert D == D_MODEL and S % TM == 0 and TM % QBLK == 0 and ffn1_norm.shape[0] == 1
    assert S % TM_IN == 0 and TM_IN % SUB_ROWS == 0
    tm = TM
    grid = (B, S // tm)
    in_width = w_in.shape[-1]
    params = pltpu.CompilerParams(dimension_semantics=("arbitrary", "arbitrary"),
                                  vmem_limit_bytes=VMEM_LIMIT)

    c_tab, sa_tab, sb_tab = _rope_tables(S)
    row = lambda g: g.reshape(1, -1).astype(F32)
    tile_spec = lambda width, rows=tm: pl.BlockSpec((1, rows, width), lambda b, i: (b, i, 0))
    tm_in = TM_IN
    tab_spec = pl.BlockSpec((tm_in, LANES), lambda b, i: (i, 0))
    hbm_spec = pl.BlockSpec(memory_space=pl.ANY)

    h1, q, kt, v, pc = pl.pallas_call(
        _ffn_in_kernel,
        grid=(B, S // tm_in),
        in_specs=[
            tile_spec(D, tm_in),
            _const_spec((1, D)),
            hbm_spec, hbm_spec, hbm_spec,
            _const_spec((1, D)),
            hbm_spec,
            tab_spec, tab_spec, tab_spec,
        ],
        out_specs=[
            tile_spec(D, tm_in),
            tile_spec(Q_WIDTH, tm_in),
            pl.BlockSpec((1, KV_WIDTH, tm_in), lambda b, i: (b, 0, i)),
            tile_spec(KV_WIDTH, tm_in),
            tile_spec(POOL_WIDTH, tm_in),
        ],
        out_shape=[
            jax.ShapeDtypeStruct((B, S, D), F32),
            jax.ShapeDtypeStruct((B, S, Q_WIDTH), BF16),
            jax.ShapeDtypeStruct((B, KV_WIDTH, S), BF16),
            jax.ShapeDtypeStruct((B, S, KV_WIDTH), BF16),
            jax.ShapeDtypeStruct((B, S, POOL_WIDTH), F32),
        ],
        scratch_shapes=[
            pltpu.VMEM((D, D_FF), BF16), pltpu.VMEM((D, D_FF), BF16), pltpu.VMEM((D_FF, D), BF16),
            pltpu.VMEM((D, in_width), BF16),
        ],
        compiler_params=params,
        name="ffn1_inproj",
    )(x, row(ffn1_norm[0]), ffn1_w_gate[0], ffn1_w_up[0], ffn1_w_down[0], row(mix_norm[0]),
      w_in[0], c_tab, sa_tab, sb_tab)

    qb = tm // QBLK
    pb = tm // POOL_HALO
    n_qb = S // QBLK
    n_pb = S // POOL_HALO
    prev_q = lambda i: jnp.maximum(i * qb - 1, 0)
    next_q = lambda i: jnp.minimum((i + 1) * qb, n_qb - 1)
    prev_p = lambda i: jnp.maximum(i * pb - 1, 0)
    next_p = lambda i: jnp.minimum((i + 1) * pb, n_pb - 1)

    out = pl.pallas_call(
        functools.partial(_mix_ffn_kernel, tm=tm, seq=S),
        grid=grid,
        in_specs=[
            pl.BlockSpec(memory_space=pltpu.SMEM),
            tile_spec(D),
            pl.BlockSpec((1, tm, Q_WIDTH), lambda b, i: (b, i, 0)),
            pl.BlockSpec((1, KV_WIDTH, QBLK), lambda b, i: (b, 0, prev_q(i))),
            pl.BlockSpec((1, KV_WIDTH, tm), lambda b, i: (b, 0, i)),
            pl.BlockSpec((1, KV_WIDTH, QBLK), lambda b, i: (b, 0, next_q(i))),
            pl.BlockSpec((1, QBLK, KV_WIDTH), lambda b, i: (b, prev_q(i), 0)),
            pl.BlockSpec((1, tm, KV_WIDTH), lambda b, i: (b, i, 0)),
            pl.BlockSpec((1, QBLK, KV_WIDTH), lambda b, i: (b, next_q(i), 0)),
            pl.BlockSpec((1, POOL_HALO, POOL_WIDTH), lambda b, i: (b, prev_p(i), 0)),
            pl.BlockSpec((1, tm, POOL_WIDTH), lambda b, i: (b, i, 0)),
            pl.BlockSpec((1, POOL_HALO, POOL_WIDTH), lambda b, i: (b, next_p(i), 0)),
            _const_spec((3, QBLK, 2 * KEYS)),
            _const_spec((QBLK, QBLK)),
            _const_spec((len(POOL_WINDOWS), QBLK, 2 * QBLK)),
            _const_spec((len(POOL_WINDOWS), GROUP_W, GROUP_W)),
            _const_spec((1, POOL_WIDTH)),
            hbm_spec,
            _const_spec((1, D)),
            hbm_spec, hbm_spec, hbm_spec,
            _const_spec((1, D)),
        ],
        out_specs=tile_spec(D),
        out_shape=jax.ShapeDtypeStruct((B, S, D), x.dtype),
        scratch_shapes=[
            pltpu.VMEM((KV_WIDTH, tm + 2 * QBLK), BF16),
            pltpu.VMEM((N_KV, 2, tm + 2 * QBLK, LANES), BF16),
            pltpu.VMEM((tm + 2 * POOL_MARGIN, 2 * POOL_WIDTH), BF16),
            pltpu.VMEM((2, 3 * POOL_HALO, POOL_WIDTH), F32),
            pltpu.VMEM((tm, POOL_WIDTH), F32),
            pltpu.VMEM((tm, D), BF16),
            pltpu.VMEM((D, D), BF16),
            pltpu.VMEM((D, D_FF), BF16), pltpu.VMEM((D, D_FF), BF16), pltpu.VMEM((D_FF, D), BF16),
        ],
        compiler_params=params,
        name="mix_ffn2",
    )(sink_logits[0].astype(F32), h1, q, kt, kt, kt, v, v, v, pc, pc, pc,
      _band_bias(S), jnp.asarray(np.eye(QBLK, dtype=np.float32)).astype(BF16), _pool_band(),
      pool_w[0].astype(BF16),
      row(pool_scale[0]), w_out[0],
      row(ffn2_norm[0]), ffn2_w_gate[0], ffn2_w_up[0], ffn2_w_down[0], row(final_norm))
    return out
```
